```python
import jax, jax.numpy as jnp
from jax import lax
import numpy as np

D_MODEL = 1024
BATCH = 8
SEQ = 2048
DEPTH = 4
DEC_BATCH = 32
DEC_SEQ = 1
PAST_LEN = 8192
PAGE_SIZE = 128

HEAD_DIM = 64
SB_W = D_MODEL // 2
SB_HEADS = SB_W // HEAD_DIM
SB_BIAS_INIT = -8.0
LRU_W = D_MODEL // 4
LRU_BLOCKS = LRU_W // HEAD_DIM
LRU_CONV = 4
LRU_C = 8.0
GLA_W = D_MODEL // 4
GLA_HEADS = GLA_W // HEAD_DIM
GLA_DV = HEAD_DIM
GLA_DK = HEAD_DIM // 2
GLA_RANK = 16
GLA_TAU = 16.0
GLA_CHUNK = 16
D_MIX = SB_W + LRU_W + GLA_W
SPLITS = [SB_W, SB_W, SB_W, LRU_W, LRU_W, GLA_HEADS * GLA_DK, GLA_HEADS * GLA_DK, GLA_W, GLA_W, GLA_RANK]
D_IN = sum(SPLITS)
D_FF = 2816
Q_BLOCK = 128
EPS = 1e-6

kernel_name = 'stick_lru_gla_hymba_macaron'


def rmsnorm(x, g):
    xf = x.astype(jnp.float32)
    y = xf * lax.rsqrt(jnp.mean(xf * xf, axis=-1, keepdims=True) + EPS)
    return (y * g.astype(jnp.float32)).astype(x.dtype)


def swiglu_ffn(x, g, wg, wu, wd):
    h = rmsnorm(x, g)
    return (jax.nn.silu(h @ wg) * (h @ wu)) @ wd


def sb_block(q, q_pos, k, v, k_pos, bias):
    z = jnp.einsum('bqhd,bkhd->bhqk', q, k).astype(jnp.float32) * (HEAD_DIM ** -0.5)
    z = z + bias.astype(jnp.float32)[None, :, None, None]
    mask = (k_pos[None, :] < q_pos[:, None])[None, None]
    log1m = jnp.where(mask, -jax.nn.softplus(z), 0.0)
    after = lax.cumsum(log1m, axis=3, reverse=True) - log1m
    a = jnp.where(mask, jnp.exp(jax.nn.log_sigmoid(z) + after), 0.0)
    return jnp.einsum('bhqk,bkhd->bqhd', a.astype(v.dtype), v)


def stick_breaking(q, k, v, q_pos, k_pos, bias):
    b, L, h, d = q.shape
    qb = Q_BLOCK if L % Q_BLOCK == 0 else L
    n = L // qb
    qs = q.reshape(b, n, qb, h, d).swapaxes(0, 1)
    ps = q_pos.reshape(n, qb)
    out = lax.map(lambda a: sb_block(a[0], a[1], k, v, k_pos, bias), (qs, ps))
    return out.swapaxes(0, 1).reshape(b, L, h, d)


def causal_conv(x, buf, w, bias):
    L = x.shape[1]
    xp = jnp.concatenate([buf.astype(x.dtype), x], axis=1)
    y = bias + w[0] * xp[:, 0:L]
    for j in range(1, LRU_CONV):
        y = y + w[j] * xp[:, j:j + L]
    return y, xp[:, L:]


def rg_lru(x, h0, wa, ba, wi, bi, lam):
    b, L, w = x.shape
    xb = x.reshape(b, L, LRU_BLOCKS, -1)
    r = jax.nn.sigmoid(jnp.einsum('blni,nij->blnj', xb, wa).reshape(b, L, w) + ba)
    i = jax.nn.sigmoid(jnp.einsum('blni,nij->blnj', xb, wi).reshape(b, L, w) + bi)
    log_a = -LRU_C * r.astype(jnp.float32) * jax.nn.softplus(-lam.astype(jnp.float32))
    a = jnp.exp(log_a)
    gx = jnp.sqrt(-jnp.expm1(2.0 * log_a)) * (i * x).astype(jnp.float32)
    gx = gx.at[:, 0].add(a[:, 0] * h0.astype(jnp.float32))

    def combine(c1, c2):
        return c2[0] * c1[0], c2[0] * c1[1] + c2[1]

    _, hs = lax.associative_scan(combine, (a, gx), axis=1)
    return hs.astype(x.dtype), hs[:, -1]


def gla(q, k, v, log_alpha, s0):
    b, L, h, dk = q.shape
    dv = v.shape[-1]
    c = GLA_CHUNK if L % GLA_CHUNK == 0 else L
    n = L // c
    rs = lambda t: t.reshape(b, n, c, h, t.shape[-1]).astype(jnp.float32)
    q, k, v, la = rs(q) * (dk ** -0.5), rs(k), rs(v), rs(log_alpha)
    bcum = jnp.cumsum(la, axis=2)
    tri = jnp.tril(jnp.ones((c, c), dtype=bool))[None, None, :, :, None, None]
    diff = bcum[:, :, :, None] - bcum[:, :, None, :]
    decay = jnp.exp(jnp.where(tri, diff, -jnp.inf))
    scores = jnp.einsum('bnthd,bnshd,bntshd->bntsh', q, k, decay)
    o_intra = jnp.einsum('bntsh,bnshv->bnthv', scores, v)
    last = bcum[:, :, -1]
    q_dec = q * jnp.exp(bcum)
    kv = jnp.einsum('bnshd,bnshv->bnhdv', k * jnp.exp(last[:, :, None] - bcum), v)

    def step(S, xs):
        qd, dec, kvc = xs
        o = jnp.einsum('bthd,bhdv->bthv', qd, S)
        return dec[..., None] * S + kvc, o

    s_fin, o_inter = lax.scan(step, s0.astype(jnp.float32),
                              (q_dec.swapaxes(0, 1), jnp.exp(last).swapaxes(0, 1), kv.swapaxes(0, 1)))
    o = o_intra + o_inter.swapaxes(0, 1)
    return o.reshape(b, L, h, dv), s_fin


def token_mix(h, past_k, past_v, lru_h0, conv0, s0, p):
    b, L, _ = h.shape
    P = past_k.shape[1]
    u = h @ p['w_in']
    cuts = [int(s) for s in np.cumsum(SPLITS)[:-1]]
    q_sb, k_sb, v_sb, x_lru, gate_lru, q_g, k_g, v_g, r_g, a_lr = jnp.split(u, cuts, axis=-1)
    heads = lambda t, nh: t.reshape(b, L, nh, -1)
    q_sb = rmsnorm(heads(q_sb, SB_HEADS), p['g_qnorm'])
    k_sb = rmsnorm(heads(k_sb, SB_HEADS), p['g_knorm'])
    v_sb = heads(v_sb, SB_HEADS)
    k_all = jnp.concatenate([past_k.astype(h.dtype), k_sb], axis=1)
    v_all = jnp.concatenate([past_v.astype(h.dtype), v_sb], axis=1)
    pos = jnp.arange(P + L, dtype=jnp.int32)
    o_sb = stick_breaking(q_sb, k_all, v_all, pos[P:], pos, p['sb_bias'])
    xc, conv_new = causal_conv(x_lru, conv0, p['conv_w'], p['conv_b'])
    o_lru, h_new = rg_lru(xc, lru_h0, p['lru_wa'], p['lru_ba'], p['lru_wi'], p['lru_bi'], p['lru_lambda'])
    log_alpha = jax.nn.log_sigmoid((a_lr @ p['gla_w_alpha'] + p['gla_b_alpha']).astype(jnp.float32)) / GLA_TAU
    o_gla, s_new = gla(heads(q_g, GLA_HEADS), heads(k_g, GLA_HEADS), heads(v_g, GLA_HEADS),
                       heads(log_alpha, GLA_HEADS), s0)
    o = jnp.concatenate([o_sb.reshape(b, L, SB_W), o_lru, o_gla.reshape(b, L, GLA_W).astype(h.dtype)], axis=-1)
    o = rmsnorm(o.reshape(b, L, D_MIX // HEAD_DIM, HEAD_DIM), p['g_mix_out'].reshape(-1, HEAD_DIM)).reshape(b, L, D_MIX)
    o = jnp.concatenate([o[..., :SB_W],
                         o[..., SB_W:SB_W + LRU_W] * jax.nn.gelu(gate_lru),
                         o[..., SB_W + LRU_W:] * jax.nn.silu(r_g)], axis=-1)
    return o @ p['w_out'], k_sb, v_sb, h_new, conv_new, s_new


def decoder_layer(x, past_k, past_v, lru_h0, conv0, s0, p):
    x = x + 0.5 * swiglu_ffn(x, p['g_ffn1'], p['w_ffn1_gate'], p['w_ffn1_up'], p['w_ffn1_down'])
    m, k_new, v_new, h_new, conv_new, s_new = token_mix(rmsnorm(x, p['g_mix']), past_k, past_v, lru_h0, conv0, s0, p)
    x = x + m
    x = x + 0.5 * swiglu_ffn(x, p['g_ffn2'], p['w_ffn2_gate'], p['w_ffn2_up'], p['w_ffn2_down'])
    return x, k_new, v_new, h_new, conv_new, s_new


def setup_inputs(seed: int = 0) -> dict:
    key = jax.random.key(seed)
    ks = iter(jax.random.split(key, 40))
    nrm = lambda shape, scale: jax.random.normal(next(ks), shape, jnp.float32) * scale
    n_pages = PAST_LEN // PAGE_SIZE
    n_used = DEC_BATCH * n_pages
    n_phys = n_used * 5 // 4
    page_table = jax.random.permutation(next(ks), n_phys)[:n_used].reshape(DEC_BATCH, n_pages).astype(jnp.int32)
    u = jax.random.uniform(next(ks), (DEPTH, LRU_W), jnp.float32, minval=0.9, maxval=0.999)
    a0 = u ** (1.0 / LRU_C)
    lru_lambda = jnp.log(a0) - jnp.log1p(-a0)
    return {
        'x_prompt': nrm((BATCH, SEQ, D_MODEL), 1.0),
        'x_sample': nrm((DEC_BATCH, DEC_SEQ, D_MODEL), 1.0),
        'cache_k': nrm((DEPTH, n_phys, PAGE_SIZE, SB_HEADS, HEAD_DIM), 1.0),
        'cache_v': nrm((DEPTH, n_phys, PAGE_SIZE, SB_HEADS, HEAD_DIM), 1.0),
        'page_table': page_table,
        'state_lru_h': nrm((DEPTH, DEC_BATCH, LRU_W), 0.5),
        'state_lru_conv': nrm((DEPTH, DEC_BATCH, LRU_CONV - 1, LRU_W), 1.0),
        'state_gla': nrm((DEPTH, DEC_BATCH, GLA_HEADS, GLA_DK, GLA_DV), 0.5),
        'g_ffn1': 1.0 + nrm((DEPTH, D_MODEL), 0.02),
        'w_ffn1_gate': nrm((DEPTH, D_MODEL, D_FF), D_MODEL ** -0.5),
        'w_ffn1_up': nrm((DEPTH, D_MODEL, D_FF), D_MODEL ** -0.5),
        'w_ffn1_down': nrm((DEPTH, D_FF, D_MODEL), D_FF ** -0.5),
        'g_mix': 1.0 + nrm((DEPTH, D_MODEL), 0.02),
        'w_in': nrm((DEPTH, D_MODEL, D_IN), D_MODEL ** -0.5),
        'g_qnorm': 1.0 + nrm((DEPTH, HEAD_DIM), 0.02),
        'g_knorm': 1.0 + nrm((DEPTH, HEAD_DIM), 0.02),
        'sb_bias': SB_BIAS_INIT + nrm((DEPTH, SB_HEADS), 0.1),
        'conv_w': nrm((DEPTH, LRU_CONV, LRU_W), LRU_CONV ** -0.5),
        'conv_b': nrm((DEPTH, LRU_W), 0.01),
        'lru_wa': nrm((DEPTH, LRU_BLOCKS, LRU_W // LRU_BLOCKS, LRU_W // LRU_BLOCKS), (LRU_W // LRU_BLOCKS) ** -0.5),
        'lru_ba': nrm((DEPTH, LRU_W), 0.01),
        'lru_wi': nrm((DEPTH, LRU_BLOCKS, LRU_W // LRU_BLOCKS, LRU_W // LRU_BLOCKS), (LRU_W // LRU_BLOCKS) ** -0.5),
        'lru_bi': nrm((DEPTH, LRU_W), 0.01),
        'lru_lambda': lru_lambda,
        'gla_w_alpha': nrm((DEPTH, GLA_RANK, GLA_HEADS * GLA_DK), GLA_RANK ** -0.5),
        'gla_b_alpha': nrm((DEPTH, GLA_HEADS * GLA_DK), 0.1),
        'g_mix_out': 1.0 + nrm((DEPTH, D_MIX), 0.02),
        'w_out': nrm((DEPTH, D_MIX, D_MODEL), D_MIX ** -0.5),
        'g_ffn2': 1.0 + nrm((DEPTH, D_MODEL), 0.02),
        'w_ffn2_gate': nrm((DEPTH, D_MODEL, D_FF), D_MODEL ** -0.5),
        'w_ffn2_up': nrm((DEPTH, D_MODEL, D_FF), D_MODEL ** -0.5),
        'w_ffn2_down': nrm((DEPTH, D_FF, D_MODEL), D_FF ** -0.5),
    }


def reference(x_prompt, x_sample, cache_k, cache_v, page_table, state_lru_h, state_lru_conv, state_gla,
              g_ffn1, w_ffn1_gate, w_ffn1_up, w_ffn1_down, g_mix, w_in, g_qnorm, g_knorm, sb_bias,
              conv_w, conv_b, lru_wa, lru_ba, lru_wi, lru_bi, lru_lambda, gla_w_alpha, gla_b_alpha,
              g_mix_out, w_out, g_ffn2, w_ffn2_gate, w_ffn2_up, w_ffn2_down):
    dt = x_prompt.dtype
    b_p = x_prompt.shape[0]
    b_s, n_pages = page_table.shape
    yp, ys = x_prompt, x_sample
    kp, vp, ks_, vs_, hp, hs, cp, cs, sp, ss = [], [], [], [], [], [], [], [], [], []
    for l in range(DEPTH):
        p = {'g_ffn1': g_ffn1[l], 'w_ffn1_gate': w_ffn1_gate[l], 'w_ffn1_up': w_ffn1_up[l], 'w_ffn1_down': w_ffn1_down[l],
             'g_mix': g_mix[l], 'w_in': w_in[l], 'g_qnorm': g_qnorm[l], 'g_knorm': g_knorm[l], 'sb_bias': sb_bias[l],
             'conv_w': conv_w[l], 'conv_b': conv_b[l], 'lru_wa': lru_wa[l], 'lru_ba': lru_ba[l],
             'lru_wi': lru_wi[l], 'lru_bi': lru_bi[l], 'lru_lambda': lru_lambda[l],
             'gla_w_alpha': gla_w_alpha[l], 'gla_b_alpha': gla_b_alpha[l],
             'g_mix_out': g_mix_out[l], 'w_out': w_out[l],
             'g_ffn2': g_ffn2[l], 'w_ffn2_gate': w_ffn2_gate[l], 'w_ffn2_up': w_ffn2_up[l], 'w_ffn2_down': w_ffn2_down[l]}
        empty = jnp.zeros((b_p, 0, SB_HEADS, HEAD_DIM), dt)
        yp, k1, v1, h1, c1, s1 = decoder_layer(
            yp, empty, empty, jnp.zeros((b_p, LRU_W), dt), jnp.zeros((b_p, LRU_CONV - 1, LRU_W), dt),
            jnp.zeros((b_p, GLA_HEADS, GLA_DK, GLA_DV), jnp.float32), p)
        past_k = cache_k[l][page_table].reshape(b_s, n_pages * PAGE_SIZE, SB_HEADS, HEAD_DIM)
        past_v = cache_v[l][page_table].reshape(b_s, n_pages * PAGE_SIZE, SB_HEADS, HEAD_DIM)
        ys, k2, v2, h2, c2, s2 = decoder_layer(ys, past_k, past_v, state_lru_h[l], state_lru_conv[l], state_gla[l], p)
        kp.append(k1); vp.append(v1); ks_.append(k2); vs_.append(v2)
        hp.append(h1); hs.append(h2); cp.append(c1); cs.append(c2); sp.append(s1); ss.append(s2)
    return (yp, ys, jnp.stack(kp), jnp.stack(vp), jnp.stack(ks_), jnp.stack(vs_),
            jnp.stack(hp), jnp.stack(hs), jnp.stack(cp), jnp.stack(cs), jnp.stack(sp), jnp.stack(ss))
```

```python
import functools

import jax
import jax.numpy as jnp
from jax import lax
from jax.experimental import pallas as pl
from jax.experimental.pallas import tpu as pltpu

F32 = jnp.float32
BF16 = jnp.bfloat16

D_MODEL = 1024
HEAD_DIM = 64
SB_W = D_MODEL // 2
SB_HEADS = SB_W // HEAD_DIM
LRU_W = D_MODEL // 4
LRU_BLOCKS = LRU_W // HEAD_DIM
LRU_CONV = 4
LRU_C = 8.0
GLA_W = D_MODEL // 4
GLA_HEADS = GLA_W // HEAD_DIM
GLA_DV = HEAD_DIM
GLA_DK = HEAD_DIM // 2
GLA_QK = GLA_HEADS * GLA_DK
GLA_RANK = 16
GLA_TAU = 16.0
D_MIX = SB_W + LRU_W + GLA_W
D_FF = 2816
EPS = 1e-6
PAGE_SIZE = 128

LANES = 128
SUBLANES = 8
VMEM_LIMIT_BYTES = 56 * 1024 * 1024

_OFF_Q = 0
_OFF_K = _OFF_Q + SB_W
_OFF_V = _OFF_K + SB_W
_OFF_XL = _OFF_V + SB_W
_OFF_GATE = _OFF_XL + LRU_W
_OFF_QG = _OFF_GATE + LRU_W
_OFF_KG = _OFF_QG + GLA_QK
_OFF_VG = _OFF_KG + GLA_QK
_OFF_RG = _OFF_VG + GLA_W
_OFF_ALR = _OFF_RG + GLA_W
D_IN_PAD = _OFF_ALR + LANES

FF_CHUNK = 256
ROW_TILE = 512
SB_BLOCK = 256
SB_PAIRS_PER_STEP = 4
GLA_CHUNK = 128
DECODE_PAGES_PER_STEP = 16


def _idiv(x, d):
    assert d & (d - 1) == 0
    return x >> (d.bit_length() - 1)


def _imod(x, d):
    assert d & (d - 1) == 0
    return x & (d - 1)


def _dot(a, b):
    return jnp.dot(a, b, preferred_element_type=F32)


def _dot_nt(a, b):
    return lax.dot_general(a, b, (((1,), (1,)), ((), ())), preferred_element_type=F32)


def _dot_tn(a, b):
    return lax.dot_general(a, b, (((0,), (0,)), ((), ())), preferred_element_type=F32)


def _split_dot(x, w, passes):
    hi = x.astype(BF16)
    acc = _dot(hi, w)
    rem = x - hi.astype(F32)
    for _ in range(passes - 1):
        lo = rem.astype(BF16)
        acc = acc + _dot(lo, w)
        rem = rem - lo.astype(F32)
    return acc


def _split_dot_left(w, x, passes):
    hi = x.astype(BF16)
    acc = _dot(w, hi)
    rem = x - hi.astype(F32)
    for _ in range(passes - 1):
        lo = rem.astype(BF16)
        acc = acc + _dot(w, lo)
        rem = rem - lo.astype(F32)
    return acc


def _rms(x, g):
    ms = jnp.mean(x * x, axis=-1, keepdims=True)
    return x * lax.rsqrt(ms + EPS) * g


def _softplus(z):
    return jnp.maximum(z, 0.0) + jnp.log1p(jnp.exp(-jnp.abs(z)))


def _log_sigmoid(z):
    return jnp.minimum(z, 0.0) - jnp.log1p(jnp.exp(-jnp.abs(z)))


def _gelu_tanh(x):
    return 0.5 * x * (1.0 + jnp.tanh(0.7978845608028654 * (x + 0.044715 * (x * x * x))))


def _head_rms_cols(x, e64, g):
    cols = []
    for c in range(x.shape[1] // LANES):
        xc = x[:, c * LANES:(c + 1) * LANES]
        ms = _split_dot(xc * xc, e64, 2)
        cols.append(xc * lax.rsqrt(ms + EPS) * g[:, c * LANES:(c + 1) * LANES])
    return cols


def _resident(shape):
    nd = len(shape)
    return pl.BlockSpec(shape, lambda *_: (0,) * nd, pipeline_mode=pl.Buffered(1))


def _params(semantics):
    return pltpu.CompilerParams(dimension_semantics=semantics, vmem_limit_bytes=VMEM_LIMIT_BYTES)


def _ffn_kernel(x_ref, g_ref, wgu_ref, wd_ref, o_ref, h_ref, acc_ref):
    x = x_ref[...]
    h_ref[...] = _rms(x, g_ref[...]).astype(BF16)
    acc_ref[...] = jnp.zeros_like(acc_ref)

    def body(c, carry):
        gu = _dot(h_ref[...], wgu_ref[c])
        gate = gu[:, :FF_CHUNK]
        up = gu[:, FF_CHUNK:]
        act = (gate * jax.nn.sigmoid(gate) * up).astype(BF16)
        acc_ref[...] += _dot(act, wd_ref[c])
        return carry

    lax.fori_loop(0, D_FF // FF_CHUNK, body, 0)
    o_ref[...] = x + 0.5 * acc_ref[...]


def _ffn(x, g, wgu, wd, tm):
    n = x.shape[0]
    nc = D_FF // FF_CHUNK
    return pl.pallas_call(
        _ffn_kernel,
        grid=(n // tm,),
        in_specs=[
            pl.BlockSpec((tm, D_MODEL), lambda i: (i, 0)),
            _resident((1, D_MODEL)),
            _resident((nc, D_MODEL, 2 * FF_CHUNK)),
            _resident((nc, FF_CHUNK, D_MODEL)),
        ],
        out_specs=pl.BlockSpec((tm, D_MODEL), lambda i: (i, 0)),
        out_shape=jax.ShapeDtypeStruct((n, D_MODEL), F32),
        scratch_shapes=[pltpu.VMEM((tm, D_MODEL), BF16), pltpu.VMEM((tm, D_MODEL), F32)],
        compiler_params=_params(("parallel",)),
        name="ffn",
    )(x, g, wgu, wd)


def _inproj_kernel(x_ref, g_ref, w_ref, gq_ref, gk_ref, wal_ref, bal_ref, e64_ref,
                   q_ref, k_ref, v_ref, xl_ref, gate_ref, qg_ref, kg_ref, vg_ref, rg_ref, la_ref,
                   h_ref):
    h_ref[...] = _rms(x_ref[...], g_ref[...]).astype(BF16)

    def proj(lo, width):
        return _dot(h_ref[...], w_ref[:, lo:lo + width])

    e64 = e64_ref[...]
    q_cols = _head_rms_cols(proj(_OFF_Q, SB_W), e64, gq_ref[...])
    for c, col in enumerate(q_cols):
        q_ref[:, c * LANES:(c + 1) * LANES] = col * (HEAD_DIM ** -0.5)
    k_cols = _head_rms_cols(proj(_OFF_K, SB_W), e64, gk_ref[...])
    for c, col in enumerate(k_cols):
        k_ref[:, c * LANES:(c + 1) * LANES] = col
    v_ref[...] = proj(_OFF_V, SB_W)
    xl_ref[...] = proj(_OFF_XL, LRU_W)
    gate_ref[...] = _gelu_tanh(proj(_OFF_GATE, LRU_W))
    qg_ref[...] = proj(_OFF_QG, GLA_QK)
    kg_ref[...] = proj(_OFF_KG, GLA_QK)
    vg_ref[...] = proj(_OFF_VG, GLA_W)
    r = proj(_OFF_RG, GLA_W)
    rg_ref[...] = r * jax.nn.sigmoid(r)
    a_lr = proj(_OFF_ALR, LANES)
    xa = _dot(a_lr.astype(BF16), wal_ref[...]) + bal_ref[...]
    la_ref[...] = _log_sigmoid(xa) * (1.0 / GLA_TAU)


def _inproj(x, g, w_in, gq, gk, wal, bal, e64, tm, kv_slabs, layer):
    n = x.shape[0]
    row = lambda w: pl.BlockSpec((tm, w), lambda i: (i, 0))
    in_specs = [
        row(D_MODEL),
        _resident((1, D_MODEL)),
        _resident((D_MODEL, D_IN_PAD)),
        _resident((1, SB_W)),
        _resident((1, SB_W)),
        _resident((LANES, LANES)),
        _resident((1, LANES)),
        _resident((LANES, LANES)),
    ]
    args = [x, g, w_in, gq, gk, wal, bal, e64]
    small = lambda w: jax.ShapeDtypeStruct((n, w), F32)
    if kv_slabs is None:
        kv_specs = [row(SB_W), row(SB_W)]
        kv_shapes = [small(SB_W), small(SB_W)]
        aliases = {}
        kernel = _inproj_kernel
    else:
        k_all, v_all = kv_slabs
        slab = pl.BlockSpec((None, tm, SB_W), lambda i: (layer, i, 0))
        kv_specs = [slab, slab]
        kv_shapes = [jax.ShapeDtypeStruct(k_all.shape, F32), jax.ShapeDtypeStruct(v_all.shape, F32)]
        in_specs += [pl.BlockSpec(memory_space=pl.ANY), pl.BlockSpec(memory_space=pl.ANY)]
        args += [k_all, v_all]
        aliases = {len(args) - 2: 1, len(args) - 1: 2}

        def kernel(*refs):
            _inproj_kernel(*refs[:8], *refs[10:])

    out_specs = [row(SB_W)] + kv_specs + [row(LRU_W), row(LRU_W), row(GLA_QK), row(GLA_QK),
                                          row(GLA_W), row(GLA_W), row(GLA_QK)]
    out_shape = [small(SB_W)] + kv_shapes + [small(LRU_W), small(LRU_W), small(GLA_QK), small(GLA_QK),
                                             small(GLA_W), small(GLA_W), small(GLA_QK)]
    return pl.pallas_call(
        kernel,
        grid=(n // tm,),
        in_specs=in_specs,
        out_specs=out_specs,
        out_shape=out_shape,
        input_output_aliases=aliases,
        scratch_shapes=[pltpu.VMEM((tm, D_MODEL), BF16)],
        compiler_params=_params(("parallel",)),
        name="inproj",
    )(*args)


def _merge_kernel(x_ref, osb_ref, olru_ref, ogla_ref, gate_ref, rg_ref, gmo_ref, e64_ref, wout_ref, o_ref):
    e64 = e64_ref[...]
    gmo = gmo_ref[...]
    sb = _head_rms_cols(osb_ref[...], e64, gmo[:, :SB_W])
    lru = _head_rms_cols(olru_ref[...], e64, gmo[:, SB_W:SB_W + LRU_W])
    gla = _head_rms_cols(ogla_ref[...], e64, gmo[:, SB_W + LRU_W:])
    gate = gate_ref[...]
    rg = rg_ref[...]
    lru = [col * gate[:, c * LANES:(c + 1) * LANES] for c, col in enumerate(lru)]
    gla = [col * rg[:, c * LANES:(c + 1) * LANES] for c, col in enumerate(gla)]
    acc = x_ref[...]
    for c, col in enumerate(sb + lru + gla):
        acc = acc + _dot(col.astype(BF16), wout_ref[c * LANES:(c + 1) * LANES, :])
    o_ref[...] = acc


def _merge(x, osb, olru, ogla, gate, rg, gmo, e64, wout, tm):
    n = x.shape[0]
    row = lambda w: pl.BlockSpec((tm, w), lambda i: (i, 0))
    return pl.pallas_call(
        _merge_kernel,
        grid=(n // tm,),
        in_specs=[row(D_MODEL), row(SB_W), row(LRU_W), row(GLA_W), row(LRU_W), row(GLA_W),
                  _resident((1, D_MIX)), _resident((LANES, LANES)), _resident((D_MIX, D_MODEL))],
        out_specs=row(D_MODEL),
        out_shape=jax.ShapeDtypeStruct((n, D_MODEL), F32),
        compiler_params=_params(("parallel",)),
        name="merge",
    )(x, osb, olru, ogla, gate, rg, gmo, e64, wout)


def _softplus_plain(z):
    return jnp.maximum(z, 0.0) + jnp.log(1.0 + jnp.exp(-jnp.abs(z)))


def _sb_tile_stages(q_h, k_b, v_h, bias, upper, carry, mask):
    st = {}

    def logits():
        st["z"] = _dot_nt(q_h, k_b) + bias

    def log_terms():
        z = st["z"]
        sp = _softplus_plain(z)
        l1m = -sp if mask is None else jnp.where(mask, -sp, 0.0)
        st["lb"] = z - sp
        st["edge"] = l1m[:, 0:1]
        hi = l1m.astype(BF16)
        st["hi"] = hi
        st["lo"] = (l1m - hi.astype(F32)).astype(BF16)

    def suffix_sums():
        st["within"] = _dot(st["hi"], upper) + _dot(st["lo"], upper)

    def weights():
        within = st["within"]
        a = jnp.exp(st["lb"] + within + carry)
        if mask is not None:
            a = jnp.where(mask, a, 0.0)
        st["a"] = a.astype(BF16)
        st["carry"] = carry + within[:, 0:1] + st["edge"]

    def values():
        return _dot(st["a"], v_h), st["carry"]

    return [logits, log_terms, suffix_sums, weights, values]


def _run_skewed(pipelines):
    depth = len(pipelines[0])
    results = [None] * len(pipelines)
    for step in range(len(pipelines) + depth - 1):
        for p, stages in enumerate(pipelines):
            s = step - p
            if 0 <= s < depth:
                results[p] = stages[s]()
    return results


def _sb_prompt_kernel(bias_ref, q_ref, k_ref, v_ref, o_ref, kb_ref, vh_ref, *, blk, pairs):
    col0 = pl.program_id(1) * pairs
    i = pl.program_id(2)
    lane = lax.broadcasted_iota(jnp.int32, (1, LANES), 1)
    first = lane < HEAD_DIM

    @pl.when(i == 0)
    def _stage():
        kb_ref[...] = k_ref[...].astype(BF16)
        for p in range(pairs):
            v = v_ref[:, p * LANES:(p + 1) * LANES]
            vh_ref[2 * p] = jnp.where(first, v, 0.0).astype(BF16)
            vh_ref[2 * p + 1] = jnp.where(first, 0.0, v).astype(BF16)

    q_heads = []
    biases = []
    for p in range(pairs):
        q = q_ref[:, p * LANES:(p + 1) * LANES]
        q_heads += [jnp.where(first, q, 0.0).astype(BF16), jnp.where(first, 0.0, q).astype(BF16)]
        biases += [bias_ref[2 * (col0 + p)], bias_ref[2 * (col0 + p) + 1]]
    r_i = lax.broadcasted_iota(jnp.int32, (blk, blk), 0)
    c_i = lax.broadcasted_iota(jnp.int32, (blk, blk), 1)
    upper = jnp.where(r_i > c_i, 1.0, 0.0).astype(BF16)
    causal = c_i < r_i

    def tile(j, carries, accs, mask):
        rows = pl.ds(pl.multiple_of(j * blk, blk), blk)
        pipelines = []
        for h in range(2 * pairs):
            k_b = kb_ref[rows, (h // 2) * LANES:(h // 2 + 1) * LANES]
            pipelines.append(_sb_tile_stages(q_heads[h], k_b, vh_ref[h, rows, :], biases[h], upper,
                                             carries[h], mask))
        results = _run_skewed(pipelines)
        new_accs = list(accs)
        for h, (out, _) in enumerate(results):
            new_accs[h // 2] = new_accs[h // 2] + out
        return tuple(c for _, c in results), tuple(new_accs)

    zero_c = jnp.zeros((blk, 1), F32)
    zero_a = jnp.zeros((blk, LANES), F32)
    state = tile(i, (zero_c,) * (2 * pairs), (zero_a,) * pairs, causal)

    def body(it, state):
        return tile(i - 1 - it, state[0], state[1], None)

    _, accs = lax.fori_loop(0, i, body, state)
    for p in range(pairs):
        o_ref[:, p * LANES:(p + 1) * LANES] = accs[p]


def _sb_prompt(q, k, v, bias, batch, seq, kv_layer=None):
    blk = min(SB_BLOCK, seq)
    nq = seq // blk
    pairs = SB_PAIRS_PER_STEP
    width = pairs * LANES
    groups = SB_W // width
    if kv_layer is None:
        kv_spec = pl.BlockSpec((seq, width), lambda b, c, i, *_: (b, c))
    else:
        kv_spec = pl.BlockSpec((None, seq, width), lambda b, c, i, *_: (kv_layer, b, c))
    return pl.pallas_call(
        functools.partial(_sb_prompt_kernel, blk=blk, pairs=pairs),
        grid_spec=pltpu.PrefetchScalarGridSpec(
            num_scalar_prefetch=1,
            grid=(batch, groups, nq),
            in_specs=[
                pl.BlockSpec((blk, width), lambda b, c, i, *_: (b * nq + i, c)),
                kv_spec,
                kv_spec,
            ],
            out_specs=pl.BlockSpec((blk, width), lambda b, c, i, *_: (b * nq + i, c)),
            scratch_shapes=[pltpu.VMEM((seq, width), BF16), pltpu.VMEM((2 * pairs, seq, LANES), BF16)],
        ),
        out_shape=jax.ShapeDtypeStruct((batch * seq, SB_W), F32),
        compiler_params=_params(("parallel", "parallel", "arbitrary")),
        name="sb_prompt",
    )(bias, q, k, v)


def _sb_decode_kernel(pt_ref, q_ref, bias_ref, *refs, n_pages, group):
    k_refs = refs[:group]
    v_refs = refs[group:2 * group]
    o_ref = refs[2 * group]
    z_ref, a_ref, acc_ref = refs[2 * group + 1:]
    s = pl.program_id(1)
    steps = n_pages // group
    rows = n_pages * SB_HEADS

    @pl.when(s < steps)
    def _scores():
        q_col = q_ref[...]
        for g in range(group):
            prod = (k_refs[g][...] * q_col).reshape(SB_HEADS, HEAD_DIM, PAGE_SIZE)
            z = jnp.sum(prod, axis=1) + bias_ref[...]
            row0 = pl.multiple_of((s * group + g) * SB_HEADS, SB_HEADS)
            z_ref[pl.ds(row0, SB_HEADS), :] = z

    @pl.when(s == steps - 1)
    def _weights():
        z = z_ref[...]
        sp = _softplus(z)
        l1m = -sp
        r_i = lax.broadcasted_iota(jnp.int32, (PAGE_SIZE, PAGE_SIZE), 0)
        c_i = lax.broadcasted_iota(jnp.int32, (PAGE_SIZE, PAGE_SIZE), 1)
        upper = (r_i > c_i).astype(BF16)
        ones = jnp.ones((PAGE_SIZE, PAGE_SIZE), BF16)
        within = _split_dot(l1m, upper, 3)
        page_total = _split_dot(l1m, ones, 3)
        pr = lax.broadcasted_iota(jnp.int32, (rows, rows), 0)
        pc = lax.broadcasted_iota(jnp.int32, (rows, rows), 1)
        same_head = _imod(pc, SB_HEADS) == _imod(pr, SB_HEADS)
        later_page = jnp.where(_idiv(pc, SB_HEADS) > _idiv(pr, SB_HEADS),
                               jnp.where(same_head, 1.0, 0.0), 0.0).astype(BF16)
        later = _split_dot_left(later_page, page_total, 3)
        a_ref[...] = jnp.exp(z - sp + within + later)
        acc_ref[...] = jnp.zeros_like(acc_ref)

    @pl.when(s >= steps)
    def _values():
        acc = acc_ref[...]
        for g in range(group):
            row0 = pl.multiple_of(((s - steps) * group + g) * SB_HEADS, SB_HEADS)
            a_g = a_ref[pl.ds(row0, SB_HEADS), :]
            a_rows = jnp.concatenate(
                [jnp.broadcast_to(a_g[h:h + 1, :], (HEAD_DIM, PAGE_SIZE)) for h in range(SB_HEADS)], axis=0)
            acc = acc + v_refs[g][...] * a_rows
        acc_ref[...] = acc

    @pl.when(s == 2 * steps - 1)
    def _emit():
        o_ref[...] = jnp.sum(acc_ref[...], axis=1, keepdims=True)


def _sb_decode(q_col, bias_b, cache_kt, cache_vt, page_table, layer):
    b, n_pages = page_table.shape
    group = min(DECODE_PAGES_PER_STEP, n_pages)
    steps = n_pages // group

    def k_map(g):
        return lambda bi, s, pt: (layer, pt[bi, jnp.minimum(s, steps - 1) * group + g], 0, 0)

    def v_map(g):
        return lambda bi, s, pt: (layer, pt[bi, jnp.maximum(s - steps, 0) * group + g], 0, 0)

    page = lambda m: pl.BlockSpec((None, None, SB_W, PAGE_SIZE), m)
    in_specs = [pl.BlockSpec((None, SB_W, 1), lambda bi, s, pt: (bi, 0, 0)),
                pl.BlockSpec((SB_HEADS, PAGE_SIZE), lambda bi, s, pt: (0, 0))]
    in_specs += [page(k_map(g)) for g in range(group)]
    in_specs += [page(v_map(g)) for g in range(group)]
    rows = n_pages * SB_HEADS
    return pl.pallas_call(
        functools.partial(_sb_decode_kernel, n_pages=n_pages, group=group),
        grid_spec=pltpu.PrefetchScalarGridSpec(
            num_scalar_prefetch=1,
            grid=(b, 2 * steps),
            in_specs=in_specs,
            out_specs=pl.BlockSpec((None, SB_W, 1), lambda bi, s, pt: (bi, 0, 0)),
            scratch_shapes=[pltpu.VMEM((rows, PAGE_SIZE), F32),
                            pltpu.VMEM((rows, PAGE_SIZE), F32),
                            pltpu.VMEM((SB_W, PAGE_SIZE), F32)],
        ),
        out_shape=jax.ShapeDtypeStruct((b, SB_W, 1), F32),
        compiler_params=_params(("parallel", "arbitrary")),
        name="sb_decode",
    )(page_table, q_col, bias_b, *([cache_kt] * group), *([cache_vt] * group))


def _lru_gates(xc, wa_ref, ba_ref, wi_ref, bi_ref, lam_ref):
    xb = xc.astype(BF16)
    r = jax.nn.sigmoid(_dot(xb, wa_ref[...]) + ba_ref[...])
    i = jax.nn.sigmoid(_dot(xb, wi_ref[...]) + bi_ref[...])
    log_a = -LRU_C * r * _softplus(-lam_ref[...])
    a = jnp.exp(log_a)
    t = jnp.tanh(log_a)
    one_minus_a2 = -2.0 * t / (1.0 - t)
    return a, jnp.sqrt(one_minus_a2) * (i * xc)


def _lru_prompt_kernel(x_ref, cw_ref, cb_ref, wa_ref, ba_ref, wi_ref, bi_ref, lam_ref,
                       o_ref, hl_ref, cn_ref, *, seq):
    x = x_ref[...]
    rows = lax.broadcasted_iota(jnp.int32, (seq, LRU_W), 0)
    xc = cb_ref[...] + cw_ref[LRU_CONV - 1:LRU_CONV, :] * x
    for j in range(1, LRU_CONV):
        shifted = jnp.where(rows >= j, pltpu.roll(x, j, 0), 0.0)
        xc = xc + cw_ref[LRU_CONV - 1 - j:LRU_CONV - j, :] * shifted
    a, g = _lru_gates(xc, wa_ref, ba_ref, wi_ref, bi_ref, lam_ref)
    step = 1
    while step < seq:
        valid = rows >= step
        a_prev = pltpu.roll(a, step, 0)
        g_prev = pltpu.roll(g, step, 0)
        g = jnp.where(valid, a * g_prev + g, g)
        a = jnp.where(valid, a * a_prev, a)
        step *= 2
    o_ref[...] = g
    hl_ref[...] = g[seq - 1:seq, :]
    cn_ref[...] = x[seq - (LRU_CONV - 1):, :]


def _lru_prompt(xl, cw, cb, wa, ba, wi, bi, lam, batch, seq):
    vec = _resident((1, LRU_W))
    mat = _resident((LRU_W, LRU_W))
    return pl.pallas_call(
        functools.partial(_lru_prompt_kernel, seq=seq),
        grid=(batch,),
        in_specs=[pl.BlockSpec((seq, LRU_W), lambda b: (b, 0)), _resident((LRU_CONV, LRU_W)), vec,
                  mat, vec, mat, vec, vec],
        out_specs=[pl.BlockSpec((seq, LRU_W), lambda b: (b, 0)),
                   pl.BlockSpec((None, 1, LRU_W), lambda b: (b, 0, 0)),
                   pl.BlockSpec((None, LRU_CONV - 1, LRU_W), lambda b: (b, 0, 0))],
        out_shape=[jax.ShapeDtypeStruct((batch * seq, LRU_W), F32),
                   jax.ShapeDtypeStruct((batch, 1, LRU_W), F32),
                   jax.ShapeDtypeStruct((batch, LRU_CONV - 1, LRU_W), F32)],
        compiler_params=_params(("parallel",)),
        name="lru_prompt",
    )(xl, cw, cb, wa, ba, wi, bi, lam)


def _gla_prompt_kernel(q_ref, k_ref, v_ref, la_ref, o_ref, sfin_ref, s_ref, *, chunk, n_chunks):
    ci = pl.program_id(1)

    @pl.when(ci == 0)
    def _init():
        s_ref[...] = jnp.zeros_like(s_ref)

    qs = q_ref[...] * (GLA_DK ** -0.5)
    k = k_ref[...]
    v = v_ref[...]
    r_i = lax.broadcasted_iota(jnp.int32, (chunk, chunk), 0)
    c_i = lax.broadcasted_iota(jnp.int32, (chunk, chunk), 1)
    lower = (c_i <= r_i).astype(BF16)
    bc = _split_dot_left(lower, la_ref[...], 3)
    rows = lax.broadcasted_iota(jnp.int32, (chunk, GLA_QK), 0)
    qk_head = _idiv(lax.broadcasted_iota(jnp.int32, (1, GLA_QK), 1), GLA_DK)
    v_head = _idiv(lax.broadcasted_iota(jnp.int32, (1, GLA_W), 1), GLA_DV)
    own = (_idiv(lax.broadcasted_iota(jnp.int32, (GLA_QK, GLA_W), 0), GLA_DK)
           == _idiv(lax.broadcasted_iota(jnp.int32, (GLA_QK, GLA_W), 1), GLA_DV))

    state = s_ref[...]
    out = _dot((qs * jnp.exp(bc)).astype(BF16), state.astype(BF16))

    scores = [jnp.zeros((chunk, chunk), F32) for _ in range(GLA_HEADS)]
    level = 2 * SUBLANES
    while level <= chunk:
        half = level // 2
        mid = jnp.concatenate(
            [jnp.broadcast_to(bc[b * level + half - 1:b * level + half, :], (level, GLA_QK))
             for b in range(chunk // level)], axis=0)
        second = (rows & (level - 1)) >= half
        q_dec = jnp.where(second, qs * jnp.exp(bc - mid), 0.0)
        k_dec = jnp.where(second, 0.0, k * jnp.exp(mid - bc)).astype(BF16)
        same_block = _idiv(r_i, level) == _idiv(c_i, level)
        for h in range(GLA_HEADS):
            sc = _dot_nt(jnp.where(qk_head == h, q_dec, 0.0).astype(BF16), k_dec)
            if level < chunk:
                sc = jnp.where(same_block, sc, 0.0)
            scores[h] = scores[h] + sc
        level *= 2
    for h in range(GLA_HEADS):
        out = out + _dot(scores[h].astype(BF16), jnp.where(v_head == h, v, 0.0).astype(BF16))

    expand = jnp.where(own, 1.0, 0.0).astype(BF16)
    out = out + _split_dot(qs * k, expand, 2) * v
    for j in range(1, SUBLANES):
        ok = (rows & (SUBLANES - 1)) >= j
        p = jnp.where(ok, qs * pltpu.roll(k, j, 0) * jnp.exp(bc - pltpu.roll(bc, j, 0)), 0.0)
        out = out + _split_dot(p, expand, 2) * pltpu.roll(v, j, 0)
    o_ref[...] = out

    last = bc[chunk - 1:chunk, :]
    kv = _dot_tn((k * jnp.exp(last - bc)).astype(BF16), v.astype(BF16))
    dec_col = jnp.transpose(jnp.broadcast_to(jnp.exp(last), (GLA_QK, GLA_QK)))
    dec = jnp.concatenate([dec_col] * (GLA_W // GLA_QK), axis=1)
    new_state = dec * state + jnp.where(own, kv, 0.0)
    s_ref[...] = new_state

    @pl.when(ci == n_chunks - 1)
    def _emit():
        for h in range(GLA_HEADS):
            sfin_ref[h * GLA_DK:(h + 1) * GLA_DK, :] = (
                new_state[h * GLA_DK:(h + 1) * GLA_DK, h * GLA_DV:(h + 1) * GLA_DV])


def _gla_prompt(qg, kg, vg, la, batch, seq):
    chunk = min(GLA_CHUNK, seq)
    n_chunks = seq // chunk
    row = lambda w: pl.BlockSpec((chunk, w), lambda b, c: (b * n_chunks + c, 0))
    return pl.pallas_call(
        functools.partial(_gla_prompt_kernel, chunk=chunk, n_chunks=n_chunks),
        grid=(batch, n_chunks),
        in_specs=[row(GLA_QK), row(GLA_QK), row(GLA_W), row(GLA_QK)],
        out_specs=[row(GLA_W), pl.BlockSpec((None, GLA_QK, GLA_DV), lambda b, c: (b, 0, 0))],
        out_shape=[jax.ShapeDtypeStruct((batch * seq, GLA_W), F32),
                   jax.ShapeDtypeStruct((batch, GLA_QK, GLA_DV), F32)],
        scratch_shapes=[pltpu.VMEM((GLA_QK, GLA_W), F32)],
        compiler_params=_params(("parallel", "arbitrary")),
        name="gla_prompt",
    )(qg, kg, vg, la)


def _step_kernel(x_ref, conv_ref, h0_ref, cw_ref, cb_ref, wa_ref, ba_ref, wi_ref, bi_ref, lam_ref,
                 qc_ref, kc_ref, lac_ref, v_ref, s0_ref,
                 h_ref, cn_ref, og_ref, s_ref):
    x = x_ref[...]
    xc = cb_ref[...] + cw_ref[LRU_CONV - 1:LRU_CONV, :] * x
    for j in range(LRU_CONV - 1):
        xc = xc + cw_ref[j:j + 1, :] * conv_ref[j]
    a, g = _lru_gates(xc, wa_ref, ba_ref, wi_ref, bi_ref, lam_ref)
    h_ref[...] = a * h0_ref[...] + g
    for j in range(LRU_CONV - 2):
        cn_ref[j] = conv_ref[j + 1]
    cn_ref[LRU_CONV - 2] = x

    alpha = jnp.exp(lac_ref[...])
    kc = kc_ref[...]
    qc = qc_ref[...] * (GLA_DK ** -0.5)
    for h in range(GLA_HEADS):
        rs = slice(h * GLA_DK, (h + 1) * GLA_DK)
        new = alpha[:, rs, :] * s0_ref[:, rs, :] + kc[:, rs, :] * v_ref[:, h:h + 1, :]
        s_ref[:, rs, :] = new
        og_ref[:, h:h + 1, :] = jnp.sum(qc[:, rs, :] * new, axis=1, keepdims=True)


def _step(xl, conv0, h0, cw, cb, wa, ba, wi, bi, lam, q_col, k_col, la_col, v3, s0):
    b = xl.shape[0]
    return pl.pallas_call(
        _step_kernel,
        out_shape=[jax.ShapeDtypeStruct((b, LRU_W), F32),
                   jax.ShapeDtypeStruct((LRU_CONV - 1, b, LRU_W), F32),
                   jax.ShapeDtypeStruct((b, GLA_HEADS, GLA_DV), F32),
                   jax.ShapeDtypeStruct((b, GLA_QK, GLA_DV), F32)],
        compiler_params=pltpu.CompilerParams(vmem_limit_bytes=VMEM_LIMIT_BYTES),
        name="sample_step",
    )(xl, conv0, h0, cw, cb, wa, ba, wi, bi, lam, q_col, k_col, la_col, v3, s0)


def _block_diag(w):
    nb, bi, bj = w.shape
    eye = jnp.eye(nb, dtype=w.dtype)
    return (eye[:, None, :, None] * w[:, :, None, :]).reshape(nb * bi, nb * bj)


def _row_tile(n):
    return min(ROW_TILE, n)


def kernel(x_prompt, x_sample, cache_k, cache_v, page_table, state_lru_h, state_lru_conv, state_gla,
           g_ffn1, w_ffn1_gate, w_ffn1_up, w_ffn1_down, g_mix, w_in, g_qnorm, g_knorm, sb_bias,
           conv_w, conv_b, lru_wa, lru_ba, lru_wi, lru_bi, lru_lambda, gla_w_alpha, gla_b_alpha,
           g_mix_out, w_out, g_ffn2, w_ffn2_gate, w_ffn2_up, w_ffn2_down):
    depth = w_in.shape[0]
    bp, seq, _ = x_prompt.shape
    bs, dec_seq, _ = x_sample.shape
    assert dec_seq == 1
    n_p = bp * seq
    n_phys = cache_k.shape[1]
    nc = D_FF // FF_CHUNK

    def ffn_weights(wg, wu, wd):
        wg = wg.reshape(depth, D_MODEL, nc, FF_CHUNK)
        wu = wu.reshape(depth, D_MODEL, nc, FF_CHUNK)
        wgu = jnp.concatenate([wg, wu], axis=-1).transpose(0, 2, 1, 3).astype(BF16)
        return wgu, wd.reshape(depth, nc, FF_CHUNK, D_MODEL).astype(BF16)

    wgu1, wd1 = ffn_weights(w_ffn1_gate, w_ffn1_up, w_ffn1_down)
    wgu2, wd2 = ffn_weights(w_ffn2_gate, w_ffn2_up, w_ffn2_down)
    w_in_p = jnp.pad(w_in, ((0, 0), (0, 0), (0, D_IN_PAD - w_in.shape[2]))).astype(BF16)
    w_al = jnp.pad(gla_w_alpha, ((0, 0), (0, LANES - GLA_RANK), (0, 0))).astype(BF16)
    w_out_b = w_out.astype(BF16)
    wa_bd = jax.vmap(_block_diag)(lru_wa).astype(BF16)
    wi_bd = jax.vmap(_block_diag)(lru_wi).astype(BF16)
    gq = jnp.tile(g_qnorm, (1, SB_HEADS))[:, None, :]
    gk = jnp.tile(g_knorm, (1, SB_HEADS))[:, None, :]
    lane = jnp.arange(LANES)
    e64 = ((lane[:, None] // HEAD_DIM == lane[None, :] // HEAD_DIM).astype(F32) / HEAD_DIM).astype(BF16)
    vec = lambda t, l: t[l][None, :]

    cache_kt = cache_k.transpose(0, 1, 3, 4, 2).reshape(depth, n_phys, SB_W, PAGE_SIZE)
    cache_vt = cache_v.transpose(0, 1, 3, 4, 2).reshape(depth, n_phys, SB_W, PAGE_SIZE)

    xp = x_prompt.reshape(n_p, D_MODEL)
    xs = x_sample.reshape(bs, D_MODEL)
    tm_p = _row_tile(n_p)
    tm_s = _row_tile(bs)
    k_all = jnp.zeros((depth, n_p, SB_W), F32)
    v_all = jnp.zeros((depth, n_p, SB_W), F32)
    outs = {name: [] for name in ("ks", "vs", "hp", "hs", "cp", "cs", "sp", "ss")}

    for l in range(depth):
        lru_w = (conv_w[l], vec(conv_b, l), wa_bd[l], vec(lru_ba, l), wi_bd[l], vec(lru_bi, l),
                 vec(lru_lambda, l))
        inproj_w = (vec(g_mix, l), w_in_p[l], gq[l], gk[l], w_al[l], vec(gla_b_alpha, l), e64)

        xp = _ffn(xp, vec(g_ffn1, l), wgu1[l], wd1[l], tm_p)
        q, k_all, v_all, xl, gate, qg, kg, vg, rg, la = _inproj(
            xp, *inproj_w, tm_p, (k_all, v_all), l)
        osb = _sb_prompt(q, k_all, v_all, sb_bias[l], bp, seq, kv_layer=l)
        olru, h_last, conv_new = _lru_prompt(xl, *lru_w, bp, seq)
        ogla, s_fin = _gla_prompt(qg, kg, vg, la, bp, seq)
        xp = _merge(xp, osb, olru, ogla, gate, rg, vec(g_mix_out, l), e64, w_out_b[l], tm_p)
        xp = _ffn(xp, vec(g_ffn2, l), wgu2[l], wd2[l], tm_p)
        outs["hp"].append(h_last.reshape(bp, LRU_W))
        outs["cp"].append(conv_new)
        outs["sp"].append(s_fin.reshape(bp, GLA_HEADS, GLA_DK, GLA_DV))

        xs = _ffn(xs, vec(g_ffn1, l), wgu1[l], wd1[l], tm_s)
        q, k_new, v_new, xl, gate, qg, kg, vg, rg, la = _inproj(xs, *inproj_w, tm_s, None, l)
        bias_b = jnp.broadcast_to(sb_bias[l][:, None], (SB_HEADS, PAGE_SIZE))
        osb = _sb_decode(q.reshape(bs, SB_W, 1), bias_b, cache_kt, cache_vt, page_table, l)
        h_new, conv_new, ogla, s_new = _step(
            xl, state_lru_conv[l].transpose(1, 0, 2), state_lru_h[l], *lru_w,
            qg.reshape(bs, GLA_QK, 1), kg.reshape(bs, GLA_QK, 1), la.reshape(bs, GLA_QK, 1),
            vg.reshape(bs, GLA_HEADS, GLA_DV), state_gla[l].reshape(bs, GLA_QK, GLA_DV))
        xs = _merge(xs, osb.reshape(bs, SB_W), h_new, ogla.reshape(bs, GLA_W), gate, rg,
                    vec(g_mix_out, l), e64, w_out_b[l], tm_s)
        xs = _ffn(xs, vec(g_ffn2, l), wgu2[l], wd2[l], tm_s)
        outs["ks"].append(k_new.reshape(bs, 1, SB_HEADS, HEAD_DIM))
        outs["vs"].append(v_new.reshape(bs, 1, SB_HEADS, HEAD_DIM))
        outs["hs"].append(h_new)
        outs["cs"].append(conv_new.transpose(1, 0, 2))
        outs["ss"].append(s_new.reshape(bs, GLA_HEADS, GLA_DK, GLA_DV))

    st = lambda name: jnp.stack(outs[name])
    return (xp.reshape(bp, seq, D_MODEL), xs.reshape(bs, 1, D_MODEL),
            k_all.reshape(depth, bp, seq, SB_HEADS, HEAD_DIM),
            v_all.reshape(depth, bp, seq, SB_HEADS, HEAD_DIM),
            st("ks"), st("vs"), st("hp"), st("hs"), st("cp"), st("cs"), st("sp"), st("ss"))
```

```python
import functools

import jax
import jax.numpy as jnp
from jax import lax
from jax.experimental import pallas as pl
from jax.experimental.pallas import tpu as pltpu

F32 = jnp.float32
BF16 = jnp.bfloat16

D_MODEL = 1024
HEAD_DIM = 64
SB_W = D_MODEL // 2
SB_HEADS = SB_W // HEAD_DIM
LRU_W = D_MODEL // 4
LRU_BLOCKS = LRU_W // HEAD_DIM
LRU_CONV = 4
LRU_C = 8.0
GLA_W = D_MODEL // 4
GLA_HEADS = GLA_W // HEAD_DIM
GLA_DV = HEAD_DIM
GLA_DK = HEAD_DIM // 2
GLA_QK = GLA_HEADS * GLA_DK
GLA_RANK = 16
GLA_TAU = 16.0
D_MIX = SB_W + LRU_W + GLA_W
D_FF = 2816
EPS = 1e-6
PAGE_SIZE = 128

LANES = 128
SUBLANES = 8
VMEM_LIMIT_BYTES = 56 * 1024 * 1024

_OFF_Q = 0
_OFF_K = _OFF_Q + SB_W
_OFF_V = _OFF_K + SB_W
_OFF_XL = _OFF_V + SB_W
_OFF_GATE = _OFF_XL + LRU_W
_OFF_QG = _OFF_GATE + LRU_W
_OFF_KG = _OFF_QG + GLA_QK
_OFF_VG = _OFF_KG + GLA_QK
_OFF_RG = _OFF_VG + GLA_W
_OFF_ALR = _OFF_RG + GLA_W
D_IN_PAD = _OFF_ALR + LANES

FF_CHUNK = 256
ROW_TILE = 512
SB_BLOCK = 256
GLA_CHUNK = 128
DECODE_PAGES_PER_STEP = 32


def _idiv(x, d):
    assert d & (d - 1) == 0
    return x >> (d.bit_length() - 1)


def _imod(x, d):
    assert d & (d - 1) == 0
    return x & (d - 1)


def _dot(a, b):
    return jnp.dot(a, b, preferred_element_type=F32)


def _dot_nt(a, b):
    return lax.dot_general(a, b, (((1,), (1,)), ((), ())), preferred_element_type=F32)


def _dot_tn(a, b):
    return lax.dot_general(a, b, (((0,), (0,)), ((), ())), preferred_element_type=F32)


def _split_dot(x, w, passes):
    hi = x.astype(BF16)
    acc = _dot(hi, w)
    rem = x - hi.astype(F32)
    for _ in range(passes - 1):
        lo = rem.astype(BF16)
        acc = acc + _dot(lo, w)
        rem = rem - lo.astype(F32)
    return acc


def _split_dot_left(w, x, passes):
    hi = x.astype(BF16)
    acc = _dot(w, hi)
    rem = x - hi.astype(F32)
    for _ in range(passes - 1):
        lo = rem.astype(BF16)
        acc = acc + _dot(w, lo)
        rem = rem - lo.astype(F32)
    return acc


def _rms(x, g):
    ms = jnp.mean(x * x, axis=-1, keepdims=True)
    return x * lax.rsqrt(ms + EPS) * g


def _softplus(z):
    return jnp.maximum(z, 0.0) + jnp.log1p(jnp.exp(-jnp.abs(z)))


def _log_sigmoid(z):
    return jnp.minimum(z, 0.0) - jnp.log1p(jnp.exp(-jnp.abs(z)))


def _gelu_tanh(x):
    return 0.5 * x * (1.0 + jnp.tanh(0.7978845608028654 * (x + 0.044715 * (x * x * x))))


def _head_rms_cols(x, e64, g):
    cols = []
    for c in range(x.shape[1] // LANES):
        xc = x[:, c * LANES:(c + 1) * LANES]
        ms = _split_dot(xc * xc, e64, 2)
        cols.append(xc * lax.rsqrt(ms + EPS) * g[:, c * LANES:(c + 1) * LANES])
    return cols


def _resident(shape):
    nd = len(shape)
    return pl.BlockSpec(shape, lambda *_: (0,) * nd, pipeline_mode=pl.Buffered(1))


def _params(semantics):
    return pltpu.CompilerParams(dimension_semantics=semantics, vmem_limit_bytes=VMEM_LIMIT_BYTES)


def _ffn_kernel(x_ref, g_ref, wgu_ref, wd_ref, o_ref, h_ref, acc_ref):
    x = x_ref[...]
    h_ref[...] = _rms(x, g_ref[...]).astype(BF16)
    acc_ref[...] = jnp.zeros_like(acc_ref)

    def body(c, carry):
        gu = _dot(h_ref[...], wgu_ref[c])
        gate = gu[:, :FF_CHUNK]
        up = gu[:, FF_CHUNK:]
        act = (gate * jax.nn.sigmoid(gate) * up).astype(BF16)
        acc_ref[...] += _dot(act, wd_ref[c])
        return carry

    lax.fori_loop(0, D_FF // FF_CHUNK, body, 0)
    o_ref[...] = x + 0.5 * acc_ref[...]


def _ffn(x, g, wgu, wd, tm):
    n = x.shape[0]
    nc = D_FF // FF_CHUNK
    return pl.pallas_call(
        _ffn_kernel,
        grid=(n // tm,),
        in_specs=[
            pl.BlockSpec((tm, D_MODEL), lambda i: (i, 0)),
            _resident((1, D_MODEL)),
            _resident((nc, D_MODEL, 2 * FF_CHUNK)),
            _resident((nc, FF_CHUNK, D_MODEL)),
        ],
        out_specs=pl.BlockSpec((tm, D_MODEL), lambda i: (i, 0)),
        out_shape=jax.ShapeDtypeStruct((n, D_MODEL), F32),
        scratch_shapes=[pltpu.VMEM((tm, D_MODEL), BF16), pltpu.VMEM((tm, D_MODEL), F32)],
        compiler_params=_params(("parallel",)),
        name="ffn",
    )(x, g, wgu, wd)


def _inproj_kernel(*refs, transposed_kv):
    if transposed_kv:
        (x_ref, g_ref, w_ref, gq_ref, gk_ref, wal_ref, bal_ref, e64_ref, wkvt_ref,
         q_ref, k_ref, v_ref, xl_ref, gate_ref, qg_ref, kg_ref, vg_ref, rg_ref, la_ref, h_ref) = refs
    else:
        (x_ref, g_ref, w_ref, gq_ref, gk_ref, wal_ref, bal_ref, e64_ref,
         q_ref, k_ref, v_ref, xl_ref, gate_ref, qg_ref, kg_ref, vg_ref, rg_ref, la_ref, h_ref) = refs
    h_ref[...] = _rms(x_ref[...], g_ref[...]).astype(BF16)

    def proj(lo, width):
        return _dot(h_ref[...], w_ref[:, lo:lo + width])

    e64 = e64_ref[...]
    q_cols = _head_rms_cols(proj(_OFF_Q, SB_W), e64, gq_ref[...])
    for c, col in enumerate(q_cols):
        q_ref[:, c * LANES:(c + 1) * LANES] = col * (HEAD_DIM ** -0.5)
    if transposed_kv:
        rows = h_ref.shape[0]
        k_t = _dot_nt(wkvt_ref[:SB_W, :], h_ref[...]).reshape(SB_HEADS, HEAD_DIM, rows)
        ms = jnp.mean(k_t * k_t, axis=1, keepdims=True)
        k_ref[...] = (k_t * lax.rsqrt(ms + EPS)).reshape(SB_W, rows) * gk_ref[...]
        v_ref[...] = _dot_nt(wkvt_ref[SB_W:, :], h_ref[...])
    else:
        k_cols = _head_rms_cols(proj(_OFF_K, SB_W), e64, gk_ref[...])
        for c, col in enumerate(k_cols):
            k_ref[:, c * LANES:(c + 1) * LANES] = col
        v_ref[...] = proj(_OFF_V, SB_W)
    xl_ref[...] = proj(_OFF_XL, LRU_W)
    gate_ref[...] = _gelu_tanh(proj(_OFF_GATE, LRU_W))
    qg_ref[...] = proj(_OFF_QG, GLA_QK)
    kg_ref[...] = proj(_OFF_KG, GLA_QK)
    vg_ref[...] = proj(_OFF_VG, GLA_W)
    r = proj(_OFF_RG, GLA_W)
    rg_ref[...] = r * jax.nn.sigmoid(r)
    a_lr = proj(_OFF_ALR, LANES)
    xa = _dot(a_lr.astype(BF16), wal_ref[...]) + bal_ref[...]
    la_ref[...] = _log_sigmoid(xa) * (1.0 / GLA_TAU)


def _inproj(x, g, w_in, gq, gk, wal, bal, e64, tm, kv_slabs=None, layer=None, w_kvt=None, seq=None):
    n = x.shape[0]
    row = lambda w: pl.BlockSpec((tm, w), lambda i: (i, 0))
    in_specs = [
        row(D_MODEL),
        _resident((1, D_MODEL)),
        _resident((D_MODEL, D_IN_PAD)),
        _resident((1, SB_W)),
        _resident(gk.shape),
        _resident((LANES, LANES)),
        _resident((1, LANES)),
        _resident((LANES, LANES)),
    ]
    args = [x, g, w_in, gq, gk, wal, bal, e64]
    small = lambda w: jax.ShapeDtypeStruct((n, w), F32)
    transposed_kv = kv_slabs is not None
    if not transposed_kv:
        kv_specs = [row(SB_W), row(SB_W)]
        kv_shapes = [small(SB_W), small(SB_W)]
        aliases = {}
        kernel = functools.partial(_inproj_kernel, transposed_kv=False)
    else:
        k_all, v_all = kv_slabs
        tiles = seq // tm
        slab = pl.BlockSpec((None, None, SB_W, tm), lambda i: (layer, i // tiles, 0, i % tiles))
        kv_specs = [slab, slab]
        kv_shapes = [jax.ShapeDtypeStruct(k_all.shape, F32), jax.ShapeDtypeStruct(v_all.shape, F32)]
        in_specs += [_resident((2 * SB_W, D_MODEL)),
                     pl.BlockSpec(memory_space=pl.ANY), pl.BlockSpec(memory_space=pl.ANY)]
        args += [w_kvt, k_all, v_all]
        aliases = {len(args) - 2: 1, len(args) - 1: 2}

        def kernel(*refs):
            _inproj_kernel(*refs[:9], *refs[11:], transposed_kv=True)

    out_specs = [row(SB_W)] + kv_specs + [row(LRU_W), row(LRU_W), row(GLA_QK), row(GLA_QK),
                                          row(GLA_W), row(GLA_W), row(GLA_QK)]
    out_shape = [small(SB_W)] + kv_shapes + [small(LRU_W), small(LRU_W), small(GLA_QK), small(GLA_QK),
                                             small(GLA_W), small(GLA_W), small(GLA_QK)]
    return pl.pallas_call(
        kernel,
        grid=(n // tm,),
        in_specs=in_specs,
        out_specs=out_specs,
        out_shape=out_shape,
        input_output_aliases=aliases,
        scratch_shapes=[pltpu.VMEM((tm, D_MODEL), BF16)],
        compiler_params=_params(("parallel",)),
        name="inproj",
    )(*args)


def _merge_kernel(x_ref, osb_ref, olru_ref, ogla_ref, gate_ref, rg_ref, gmo_ref, e64_ref, wout_ref, o_ref):
    e64 = e64_ref[...]
    gmo = gmo_ref[...]
    sb = _head_rms_cols(osb_ref[...], e64, gmo[:, :SB_W])
    lru = _head_rms_cols(olru_ref[...], e64, gmo[:, SB_W:SB_W + LRU_W])
    gla = _head_rms_cols(ogla_ref[...], e64, gmo[:, SB_W + LRU_W:])
    gate = gate_ref[...]
    rg = rg_ref[...]
    lru = [col * gate[:, c * LANES:(c + 1) * LANES] for c, col in enumerate(lru)]
    gla = [col * rg[:, c * LANES:(c + 1) * LANES] for c, col in enumerate(gla)]
    acc = x_ref[...]
    for c, col in enumerate(sb + lru + gla):
        acc = acc + _dot(col.astype(BF16), wout_ref[c * LANES:(c + 1) * LANES, :])
    o_ref[...] = acc


def _merge(x, osb, olru, ogla, gate, rg, gmo, e64, wout, tm):
    n = x.shape[0]
    row = lambda w: pl.BlockSpec((tm, w), lambda i: (i, 0))
    return pl.pallas_call(
        _merge_kernel,
        grid=(n // tm,),
        in_specs=[row(D_MODEL), row(SB_W), row(LRU_W), row(GLA_W), row(LRU_W), row(GLA_W),
                  _resident((1, D_MIX)), _resident((LANES, LANES)), _resident((D_MIX, D_MODEL))],
        out_specs=row(D_MODEL),
        out_shape=jax.ShapeDtypeStruct((n, D_MODEL), F32),
        compiler_params=_params(("parallel",)),
        name="merge",
    )(x, osb, olru, ogla, gate, rg, gmo, e64, wout)


def _softplus_plain(z):
    return jnp.maximum(z, 0.0) + jnp.log(1.0 + jnp.exp(-jnp.abs(z)))


def _sb_tile_stages(q_h, kt_b, vt_h, bias, upper, carry, mask):
    st = {}

    def logits():
        st["z"] = _dot(q_h, kt_b) + bias

    def log_terms():
        z = st["z"]
        sp = _softplus_plain(z)
        l1m = -sp if mask is None else jnp.where(mask, -sp, 0.0)
        st["lb"] = z - sp
        st["edge"] = l1m[:, 0:1]
        hi = l1m.astype(BF16)
        st["hi"] = hi
        st["lo"] = (l1m - hi.astype(F32)).astype(BF16)

    def suffix_sums():
        st["within"] = _dot(st["hi"], upper) + _dot(st["lo"], upper)

    def weights():
        within = st["within"]
        a = jnp.exp(st["lb"] + within + carry)
        if mask is not None:
            a = jnp.where(mask, a, 0.0)
        st["a"] = a.astype(BF16)
        st["carry"] = carry + within[:, 0:1] + st["edge"]

    def values():
        return _dot_nt(st["a"], vt_h), st["carry"]

    return [logits, log_terms, suffix_sums, weights, values]


def _run_skewed(pipelines):
    depth = len(pipelines[0])
    results = [None] * len(pipelines)
    for step in range(len(pipelines) + depth - 1):
        for p, stages in enumerate(pipelines):
            s = step - p
            if 0 <= s < depth:
                results[p] = stages[s]()
    return results


def _sb_prompt_kernel(bias_ref, q_ref, kt_ref, vt_ref, o_ref, kb_ref, vh_ref, *, blk):
    pairs = SB_W // LANES
    i = pl.program_id(1)
    first = lax.broadcasted_iota(jnp.int32, (1, LANES), 1) < HEAD_DIM

    @pl.when(i == 0)
    def _stage():
        kb_ref[...] = kt_ref[...].astype(BF16)
        first_rows = lax.broadcasted_iota(jnp.int32, (LANES, vt_ref.shape[1]), 0) < HEAD_DIM
        for p in range(pairs):
            v = vt_ref[p * LANES:(p + 1) * LANES, :]
            vh_ref[2 * p] = jnp.where(first_rows, v, 0.0).astype(BF16)
            vh_ref[2 * p + 1] = jnp.where(first_rows, 0.0, v).astype(BF16)

    q_heads = []
    biases = []
    for p in range(pairs):
        q = q_ref[:, p * LANES:(p + 1) * LANES]
        q_heads += [jnp.where(first, q, 0.0).astype(BF16), jnp.where(first, 0.0, q).astype(BF16)]
        biases += [bias_ref[2 * p], bias_ref[2 * p + 1]]
    r_i = lax.broadcasted_iota(jnp.int32, (blk, blk), 0)
    c_i = lax.broadcasted_iota(jnp.int32, (blk, blk), 1)
    upper = jnp.where(r_i > c_i, 1.0, 0.0).astype(BF16)
    causal = c_i < r_i

    def tile(j, carries, accs, mask):
        cols = pl.ds(pl.multiple_of(j * blk, blk), blk)
        pipelines = []
        for h in range(2 * pairs):
            kt_b = kb_ref[(h // 2) * LANES:(h // 2 + 1) * LANES, cols]
            pipelines.append(_sb_tile_stages(q_heads[h], kt_b, vh_ref[h, :, cols], biases[h], upper,
                                             carries[h], mask))
        results = _run_skewed(pipelines)
        new_accs = list(accs)
        for h, (out, _) in enumerate(results):
            new_accs[h // 2] = new_accs[h // 2] + out
        return tuple(c for _, c in results), tuple(new_accs)

    zero_c = jnp.zeros((blk, 1), F32)
    zero_a = jnp.zeros((blk, LANES), F32)
    state = tile(i, (zero_c,) * (2 * pairs), (zero_a,) * pairs, causal)

    def body(it, state):
        return tile(i - 1 - it, state[0], state[1], None)

    _, accs = lax.fori_loop(0, i, body, state)
    for p in range(pairs):
        o_ref[:, p * LANES:(p + 1) * LANES] = accs[p]


def _sb_prompt(q, kt_all, vt_all, bias, layer):
    _, batch, _, seq = kt_all.shape
    blk = min(SB_BLOCK, seq)
    nq = seq // blk
    kv_spec = pl.BlockSpec((None, None, SB_W, seq), lambda b, i, *_: (layer, b, 0, 0))
    return pl.pallas_call(
        functools.partial(_sb_prompt_kernel, blk=blk),
        grid_spec=pltpu.PrefetchScalarGridSpec(
            num_scalar_prefetch=1,
            grid=(batch, nq),
            in_specs=[pl.BlockSpec((blk, SB_W), lambda b, i, *_: (b * nq + i, 0)), kv_spec, kv_spec],
            out_specs=pl.BlockSpec((blk, SB_W), lambda b, i, *_: (b * nq + i, 0)),
            scratch_shapes=[pltpu.VMEM((SB_W, seq), BF16), pltpu.VMEM((SB_HEADS, LANES, seq), BF16)],
        ),
        out_shape=jax.ShapeDtypeStruct((batch * seq, SB_W), F32),
        compiler_params=_params(("parallel", "arbitrary")),
        name="sb_prompt",
    )(bias, q, kt_all, vt_all)


def _sb_decode_kernel(pt_ref, q_ref, bias_ref, *refs, n_pages, group):
    k_refs = refs[:group]
    v_refs = refs[group:2 * group]
    o_ref = refs[2 * group]
    z_ref, a_ref, acc_ref = refs[2 * group + 1:]
    s = pl.program_id(1)
    steps = n_pages // group
    rows = n_pages * SB_HEADS

    @pl.when(s < steps)
    def _scores():
        q_col = q_ref[...]
        for g in range(group):
            prod = (k_refs[g][...] * q_col).reshape(SB_HEADS, HEAD_DIM, PAGE_SIZE)
            z = jnp.sum(prod, axis=1) + bias_ref[...]
            row0 = pl.multiple_of((s * group + g) * SB_HEADS, SB_HEADS)
            z_ref[pl.ds(row0, SB_HEADS), :] = z

    @pl.when(s == steps - 1)
    def _weights():
        z = z_ref[...]
        sp = _softplus(z)
        l1m = -sp
        r_i = lax.broadcasted_iota(jnp.int32, (PAGE_SIZE, PAGE_SIZE), 0)
        c_i = lax.broadcasted_iota(jnp.int32, (PAGE_SIZE, PAGE_SIZE), 1)
        upper = (r_i > c_i).astype(BF16)
        ones = jnp.ones((PAGE_SIZE, PAGE_SIZE), BF16)
        within = _split_dot(l1m, upper, 3)
        page_total = _split_dot(l1m, ones, 3)
        pr = lax.broadcasted_iota(jnp.int32, (rows, rows), 0)
        pc = lax.broadcasted_iota(jnp.int32, (rows, rows), 1)
        same_head = _imod(pc, SB_HEADS) == _imod(pr, SB_HEADS)
        later_page = jnp.where(_idiv(pc, SB_HEADS) > _idiv(pr, SB_HEADS),
                               jnp.where(same_head, 1.0, 0.0), 0.0).astype(BF16)
        later = _split_dot_left(later_page, page_total, 3)
        a_ref[...] = jnp.exp(z - sp + within + later)
        acc_ref[...] = jnp.zeros_like(acc_ref)

    @pl.when(s >= steps)
    def _values():
        acc = acc_ref[...]
        for g in range(group):
            row0 = pl.multiple_of(((s - steps) * group + g) * SB_HEADS, SB_HEADS)
            a_g = a_ref[pl.ds(row0, SB_HEADS), :]
            a_rows = jnp.concatenate(
                [jnp.broadcast_to(a_g[h:h + 1, :], (HEAD_DIM, PAGE_SIZE)) for h in range(SB_HEADS)], axis=0)
            acc = acc + v_refs[g][...] * a_rows
        acc_ref[...] = acc

    @pl.when(s == 2 * steps - 1)
    def _emit():
        o_ref[...] = jnp.sum(acc_ref[...], axis=1, keepdims=True)


def _sb_decode(q_col, bias_b, cache_kt, cache_vt, page_table, layer):
    b, n_pages = page_table.shape
    group = min(DECODE_PAGES_PER_STEP, n_pages)
    steps = n_pages // group

    def k_map(g):
        return lambda bi, s, pt: (layer, pt[bi, jnp.minimum(s, steps - 1) * group + g], 0, 0)

    def v_map(g):
        return lambda bi, s, pt: (layer, pt[bi, jnp.maximum(s - steps, 0) * group + g], 0, 0)

    page = lambda m: pl.BlockSpec((None, None, SB_W, PAGE_SIZE), m)
    in_specs = [pl.BlockSpec((None, SB_W, 1), lambda bi, s, pt: (bi, 0, 0)),
                pl.BlockSpec((SB_HEADS, PAGE_SIZE), lambda bi, s, pt: (0, 0))]
    in_specs += [page(k_map(g)) for g in range(group)]
    in_specs += [page(v_map(g)) for g in range(group)]
    rows = n_pages * SB_HEADS
    return pl.pallas_call(
        functools.partial(_sb_decode_kernel, n_pages=n_pages, group=group),
        grid_spec=pltpu.PrefetchScalarGridSpec(
            num_scalar_prefetch=1,
            grid=(b, 2 * steps),
            in_specs=in_specs,
            out_specs=pl.BlockSpec((None, SB_W, 1), lambda bi, s, pt: (bi, 0, 0)),
            scratch_shapes=[pltpu.VMEM((rows, PAGE_SIZE), F32),
                            pltpu.VMEM((rows, PAGE_SIZE), F32),
                            pltpu.VMEM((SB_W, PAGE_SIZE), F32)],
        ),
        out_shape=jax.ShapeDtypeStruct((b, SB_W, 1), F32),
        compiler_params=_params(("parallel", "arbitrary")),
        name="sb_decode",
    )(page_table, q_col, bias_b, *([cache_kt] * group), *([cache_vt] * group))


def _lru_gates(xc, wa_ref, ba_ref, wi_ref, bi_ref, lam_ref):
    xb = xc.astype(BF16)
    r = jax.nn.sigmoid(_dot(xb, wa_ref[...]) + ba_ref[...])
    i = jax.nn.sigmoid(_dot(xb, wi_ref[...]) + bi_ref[...])
    log_a = -LRU_C * r * _softplus(-lam_ref[...])
    a = jnp.exp(log_a)
    t = jnp.tanh(log_a)
    one_minus_a2 = -2.0 * t / (1.0 - t)
    return a, jnp.sqrt(one_minus_a2) * (i * xc)


def _lru_prompt_kernel(x_ref, cw_ref, cb_ref, wa_ref, ba_ref, wi_ref, bi_ref, lam_ref,
                       o_ref, hl_ref, cn_ref, *, seq):
    x = x_ref[...]
    rows = lax.broadcasted_iota(jnp.int32, (seq, LRU_W), 0)
    xc = cb_ref[...] + cw_ref[LRU_CONV - 1:LRU_CONV, :] * x
    for j in range(1, LRU_CONV):
        shifted = jnp.where(rows >= j, pltpu.roll(x, j, 0), 0.0)
        xc = xc + cw_ref[LRU_CONV - 1 - j:LRU_CONV - j, :] * shifted
    a, g = _lru_gates(xc, wa_ref, ba_ref, wi_ref, bi_ref, lam_ref)
    step = 1
    while step < seq:
        valid = rows >= step
        a_prev = pltpu.roll(a, step, 0)
        g_prev = pltpu.roll(g, step, 0)
        g = jnp.where(valid, a * g_prev + g, g)
        a = jnp.where(valid, a * a_prev, a)
        step *= 2
    o_ref[...] = g
    hl_ref[...] = g[seq - 1:seq, :]
    cn_ref[...] = x[seq - (LRU_CONV - 1):, :]


def _lru_prompt(xl, cw, cb, wa, ba, wi, bi, lam, batch, seq):
    vec = _resident((1, LRU_W))
    mat = _resident((LRU_W, LRU_W))
    return pl.pallas_call(
        functools.partial(_lru_prompt_kernel, seq=seq),
        grid=(batch,),
        in_specs=[pl.BlockSpec((seq, LRU_W), lambda b: (b, 0)), _resident((LRU_CONV, LRU_W)), vec,
                  mat, vec, mat, vec, vec],
        out_specs=[pl.BlockSpec((seq, LRU_W), lambda b: (b, 0)),
                   pl.BlockSpec((None, 1, LRU_W), lambda b: (b, 0, 0)),
                   pl.BlockSpec((None, LRU_CONV - 1, LRU_W), lambda b: (b, 0, 0))],
        out_shape=[jax.ShapeDtypeStruct((batch * seq, LRU_W), F32),
                   jax.ShapeDtypeStruct((batch, 1, LRU_W), F32),
                   jax.ShapeDtypeStruct((batch, LRU_CONV - 1, LRU_W), F32)],
        compiler_params=_params(("parallel",)),
        name="lru_prompt",
    )(xl, cw, cb, wa, ba, wi, bi, lam)


def _gla_prompt_kernel(q_ref, k_ref, v_ref, la_ref, o_ref, sfin_ref, s_ref, *, chunk, n_chunks):
    ci = pl.program_id(1)

    @pl.when(ci == 0)
    def _init():
        s_ref[...] = jnp.zeros_like(s_ref)

    qs = q_ref[...] * (GLA_DK ** -0.5)
    k = k_ref[...]
    v = v_ref[...]
    r_i = lax.broadcasted_iota(jnp.int32, (chunk, chunk), 0)
    c_i = lax.broadcasted_iota(jnp.int32, (chunk, chunk), 1)
    lower = (c_i <= r_i).astype(BF16)
    bc = _split_dot_left(lower, la_ref[...], 3)
    rows = lax.broadcasted_iota(jnp.int32, (chunk, GLA_QK), 0)
    qk_head = _idiv(lax.broadcasted_iota(jnp.int32, (1, GLA_QK), 1), GLA_DK)
    v_head = _idiv(lax.broadcasted_iota(jnp.int32, (1, GLA_W), 1), GLA_DV)
    own = (_idiv(lax.broadcasted_iota(jnp.int32, (GLA_QK, GLA_W), 0), GLA_DK)
           == _idiv(lax.broadcasted_iota(jnp.int32, (GLA_QK, GLA_W), 1), GLA_DV))

    state = s_ref[...]
    out = _dot((qs * jnp.exp(bc)).astype(BF16), state.astype(BF16))

    scores = [jnp.zeros((chunk, chunk), F32) for _ in range(GLA_HEADS)]
    level = 2 * SUBLANES
    while level <= chunk:
        half = level // 2
        mid = jnp.concatenate(
            [jnp.broadcast_to(bc[b * level + half - 1:b * level + half, :], (level, GLA_QK))
             for b in range(chunk // level)], axis=0)
        second = (rows & (level - 1)) >= half
        q_dec = jnp.where(second, qs * jnp.exp(bc - mid), 0.0)
        k_dec = jnp.where(second, 0.0, k * jnp.exp(mid - bc)).astype(BF16)
        same_block = _idiv(r_i, level) == _idiv(c_i, level)
        for h in range(GLA_HEADS):
            sc = _dot_nt(jnp.where(qk_head == h, q_dec, 0.0).astype(BF16), k_dec)
            if level < chunk:
                sc = jnp.where(same_block, sc, 0.0)
            scores[h] = scores[h] + sc
        level *= 2
    for h in range(GLA_HEADS):
        out = out + _dot(scores[h].astype(BF16), jnp.where(v_head == h, v, 0.0).astype(BF16))

    expand = jnp.where(own, 1.0, 0.0).astype(BF16)
    out = out + _split_dot(qs * k, expand, 2) * v
    for j in range(1, SUBLANES):
        ok = (rows & (SUBLANES - 1)) >= j
        p = jnp.where(ok, qs * pltpu.roll(k, j, 0) * jnp.exp(bc - pltpu.roll(bc, j, 0)), 0.0)
        out = out + _split_dot(p, expand, 2) * pltpu.roll(v, j, 0)
    o_ref[...] = out

    last = bc[chunk - 1:chunk, :]
    kv = _dot_tn((k * jnp.exp(last - bc)).astype(BF16), v.astype(BF16))
    dec_col = jnp.transpose(jnp.broadcast_to(jnp.exp(last), (GLA_QK, GLA_QK)))
    dec = jnp.concatenate([dec_col] * (GLA_W // GLA_QK), axis=1)
    new_state = dec * state + jnp.where(own, kv, 0.0)
    s_ref[...] = new_state

    @pl.when(ci == n_chunks - 1)
    def _emit():
        for h in range(GLA_HEADS):
            sfin_ref[h * GLA_DK:(h + 1) * GLA_DK, :] = (
                new_state[h * GLA_DK:(h + 1) * GLA_DK, h * GLA_DV:(h + 1) * GLA_DV])


def _gla_prompt(qg, kg, vg, la, batch, seq):
    chunk = min(GLA_CHUNK, seq)
    n_chunks = seq // chunk
    row = lambda w: pl.BlockSpec((chunk, w), lambda b, c: (b * n_chunks + c, 0))
    return pl.pallas_call(
        functools.partial(_gla_prompt_kernel, chunk=chunk, n_chunks=n_chunks),
        grid=(batch, n_chunks),
        in_specs=[row(GLA_QK), row(GLA_QK), row(GLA_W), row(GLA_QK)],
        out_specs=[row(GLA_W), pl.BlockSpec((None, GLA_QK, GLA_DV), lambda b, c: (b, 0, 0))],
        out_shape=[jax.ShapeDtypeStruct((batch * seq, GLA_W), F32),
                   jax.ShapeDtypeStruct((batch, GLA_QK, GLA_DV), F32)],
        scratch_shapes=[pltpu.VMEM((GLA_QK, GLA_W), F32)],
        compiler_params=_params(("parallel", "arbitrary")),
        name="gla_prompt",
    )(qg, kg, vg, la)


def _step_kernel(x_ref, conv_ref, h0_ref, cw_ref, cb_ref, wa_ref, ba_ref, wi_ref, bi_ref, lam_ref,
                 qc_ref, kc_ref, lac_ref, v_ref, s0_ref,
                 h_ref, cn_ref, og_ref, s_ref):
    x = x_ref[...]
    xc = cb_ref[...] + cw_ref[LRU_CONV - 1:LRU_CONV, :] * x
    for j in range(LRU_CONV - 1):
        xc = xc + cw_ref[j:j + 1, :] * conv_ref[j]
    a, g = _lru_gates(xc, wa_ref, ba_ref, wi_ref, bi_ref, lam_ref)
    h_ref[...] = a * h0_ref[...] + g
    for j in range(LRU_CONV - 2):
        cn_ref[j] = conv_ref[j + 1]
    cn_ref[LRU_CONV - 2] = x

    alpha = jnp.exp(lac_ref[...])
    kc = kc_ref[...]
    qc = qc_ref[...] * (GLA_DK ** -0.5)
    for h in range(GLA_HEADS):
        rs = slice(h * GLA_DK, (h + 1) * GLA_DK)
        new = alpha[:, rs, :] * s0_ref[:, rs, :] + kc[:, rs, :] * v_ref[:, h:h + 1, :]
        s_ref[:, rs, :] = new
        og_ref[:, h:h + 1, :] = jnp.sum(qc[:, rs, :] * new, axis=1, keepdims=True)


def _step(xl, conv0, h0, cw, cb, wa, ba, wi, bi, lam, q_col, k_col, la_col, v3, s0):
    b = xl.shape[0]
    return pl.pallas_call(
        _step_kernel,
        out_shape=[jax.ShapeDtypeStruct((b, LRU_W), F32),
                   jax.ShapeDtypeStruct((LRU_CONV - 1, b, LRU_W), F32),
                   jax.ShapeDtypeStruct((b, GLA_HEADS, GLA_DV), F32),
                   jax.ShapeDtypeStruct((b, GLA_QK, GLA_DV), F32)],
        compiler_params=pltpu.CompilerParams(vmem_limit_bytes=VMEM_LIMIT_BYTES),
        name="sample_step",
    )(xl, conv0, h0, cw, cb, wa, ba, wi, bi, lam, q_col, k_col, la_col, v3, s0)


def _block_diag(w):
    nb, bi, bj = w.shape
    eye = jnp.eye(nb, dtype=w.dtype)
    return (eye[:, None, :, None] * w[:, :, None, :]).reshape(nb * bi, nb * bj)


def _row_tile(n):
    return min(ROW_TILE, n)


def kernel(x_prompt, x_sample, cache_k, cache_v, page_table, state_lru_h, state_lru_conv, state_gla,
           g_ffn1, w_ffn1_gate, w_ffn1_up, w_ffn1_down, g_mix, w_in, g_qnorm, g_knorm, sb_bias,
           conv_w, conv_b, lru_wa, lru_ba, lru_wi, lru_bi, lru_lambda, gla_w_alpha, gla_b_alpha,
           g_mix_out, w_out, g_ffn2, w_ffn2_gate, w_ffn2_up, w_ffn2_down):
    depth = w_in.shape[0]
    bp, seq, _ = x_prompt.shape
    bs, dec_seq, _ = x_sample.shape
    assert dec_seq == 1
    n_p = bp * seq
    n_phys = cache_k.shape[1]
    nc = D_FF // FF_CHUNK

    def ffn_weights(wg, wu, wd):
        wg = wg.reshape(depth, D_MODEL, nc, FF_CHUNK)
        wu = wu.reshape(depth, D_MODEL, nc, FF_CHUNK)
        wgu = jnp.concatenate([wg, wu], axis=-1).transpose(0, 2, 1, 3).astype(BF16)
        return wgu, wd.reshape(depth, nc, FF_CHUNK, D_MODEL).astype(BF16)

    wgu1, wd1 = ffn_weights(w_ffn1_gate, w_ffn1_up, w_ffn1_down)
    wgu2, wd2 = ffn_weights(w_ffn2_gate, w_ffn2_up, w_ffn2_down)
    w_in_p = jnp.pad(w_in, ((0, 0), (0, 0), (0, D_IN_PAD - w_in.shape[2]))).astype(BF16)
    w_kvt = w_in[:, :, _OFF_K:_OFF_XL].transpose(0, 2, 1).astype(BF16)
    w_al = jnp.pad(gla_w_alpha, ((0, 0), (0, LANES - GLA_RANK), (0, 0))).astype(BF16)
    w_out_b = w_out.astype(BF16)
    wa_bd = jax.vmap(_block_diag)(lru_wa).astype(BF16)
    wi_bd = jax.vmap(_block_diag)(lru_wi).astype(BF16)
    gq = jnp.tile(g_qnorm, (1, SB_HEADS))[:, None, :]
    gk = jnp.tile(g_knorm, (1, SB_HEADS))[:, None, :]
    gk_col = jnp.tile(g_knorm, (1, SB_HEADS))[:, :, None]
    lane = jnp.arange(LANES)
    e64 = ((lane[:, None] // HEAD_DIM == lane[None, :] // HEAD_DIM).astype(F32) / HEAD_DIM).astype(BF16)
    vec = lambda t, l: t[l][None, :]

    cache_kt = cache_k.transpose(0, 1, 3, 4, 2).reshape(depth, n_phys, SB_W, PAGE_SIZE)
    cache_vt = cache_v.transpose(0, 1, 3, 4, 2).reshape(depth, n_phys, SB_W, PAGE_SIZE)

    xp = x_prompt.reshape(n_p, D_MODEL)
    xs = x_sample.reshape(bs, D_MODEL)
    tm_p = _row_tile(n_p)
    tm_s = _row_tile(bs)
    kt_all = jnp.zeros((depth, bp, SB_W, seq), F32)
    vt_all = jnp.zeros((depth, bp, SB_W, seq), F32)
    outs = {name: [] for name in ("ks", "vs", "hp", "hs", "cp", "cs", "sp", "ss")}

    for l in range(depth):
        lru_w = (conv_w[l], vec(conv_b, l), wa_bd[l], vec(lru_ba, l), wi_bd[l], vec(lru_bi, l),
                 vec(lru_lambda, l))
        inproj_w = (vec(g_mix, l), w_in_p[l], gq[l])
        inproj_w2 = (w_al[l], vec(gla_b_alpha, l), e64)

        xp = _ffn(xp, vec(g_ffn1, l), wgu1[l], wd1[l], tm_p)
        q, kt_all, vt_all, xl, gate, qg, kg, vg, rg, la = _inproj(
            xp, *inproj_w, gk_col[l], *inproj_w2, tm_p, (kt_all, vt_all), l, w_kvt[l], seq)
        osb = _sb_prompt(q, kt_all, vt_all, sb_bias[l], l)
        olru, h_last, conv_new = _lru_prompt(xl, *lru_w, bp, seq)
        ogla, s_fin = _gla_prompt(qg, kg, vg, la, bp, seq)
        xp = _merge(xp, osb, olru, ogla, gate, rg, vec(g_mix_out, l), e64, w_out_b[l], tm_p)
        xp = _ffn(xp, vec(g_ffn2, l), wgu2[l], wd2[l], tm_p)
        outs["hp"].append(h_last.reshape(bp, LRU_W))
        outs["cp"].append(conv_new)
        outs["sp"].append(s_fin.reshape(bp, GLA_HEADS, GLA_DK, GLA_DV))

        xs = _ffn(xs, vec(g_ffn1, l), wgu1[l], wd1[l], tm_s)
        q, k_new, v_new, xl, gate, qg, kg, vg, rg, la = _inproj(xs, *inproj_w, gk[l], *inproj_w2, tm_s)
        bias_b = jnp.broadcast_to(sb_bias[l][:, None], (SB_HEADS, PAGE_SIZE))
        osb = _sb_decode(q.reshape(bs, SB_W, 1), bias_b, cache_kt, cache_vt, page_table, l)
        h_new, conv_new, ogla, s_new = _step(
            xl, state_lru_conv[l].transpose(1, 0, 2), state_lru_h[l], *lru_w,
            qg.reshape(bs, GLA_QK, 1), kg.reshape(bs, GLA_QK, 1), la.reshape(bs, GLA_QK, 1),
            vg.reshape(bs, GLA_HEADS, GLA_DV), state_gla[l].reshape(bs, GLA_QK, GLA_DV))
        xs = _merge(xs, osb.reshape(bs, SB_W), h_new, ogla.reshape(bs, GLA_W), gate, rg,
                    vec(g_mix_out, l), e64, w_out_b[l], tm_s)
        xs = _ffn(xs, vec(g_ffn2, l), wgu2[l], wd2[l], tm_s)
        outs["ks"].append(k_new.reshape(bs, 1, SB_HEADS, HEAD_DIM))
        outs["vs"].append(v_new.reshape(bs, 1, SB_HEADS, HEAD_DIM))
        outs["hs"].append(h_new)
        outs["cs"].append(conv_new.transpose(1, 0, 2))
        outs["ss"].append(s_new.reshape(bs, GLA_HEADS, GLA_DK, GLA_DV))

    st = lambda name: jnp.stack(outs[name])
    return (xp.reshape(bp, seq, D_MODEL), xs.reshape(bs, 1, D_MODEL),
            kt_all.reshape(depth, bp, SB_HEADS, HEAD_DIM, seq).transpose(0, 1, 4, 2, 3),
            vt_all.reshape(depth, bp, SB_HEADS, HEAD_DIM, seq).transpose(0, 1, 4, 2, 3),
            st("ks"), st("vs"), st("hp"), st("hs"), st("cp"), st("cs"), st("sp"), st("ss"))
```

```python
import functools

import jax
import jax.numpy as jnp
from jax import lax
from jax.experimental import pallas as pl
from jax.experimental.pallas import tpu as pltpu

F32 = jnp.float32
BF16 = jnp.bfloat16

D_MODEL = 1024
HEAD_DIM = 64
SB_W = D_MODEL // 2
SB_HEADS = SB_W // HEAD_DIM
LRU_W = D_MODEL // 4
LRU_BLOCKS = LRU_W // HEAD_DIM
LRU_CONV = 4
LRU_C = 8.0
GLA_W = D_MODEL // 4
GLA_HEADS = GLA_W // HEAD_DIM
GLA_DV = HEAD_DIM
GLA_DK = HEAD_DIM // 2
GLA_QK = GLA_HEADS * GLA_DK
GLA_RANK = 16
GLA_TAU = 16.0
D_MIX = SB_W + LRU_W + GLA_W
D_FF = 2816
EPS = 1e-6
PAGE_SIZE = 128

LANES = 128
SUBLANES = 8
VMEM_LIMIT_BYTES = 56 * 1024 * 1024

_OFF_Q = 0
_OFF_K = _OFF_Q + SB_W
_OFF_V = _OFF_K + SB_W
_OFF_XL = _OFF_V + SB_W
_OFF_GATE = _OFF_XL + LRU_W
_OFF_QG = _OFF_GATE + LRU_W
_OFF_KG = _OFF_QG + GLA_QK
_OFF_VG = _OFF_KG + GLA_QK
_OFF_RG = _OFF_VG + GLA_W
_OFF_ALR = _OFF_RG + GLA_W
D_IN_PAD = _OFF_ALR + LANES

FF_CHUNK = 256
ROW_TILE = 512
SB_BLOCK = 256
GLA_CHUNK = 128
DECODE_PAGES_PER_STEP = 32


def _idiv(x, d):
    assert d & (d - 1) == 0
    return x >> (d.bit_length() - 1)


def _imod(x, d):
    assert d & (d - 1) == 0
    return x & (d - 1)


def _dot(a, b):
    return jnp.dot(a, b, preferred_element_type=F32)


def _dot_nt(a, b):
    return lax.dot_general(a, b, (((1,), (1,)), ((), ())), preferred_element_type=F32)


def _dot_tn(a, b):
    return lax.dot_general(a, b, (((0,), (0,)), ((), ())), preferred_element_type=F32)


def _split_dot(x, w, passes):
    hi = x.astype(BF16)
    acc = _dot(hi, w)
    rem = x - hi.astype(F32)
    for _ in range(passes - 1):
        lo = rem.astype(BF16)
        acc = acc + _dot(lo, w)
        rem = rem - lo.astype(F32)
    return acc


def _split_dot_left(w, x, passes):
    hi = x.astype(BF16)
    acc = _dot(w, hi)
    rem = x - hi.astype(F32)
    for _ in range(passes - 1):
        lo = rem.astype(BF16)
        acc = acc + _dot(w, lo)
        rem = rem - lo.astype(F32)
    return acc


def _rms(x, g):
    ms = jnp.mean(x * x, axis=-1, keepdims=True)
    return x * lax.rsqrt(ms + EPS) * g


def _softplus(z):
    return jnp.maximum(z, 0.0) + jnp.log1p(jnp.exp(-jnp.abs(z)))


def _log_sigmoid(z):
    return jnp.minimum(z, 0.0) - jnp.log1p(jnp.exp(-jnp.abs(z)))


def _gelu_tanh(x):
    return 0.5 * x * (1.0 + jnp.tanh(0.7978845608028654 * (x + 0.044715 * (x * x * x))))


def _head_rms_cols(x, e64, g):
    cols = []
    for c in range(x.shape[1] // LANES):
        xc = x[:, c * LANES:(c + 1) * LANES]
        ms = _split_dot(xc * xc, e64, 2)
        cols.append(xc * lax.rsqrt(ms + EPS) * g[:, c * LANES:(c + 1) * LANES])
    return cols


def _resident(shape):
    nd = len(shape)
    return pl.BlockSpec(shape, lambda *_: (0,) * nd, pipeline_mode=pl.Buffered(1))


def _params(semantics):
    return pltpu.CompilerParams(dimension_semantics=semantics, vmem_limit_bytes=VMEM_LIMIT_BYTES)


def _ffn_kernel(x_ref, g_ref, wg_ref, wu_ref, wd_ref, o_ref, h_ref, acc_ref):
    x = x_ref[...]
    h_ref[...] = _rms(x, g_ref[...]).astype(BF16)
    acc_ref[...] = jnp.zeros_like(acc_ref)

    def body(c, carry):
        cols = pl.ds(pl.multiple_of(c * FF_CHUNK, FF_CHUNK), FF_CHUNK)
        h = h_ref[...]
        gate = _dot(h, wg_ref[:, cols].astype(BF16))
        up = _dot(h, wu_ref[:, cols].astype(BF16))
        act = (gate * jax.nn.sigmoid(gate) * up).astype(BF16)
        acc_ref[...] += _dot(act, wd_ref[cols, :].astype(BF16))
        return carry

    lax.fori_loop(0, D_FF // FF_CHUNK, body, 0)
    o_ref[...] = x + 0.5 * acc_ref[...]


def _ffn(x, g, wg, wu, wd, tm):
    n = x.shape[0]
    return pl.pallas_call(
        _ffn_kernel,
        grid=(n // tm,),
        in_specs=[
            pl.BlockSpec((tm, D_MODEL), lambda i: (i, 0)),
            _resident((1, D_MODEL)),
            _resident((D_MODEL, D_FF)),
            _resident((D_MODEL, D_FF)),
            _resident((D_FF, D_MODEL)),
        ],
        out_specs=pl.BlockSpec((tm, D_MODEL), lambda i: (i, 0)),
        out_shape=jax.ShapeDtypeStruct((n, D_MODEL), F32),
        scratch_shapes=[pltpu.VMEM((tm, D_MODEL), BF16), pltpu.VMEM((tm, D_MODEL), F32)],
        compiler_params=_params(("parallel",)),
        name="ffn",
    )(x, g, wg, wu, wd)


def _inproj_kernel(*refs, transposed_kv):
    if transposed_kv:
        (x_ref, g_ref, w_ref, gq_ref, gk_ref, wal_ref, bal_ref, e64_ref, wkvt_ref,
         q_ref, k_ref, v_ref, xl_ref, gate_ref, qg_ref, kg_ref, vg_ref, rg_ref, la_ref, h_ref) = refs
    else:
        (x_ref, g_ref, w_ref, gq_ref, gk_ref, wal_ref, bal_ref, e64_ref,
         q_ref, k_ref, v_ref, xl_ref, gate_ref, qg_ref, kg_ref, vg_ref, rg_ref, la_ref, h_ref) = refs
    h_ref[...] = _rms(x_ref[...], g_ref[...]).astype(BF16)

    def proj(lo, width):
        return _dot(h_ref[...], w_ref[:, lo:lo + width])

    e64 = e64_ref[...]
    q_cols = _head_rms_cols(proj(_OFF_Q, SB_W), e64, gq_ref[...])
    for c, col in enumerate(q_cols):
        q_ref[:, c * LANES:(c + 1) * LANES] = col * (HEAD_DIM ** -0.5)
    if transposed_kv:
        rows = h_ref.shape[0]
        k_t = _dot_nt(wkvt_ref[:SB_W, :], h_ref[...]).reshape(SB_HEADS, HEAD_DIM, rows)
        ms = jnp.mean(k_t * k_t, axis=1, keepdims=True)
        k_ref[...] = (k_t * lax.rsqrt(ms + EPS)).reshape(SB_W, rows) * gk_ref[...]
        v_ref[...] = _dot_nt(wkvt_ref[SB_W:, :], h_ref[...])
    else:
        k_cols = _head_rms_cols(proj(_OFF_K, SB_W), e64, gk_ref[...])
        for c, col in enumerate(k_cols):
            k_ref[:, c * LANES:(c + 1) * LANES] = col
        v_ref[...] = proj(_OFF_V, SB_W)
    xl_ref[...] = proj(_OFF_XL, LRU_W)
    gate_ref[...] = _gelu_tanh(proj(_OFF_GATE, LRU_W))
    qg_ref[...] = proj(_OFF_QG, GLA_QK)
    kg_ref[...] = proj(_OFF_KG, GLA_QK)
    vg_ref[...] = proj(_OFF_VG, GLA_W)
    r = proj(_OFF_RG, GLA_W)
    rg_ref[...] = r * jax.nn.sigmoid(r)
    a_lr = proj(_OFF_ALR, LANES)
    xa = _dot(a_lr.astype(BF16), wal_ref[...]) + bal_ref[...]
    la_ref[...] = _log_sigmoid(xa) * (1.0 / GLA_TAU)


def _inproj(x, g, w_in, gq, gk, wal, bal, e64, tm, kv_slabs=None, layer=None, w_kvt=None, seq=None):
    n = x.shape[0]
    row = lambda w: pl.BlockSpec((tm, w), lambda i: (i, 0))
    in_specs = [
        row(D_MODEL),
        _resident((1, D_MODEL)),
        _resident((D_MODEL, D_IN_PAD)),
        _resident((1, SB_W)),
        _resident(gk.shape),
        _resident((LANES, LANES)),
        _resident((1, LANES)),
        _resident((LANES, LANES)),
    ]
    args = [x, g, w_in, gq, gk, wal, bal, e64]
    small = lambda w: jax.ShapeDtypeStruct((n, w), F32)
    transposed_kv = kv_slabs is not None
    if not transposed_kv:
        kv_specs = [row(SB_W), row(SB_W)]
        kv_shapes = [small(SB_W), small(SB_W)]
        aliases = {}
        kernel = functools.partial(_inproj_kernel, transposed_kv=False)
    else:
        k_all, v_all = kv_slabs
        tiles = seq // tm
        slab = pl.BlockSpec((None, None, SB_W, tm), lambda i: (layer, i // tiles, 0, i % tiles))
        kv_specs = [slab, slab]
        kv_shapes = [jax.ShapeDtypeStruct(k_all.shape, F32), jax.ShapeDtypeStruct(v_all.shape, F32)]
        in_specs += [_resident((2 * SB_W, D_MODEL)),
                     pl.BlockSpec(memory_space=pl.ANY), pl.BlockSpec(memory_space=pl.ANY)]
        args += [w_kvt, k_all, v_all]
        aliases = {len(args) - 2: 1, len(args) - 1: 2}

        def kernel(*refs):
            _inproj_kernel(*refs[:9], *refs[11:], transposed_kv=True)

    out_specs = [row(SB_W)] + kv_specs + [row(LRU_W), row(LRU_W), row(GLA_QK), row(GLA_QK),
                                          row(GLA_W), row(GLA_W), row(GLA_QK)]
    out_shape = [small(SB_W)] + kv_shapes + [small(LRU_W), small(LRU_W), small(GLA_QK), small(GLA_QK),
                                             small(GLA_W), small(GLA_W), small(GLA_QK)]
    return pl.pallas_call(
        kernel,
        grid=(n // tm,),
        in_specs=in_specs,
        out_specs=out_specs,
        out_shape=out_shape,
        input_output_aliases=aliases,
        scratch_shapes=[pltpu.VMEM((tm, D_MODEL), BF16)],
        compiler_params=_params(("parallel",)),
        name="inproj",
    )(*args)


def _merge_kernel(x_ref, osb_ref, olru_ref, ogla_ref, gate_ref, rg_ref, gmo_ref, e64_ref, wout_ref, o_ref):
    e64 = e64_ref[...]
    gmo = gmo_ref[...]
    sb = _head_rms_cols(osb_ref[...], e64, gmo[:, :SB_W])
    lru = _head_rms_cols(olru_ref[...], e64, gmo[:, SB_W:SB_W + LRU_W])
    gla = _head_rms_cols(ogla_ref[...], e64, gmo[:, SB_W + LRU_W:])
    gate = gate_ref[...]
    rg = rg_ref[...]
    lru = [col * gate[:, c * LANES:(c + 1) * LANES] for c, col in enumerate(lru)]
    gla = [col * rg[:, c * LANES:(c + 1) * LANES] for c, col in enumerate(gla)]
    acc = x_ref[...]
    for c, col in enumerate(sb + lru + gla):
        acc = acc + _dot(col.astype(BF16), wout_ref[c * LANES:(c + 1) * LANES, :])
    o_ref[...] = acc


def _merge(x, osb, olru, ogla, gate, rg, gmo, e64, wout, tm):
    n = x.shape[0]
    row = lambda w: pl.BlockSpec((tm, w), lambda i: (i, 0))
    return pl.pallas_call(
        _merge_kernel,
        grid=(n // tm,),
        in_specs=[row(D_MODEL), row(SB_W), row(LRU_W), row(GLA_W), row(LRU_W), row(GLA_W),
                  _resident((1, D_MIX)), _resident((LANES, LANES)), _resident((D_MIX, D_MODEL))],
        out_specs=row(D_MODEL),
        out_shape=jax.ShapeDtypeStruct((n, D_MODEL), F32),
        compiler_params=_params(("parallel",)),
        name="merge",
    )(x, osb, olru, ogla, gate, rg, gmo, e64, wout)


def _softplus_plain(z):
    return jnp.maximum(z, 0.0) + jnp.log(1.0 + jnp.exp(-jnp.abs(z)))


def _sb_tile_stages(q_h, kt_b, vt_h, bias, upper, carry, mask):
    st = {}

    def logits():
        st["z"] = _dot(q_h, kt_b) + bias

    def log_terms():
        z = st["z"]
        sp = _softplus_plain(z)
        l1m = -sp if mask is None else jnp.where(mask, -sp, 0.0)
        st["lb"] = z - sp
        st["edge"] = l1m[:, 0:1]
        hi = l1m.astype(BF16)
        st["hi"] = hi
        st["lo"] = (l1m - hi.astype(F32)).astype(BF16)

    def suffix_sums():
        st["within"] = _dot(st["hi"], upper) + _dot(st["lo"], upper)

    def weights():
        within = st["within"]
        a = jnp.exp(st["lb"] + within + carry)
        if mask is not None:
            a = jnp.where(mask, a, 0.0)
        st["a"] = a.astype(BF16)
        st["carry"] = carry + within[:, 0:1] + st["edge"]

    def values():
        return _dot_nt(st["a"], vt_h), st["carry"]

    return [logits, log_terms, suffix_sums, weights, values]


def _run_skewed(pipelines):
    depth = len(pipelines[0])
    results = [None] * len(pipelines)
    for step in range(len(pipelines) + depth - 1):
        for p, stages in enumerate(pipelines):
            s = step - p
            if 0 <= s < depth:
                results[p] = stages[s]()
    return results


def _sb_prompt_kernel(bias_ref, q_ref, kt_ref, vt_ref, o_ref, kb_ref, vh_ref, *, blk):
    pairs = SB_W // LANES
    i = pl.program_id(1)
    first = lax.broadcasted_iota(jnp.int32, (1, LANES), 1) < HEAD_DIM

    @pl.when(i == 0)
    def _stage():
        kb_ref[...] = kt_ref[...].astype(BF16)
        first_rows = lax.broadcasted_iota(jnp.int32, (LANES, vt_ref.shape[1]), 0) < HEAD_DIM
        for p in range(pairs):
            v = vt_ref[p * LANES:(p + 1) * LANES, :]
            vh_ref[2 * p] = jnp.where(first_rows, v, 0.0).astype(BF16)
            vh_ref[2 * p + 1] = jnp.where(first_rows, 0.0, v).astype(BF16)

    q_heads = []
    biases = []
    for p in range(pairs):
        q = q_ref[:, p * LANES:(p + 1) * LANES]
        q_heads += [jnp.where(first, q, 0.0).astype(BF16), jnp.where(first, 0.0, q).astype(BF16)]
        biases += [bias_ref[2 * p], bias_ref[2 * p + 1]]
    r_i = lax.broadcasted_iota(jnp.int32, (blk, blk), 0)
    c_i = lax.broadcasted_iota(jnp.int32, (blk, blk), 1)
    upper = jnp.where(r_i > c_i, 1.0, 0.0).astype(BF16)
    causal = c_i < r_i

    def tile(j, carries, accs, mask):
        cols = pl.ds(pl.multiple_of(j * blk, blk), blk)
        pipelines = []
        for h in range(2 * pairs):
            kt_b = kb_ref[(h // 2) * LANES:(h // 2 + 1) * LANES, cols]
            pipelines.append(_sb_tile_stages(q_heads[h], kt_b, vh_ref[h, :, cols], biases[h], upper,
                                             carries[h], mask))
        results = _run_skewed(pipelines)
        new_accs = list(accs)
        for h, (out, _) in enumerate(results):
            new_accs[h // 2] = new_accs[h // 2] + out
        return tuple(c for _, c in results), tuple(new_accs)

    zero_c = jnp.zeros((blk, 1), F32)
    zero_a = jnp.zeros((blk, LANES), F32)
    state = tile(i, (zero_c,) * (2 * pairs), (zero_a,) * pairs, causal)

    def body(it, state):
        return tile(i - 1 - it, state[0], state[1], None)

    _, accs = lax.fori_loop(0, i, body, state)
    for p in range(pairs):
        o_ref[:, p * LANES:(p + 1) * LANES] = accs[p]


def _sb_prompt(q, kt_all, vt_all, bias, layer):
    _, batch, _, seq = kt_all.shape
    blk = min(SB_BLOCK, seq)
    nq = seq // blk
    kv_spec = pl.BlockSpec((None, None, SB_W, seq), lambda b, i, *_: (layer, b, 0, 0))
    return pl.pallas_call(
        functools.partial(_sb_prompt_kernel, blk=blk),
        grid_spec=pltpu.PrefetchScalarGridSpec(
            num_scalar_prefetch=1,
            grid=(batch, nq),
            in_specs=[pl.BlockSpec((blk, SB_W), lambda b, i, *_: (b * nq + i, 0)), kv_spec, kv_spec],
            out_specs=pl.BlockSpec((blk, SB_W), lambda b, i, *_: (b * nq + i, 0)),
            scratch_shapes=[pltpu.VMEM((SB_W, seq), BF16), pltpu.VMEM((SB_HEADS, LANES, seq), BF16)],
        ),
        out_shape=jax.ShapeDtypeStruct((batch * seq, SB_W), F32),
        compiler_params=_params(("parallel", "arbitrary")),
        name="sb_prompt",
    )(bias, q, kt_all, vt_all)


def _sb_decode_kernel(pt_ref, q_ref, bias_ref, *refs, n_pages, group):
    k_refs = refs[:group]
    v_refs = refs[group:2 * group]
    o_ref = refs[2 * group]
    z_ref, a_ref, acc_ref = refs[2 * group + 1:]
    s = pl.program_id(1)
    steps = n_pages // group
    rows = n_pages * SB_HEADS

    @pl.when(s < steps)
    def _scores():
        q_col = q_ref[...]
        for g in range(group):
            prod = (k_refs[g][...] * q_col).reshape(SB_HEADS, HEAD_DIM, PAGE_SIZE)
            z = jnp.sum(prod, axis=1) + bias_ref[...]
            row0 = pl.multiple_of((s * group + g) * SB_HEADS, SB_HEADS)
            z_ref[pl.ds(row0, SB_HEADS), :] = z

    @pl.when(s == steps - 1)
    def _weights():
        z = z_ref[...]
        sp = _softplus(z)
        l1m = -sp
        r_i = lax.broadcasted_iota(jnp.int32, (PAGE_SIZE, PAGE_SIZE), 0)
        c_i = lax.broadcasted_iota(jnp.int32, (PAGE_SIZE, PAGE_SIZE), 1)
        upper = (r_i > c_i).astype(BF16)
        ones = jnp.ones((PAGE_SIZE, PAGE_SIZE), BF16)
        within = _split_dot(l1m, upper, 3)
        page_total = _split_dot(l1m, ones, 3)
        pr = lax.broadcasted_iota(jnp.int32, (rows, rows), 0)
        pc = lax.broadcasted_iota(jnp.int32, (rows, rows), 1)
        same_head = _imod(pc, SB_HEADS) == _imod(pr, SB_HEADS)
        later_page = jnp.where(_idiv(pc, SB_HEADS) > _idiv(pr, SB_HEADS),
                               jnp.where(same_head, 1.0, 0.0), 0.0).astype(BF16)
        later = _split_dot_left(later_page, page_total, 3)
        a_ref[...] = jnp.exp(z - sp + within + later)
        acc_ref[...] = jnp.zeros_like(acc_ref)

    @pl.when(s >= steps)
    def _values():
        acc = acc_ref[...]
        for g in range(group):
            row0 = pl.multiple_of(((s - steps) * group + g) * SB_HEADS, SB_HEADS)
            a_g = a_ref[pl.ds(row0, SB_HEADS), :]
            a_rows = jnp.concatenate(
                [jnp.broadcast_to(a_g[h:h + 1, :], (HEAD_DIM, PAGE_SIZE)) for h in range(SB_HEADS)], axis=0)
            acc = acc + v_refs[g][...] * a_rows
        acc_ref[...] = acc

    @pl.when(s == 2 * steps - 1)
    def _emit():
        o_ref[...] = jnp.sum(acc_ref[...], axis=1, keepdims=True)


def _sb_decode(q_col, bias_b, cache_kt, cache_vt, page_table, layer):
    b, n_pages = page_table.shape
    group = min(DECODE_PAGES_PER_STEP, n_pages)
    steps = n_pages // group

    def k_map(g):
        return lambda bi, s, pt: (layer, pt[bi, jnp.minimum(s, steps - 1) * group + g], 0, 0)

    def v_map(g):
        def index(bi, s, pt):
            in_values = s >= steps
            row = jnp.where(in_values, bi, jnp.maximum(bi - 1, 0))
            col = jnp.where(in_values, s - steps, steps - 1) * group + g
            return (layer, pt[row, col], 0, 0)
        return index

    page = lambda m: pl.BlockSpec((None, None, SB_W, PAGE_SIZE), m)
    in_specs = [pl.BlockSpec((None, SB_W, 1), lambda bi, s, pt: (bi, 0, 0)),
                pl.BlockSpec((SB_HEADS, PAGE_SIZE), lambda bi, s, pt: (0, 0))]
    in_specs += [page(k_map(g)) for g in range(group)]
    in_specs += [page(v_map(g)) for g in range(group)]
    rows = n_pages * SB_HEADS
    return pl.pallas_call(
        functools.partial(_sb_decode_kernel, n_pages=n_pages, group=group),
        grid_spec=pltpu.PrefetchScalarGridSpec(
            num_scalar_prefetch=1,
            grid=(b, 2 * steps),
            in_specs=in_specs,
            out_specs=pl.BlockSpec((None, SB_W, 1), lambda bi, s, pt: (bi, 0, 0)),
            scratch_shapes=[pltpu.VMEM((rows, PAGE_SIZE), F32),
                            pltpu.VMEM((rows, PAGE_SIZE), F32),
                            pltpu.VMEM((SB_W, PAGE_SIZE), F32)],
        ),
        out_shape=jax.ShapeDtypeStruct((b, SB_W, 1), F32),
        compiler_params=_params(("parallel", "arbitrary")),
        name="sb_decode",
    )(page_table, q_col, bias_b, *([cache_kt] * group), *([cache_vt] * group))


def _lru_gates(xc, wa_ref, ba_ref, wi_ref, bi_ref, lam_ref):
    xb = xc.astype(BF16)
    r = jax.nn.sigmoid(_dot(xb, wa_ref[...]) + ba_ref[...])
    i = jax.nn.sigmoid(_dot(xb, wi_ref[...]) + bi_ref[...])
    log_a = -LRU_C * r * _softplus(-lam_ref[...])
    a = jnp.exp(log_a)
    t = jnp.tanh(log_a)
    one_minus_a2 = -2.0 * t / (1.0 - t)
    return a, jnp.sqrt(one_minus_a2) * (i * xc)


def _lru_prompt_kernel(x_ref, cw_ref, cb_ref, wa_ref, ba_ref, wi_ref, bi_ref, lam_ref,
                       o_ref, hl_ref, cn_ref, *, seq):
    x = x_ref[...]
    rows = lax.broadcasted_iota(jnp.int32, (seq, LRU_W), 0)
    xc = cb_ref[...] + cw_ref[LRU_CONV - 1:LRU_CONV, :] * x
    for j in range(1, LRU_CONV):
        shifted = jnp.where(rows >= j, pltpu.roll(x, j, 0), 0.0)
        xc = xc + cw_ref[LRU_CONV - 1 - j:LRU_CONV - j, :] * shifted
    a, g = _lru_gates(xc, wa_ref, ba_ref, wi_ref, bi_ref, lam_ref)
    step = 1
    while step < seq:
        valid = rows >= step
        a_prev = pltpu.roll(a, step, 0)
        g_prev = pltpu.roll(g, step, 0)
        g = jnp.where(valid, a * g_prev + g, g)
        a = jnp.where(valid, a * a_prev, a)
        step *= 2
    o_ref[...] = g
    hl_ref[...] = g[seq - 1:seq, :]
    cn_ref[...] = x[seq - (LRU_CONV - 1):, :]


def _lru_prompt(xl, cw, cb, wa, ba, wi, bi, lam, batch, seq):
    vec = _resident((1, LRU_W))
    mat = _resident((LRU_W, LRU_W))
    return pl.pallas_call(
        functools.partial(_lru_prompt_kernel, seq=seq),
        grid=(batch,),
        in_specs=[pl.BlockSpec((seq, LRU_W), lambda b: (b, 0)), _resident((LRU_CONV, LRU_W)), vec,
                  mat, vec, mat, vec, vec],
        out_specs=[pl.BlockSpec((seq, LRU_W), lambda b: (b, 0)),
                   pl.BlockSpec((None, 1, LRU_W), lambda b: (b, 0, 0)),
                   pl.BlockSpec((None, LRU_CONV - 1, LRU_W), lambda b: (b, 0, 0))],
        out_shape=[jax.ShapeDtypeStruct((batch * seq, LRU_W), F32),
                   jax.ShapeDtypeStruct((batch, 1, LRU_W), F32),
                   jax.ShapeDtypeStruct((batch, LRU_CONV - 1, LRU_W), F32)],
        compiler_params=_params(("parallel",)),
        name="lru_prompt",
    )(xl, cw, cb, wa, ba, wi, bi, lam)


def _gla_prompt_kernel(q_ref, k_ref, v_ref, la_ref, o_ref, sfin_ref, s_ref, *, chunk, n_chunks):
    ci = pl.program_id(1)

    @pl.when(ci == 0)
    def _init():
        s_ref[...] = jnp.zeros_like(s_ref)

    qs = q_ref[...] * (GLA_DK ** -0.5)
    k = k_ref[...]
    v = v_ref[...]
    r_i = lax.broadcasted_iota(jnp.int32, (chunk, chunk), 0)
    c_i = lax.broadcasted_iota(jnp.int32, (chunk, chunk), 1)
    lower = (c_i <= r_i).astype(BF16)
    bc = _split_dot_left(lower, la_ref[...], 3)
    rows = lax.broadcasted_iota(jnp.int32, (chunk, GLA_QK), 0)
    qk_head = _idiv(lax.broadcasted_iota(jnp.int32, (1, GLA_QK), 1), GLA_DK)
    v_head = _idiv(lax.broadcasted_iota(jnp.int32, (1, GLA_W), 1), GLA_DV)
    own = (_idiv(lax.broadcasted_iota(jnp.int32, (GLA_QK, GLA_W), 0), GLA_DK)
           == _idiv(lax.broadcasted_iota(jnp.int32, (GLA_QK, GLA_W), 1), GLA_DV))

    state = s_ref[...]
    out = _dot((qs * jnp.exp(bc)).astype(BF16), state.astype(BF16))

    scores = [jnp.zeros((chunk, chunk), F32) for _ in range(GLA_HEADS)]
    level = 2 * SUBLANES
    while level <= chunk:
        half = level // 2
        mid = jnp.concatenate(
            [jnp.broadcast_to(bc[b * level + half - 1:b * level + half, :], (level, GLA_QK))
             for b in range(chunk // level)], axis=0)
        second = (rows & (level - 1)) >= half
        q_dec = jnp.where(second, qs * jnp.exp(bc - mid), 0.0)
        k_dec = jnp.where(second, 0.0, k * jnp.exp(mid - bc)).astype(BF16)
        same_block = _idiv(r_i, level) == _idiv(c_i, level)
        for h in range(GLA_HEADS):
            sc = _dot_nt(jnp.where(qk_head == h, q_dec, 0.0).astype(BF16), k_dec)
            if level < chunk:
                sc = jnp.where(same_block, sc, 0.0)
            scores[h] = scores[h] + sc
        level *= 2
    for h in range(GLA_HEADS):
        out = out + _dot(scores[h].astype(BF16), jnp.where(v_head == h, v, 0.0).astype(BF16))

    expand = jnp.where(own, 1.0, 0.0).astype(BF16)
    out = out + _split_dot(qs * k, expand, 2) * v
    for j in range(1, SUBLANES):
        ok = (rows & (SUBLANES - 1)) >= j
        p = jnp.where(ok, qs * pltpu.roll(k, j, 0) * jnp.exp(bc - pltpu.roll(bc, j, 0)), 0.0)
        out = out + _split_dot(p, expand, 2) * pltpu.roll(v, j, 0)
    o_ref[...] = out

    last = bc[chunk - 1:chunk, :]
    kv = _dot_tn((k * jnp.exp(last - bc)).astype(BF16), v.astype(BF16))
    dec_col = jnp.transpose(jnp.broadcast_to(jnp.exp(last), (GLA_QK, GLA_QK)))
    dec = jnp.concatenate([dec_col] * (GLA_W // GLA_QK), axis=1)
    new_state = dec * state + jnp.where(own, kv, 0.0)
    s_ref[...] = new_state

    @pl.when(ci == n_chunks - 1)
    def _emit():
        for h in range(GLA_HEADS):
            sfin_ref[h * GLA_DK:(h + 1) * GLA_DK, :] = (
                new_state[h * GLA_DK:(h + 1) * GLA_DK, h * GLA_DV:(h + 1) * GLA_DV])


def _gla_prompt(qg, kg, vg, la, batch, seq):
    chunk = min(GLA_CHUNK, seq)
    n_chunks = seq // chunk
    row = lambda w: pl.BlockSpec((chunk, w), lambda b, c: (b * n_chunks + c, 0))
    return pl.pallas_call(
        functools.partial(_gla_prompt_kernel, chunk=chunk, n_chunks=n_chunks),
        grid=(batch, n_chunks),
        in_specs=[row(GLA_QK), row(GLA_QK), row(GLA_W), row(GLA_QK)],
        out_specs=[row(GLA_W), pl.BlockSpec((None, GLA_QK, GLA_DV), lambda b, c: (b, 0, 0))],
        out_shape=[jax.ShapeDtypeStruct((batch * seq, GLA_W), F32),
                   jax.ShapeDtypeStruct((batch, GLA_QK, GLA_DV), F32)],
        scratch_shapes=[pltpu.VMEM((GLA_QK, GLA_W), F32)],
        compiler_params=_params(("parallel", "arbitrary")),
        name="gla_prompt",
    )(qg, kg, vg, la)


def _step_kernel(x_ref, conv_ref, h0_ref, cw_ref, cb_ref, wa_ref, ba_ref, wi_ref, bi_ref, lam_ref,
                 qc_ref, kc_ref, lac_ref, v_ref, s0_ref,
                 h_ref, cn_ref, og_ref, s_ref):
    x = x_ref[...]
    xc = cb_ref[...] + cw_ref[LRU_CONV - 1:LRU_CONV, :] * x
    for j in range(LRU_CONV - 1):
        xc = xc + cw_ref[j:j + 1, :] * conv_ref[j]
    a, g = _lru_gates(xc, wa_ref, ba_ref, wi_ref, bi_ref, lam_ref)
    h_ref[...] = a * h0_ref[...] + g
    for j in range(LRU_CONV - 2):
        cn_ref[j] = conv_ref[j + 1]
    cn_ref[LRU_CONV - 2] = x

    alpha = jnp.exp(lac_ref[...])
    kc = kc_ref[...]
    qc = qc_ref[...] * (GLA_DK ** -0.5)
    for h in range(GLA_HEADS):
        rs = slice(h * GLA_DK, (h + 1) * GLA_DK)
        new = alpha[:, rs, :] * s0_ref[:, rs, :] + kc[:, rs, :] * v_ref[:, h:h + 1, :]
        s_ref[:, rs, :] = new
        og_ref[:, h:h + 1, :] = jnp.sum(qc[:, rs, :] * new, axis=1, keepdims=True)


def _step(xl, conv0, h0, cw, cb, wa, ba, wi, bi, lam, q_col, k_col, la_col, v3, s0):
    b = xl.shape[0]
    return pl.pallas_call(
        _step_kernel,
        out_shape=[jax.ShapeDtypeStruct((b, LRU_W), F32),
                   jax.ShapeDtypeStruct((LRU_CONV - 1, b, LRU_W), F32),
                   jax.ShapeDtypeStruct((b, GLA_HEADS, GLA_DV), F32),
                   jax.ShapeDtypeStruct((b, GLA_QK, GLA_DV), F32)],
        compiler_params=pltpu.CompilerParams(vmem_limit_bytes=VMEM_LIMIT_BYTES),
        name="sample_step",
    )(xl, conv0, h0, cw, cb, wa, ba, wi, bi, lam, q_col, k_col, la_col, v3, s0)


def _block_diag(w):
    nb, bi, bj = w.shape
    eye = jnp.eye(nb, dtype=w.dtype)
    return (eye[:, None, :, None] * w[:, :, None, :]).reshape(nb * bi, nb * bj)


def _row_tile(n):
    return min(ROW_TILE, n)


def kernel(x_prompt, x_sample, cache_k, cache_v, page_table, state_lru_h, state_lru_conv, state_gla,
           g_ffn1, w_ffn1_gate, w_ffn1_up, w_ffn1_down, g_mix, w_in, g_qnorm, g_knorm, sb_bias,
           conv_w, conv_b, lru_wa, lru_ba, lru_wi, lru_bi, lru_lambda, gla_w_alpha, gla_b_alpha,
           g_mix_out, w_out, g_ffn2, w_ffn2_gate, w_ffn2_up, w_ffn2_down):
    depth = w_in.shape[0]
    bp, seq, _ = x_prompt.shape
    bs, dec_seq, _ = x_sample.shape
    assert dec_seq == 1
    n_p = bp * seq
    n_phys = cache_k.shape[1]

    w_in_p = jnp.pad(w_in, ((0, 0), (0, 0), (0, D_IN_PAD - w_in.shape[2]))).astype(BF16)
    w_kvt = w_in[:, :, _OFF_K:_OFF_XL].transpose(0, 2, 1).astype(BF16)
    w_al = jnp.pad(gla_w_alpha, ((0, 0), (0, LANES - GLA_RANK), (0, 0))).astype(BF16)
    w_out_b = w_out.astype(BF16)
    wa_bd = jax.vmap(_block_diag)(lru_wa).astype(BF16)
    wi_bd = jax.vmap(_block_diag)(lru_wi).astype(BF16)
    gq = jnp.tile(g_qnorm, (1, SB_HEADS))[:, None, :]
    gk = jnp.tile(g_knorm, (1, SB_HEADS))[:, None, :]
    gk_col = jnp.tile(g_knorm, (1, SB_HEADS))[:, :, None]
    lane = jnp.arange(LANES)
    e64 = ((lane[:, None] // HEAD_DIM == lane[None, :] // HEAD_DIM).astype(F32) / HEAD_DIM).astype(BF16)
    vec = lambda t, l: t[l][None, :]

    cache_kt = cache_k.transpose(0, 1, 3, 4, 2).reshape(depth, n_phys, SB_W, PAGE_SIZE)
    cache_vt = cache_v.transpose(0, 1, 3, 4, 2).reshape(depth, n_phys, SB_W, PAGE_SIZE)

    xp = x_prompt.reshape(n_p, D_MODEL)
    xs = x_sample.reshape(bs, D_MODEL)
    tm_p = _row_tile(n_p)
    tm_s = _row_tile(bs)
    kt_all = jnp.zeros((depth, bp, SB_W, seq), F32)
    vt_all = jnp.zeros((depth, bp, SB_W, seq), F32)
    outs = {name: [] for name in ("ks", "vs", "hp", "hs", "cp", "cs", "sp", "ss")}

    for l in range(depth):
        lru_w = (conv_w[l], vec(conv_b, l), wa_bd[l], vec(lru_ba, l), wi_bd[l], vec(lru_bi, l),
                 vec(lru_lambda, l))
        inproj_w = (vec(g_mix, l), w_in_p[l], gq[l])
        inproj_w2 = (w_al[l], vec(gla_b_alpha, l), e64)

        ffn1_w = (vec(g_ffn1, l), w_ffn1_gate[l], w_ffn1_up[l], w_ffn1_down[l])
        ffn2_w = (vec(g_ffn2, l), w_ffn2_gate[l], w_ffn2_up[l], w_ffn2_down[l])
        xp = _ffn(xp, *ffn1_w, tm_p)
        q, kt_all, vt_all, xl, gate, qg, kg, vg, rg, la = _inproj(
            xp, *inproj_w, gk_col[l], *inproj_w2, tm_p, (kt_all, vt_all), l, w_kvt[l], seq)
        osb = _sb_prompt(q, kt_all, vt_all, sb_bias[l], l)
        olru, h_last, conv_new = _lru_prompt(xl, *lru_w, bp, seq)
        ogla, s_fin = _gla_prompt(qg, kg, vg, la, bp, seq)
        xp = _merge(xp, osb, olru, ogla, gate, rg, vec(g_mix_out, l), e64, w_out_b[l], tm_p)
        xp = _ffn(xp, *ffn2_w, tm_p)
        outs["hp"].append(h_last.reshape(bp, LRU_W))
        outs["cp"].append(conv_new)
        outs["sp"].append(s_fin.reshape(bp, GLA_HEADS, GLA_DK, GLA_DV))

        xs = _ffn(xs, *ffn1_w, tm_s)
        q, k_new, v_new, xl, gate, qg, kg, vg, rg, la = _inproj(xs, *inproj_w, gk[l], *inproj_w2, tm_s)
        bias_b = jnp.broadcast_to(sb_bias[l][:, None], (SB_HEADS, PAGE_SIZE))
        osb = _sb_decode(q.reshape(bs, SB_W, 1), bias_b, cache_kt, cache_vt, page_table, l)
        h_new, conv_new, ogla, s_new = _step(
            xl, state_lru_conv[l].transpose(1, 0, 2), state_lru_h[l], *lru_w,
            qg.reshape(bs, GLA_QK, 1), kg.reshape(bs, GLA_QK, 1), la.reshape(bs, GLA_QK, 1),
            vg.reshape(bs, GLA_HEADS, GLA_DV), state_gla[l].reshape(bs, GLA_QK, GLA_DV))
        xs = _merge(xs, osb.reshape(bs, SB_W), h_new, ogla.reshape(bs, GLA_W), gate, rg,
                    vec(g_mix_out, l), e64, w_out_b[l], tm_s)
        xs = _ffn(xs, *ffn2_w, tm_s)
        outs["ks"].append(k_new.reshape(bs, 1, SB_HEADS, HEAD_DIM))
        outs["vs"].append(v_new.reshape(bs, 1, SB_HEADS, HEAD_DIM))
        outs["hs"].append(h_new)
        outs["cs"].append(conv_new.transpose(1, 0, 2))
        outs["ss"].append(s_new.reshape(bs, GLA_HEADS, GLA_DK, GLA_DV))

    st = lambda name: jnp.stack(outs[name])
    return (xp.reshape(bp, seq, D_MODEL), xs.reshape(bs, 1, D_MODEL),
            kt_all.reshape(depth, bp, SB_HEADS, HEAD_DIM, seq).transpose(0, 1, 4, 2, 3),
            vt_all.reshape(depth, bp, SB_HEADS, HEAD_DIM, seq).transpose(0, 1, 4, 2, 3),
            st("ks"), st("vs"), st("hp"), st("hs"), st("cp"), st("cs"), st("sp"), st("ss"))
```

```python
import functools

import jax
import jax.numpy as jnp
from jax import lax
from jax.experimental import pallas as pl
from jax.experimental.pallas import tpu as pltpu

F32 = jnp.float32
BF16 = jnp.bfloat16

D_MODEL = 1024
HEAD_DIM = 64
SB_W = D_MODEL // 2
SB_HEADS = SB_W // HEAD_DIM
LRU_W = D_MODEL // 4
LRU_BLOCKS = LRU_W // HEAD_DIM
LRU_CONV = 4
LRU_C = 8.0
GLA_W = D_MODEL // 4
GLA_HEADS = GLA_W // HEAD_DIM
GLA_DV = HEAD_DIM
GLA_DK = HEAD_DIM // 2
GLA_QK = GLA_HEADS * GLA_DK
GLA_RANK = 16
GLA_TAU = 16.0
D_MIX = SB_W + LRU_W + GLA_W
D_FF = 2816
EPS = 1e-6
PAGE_SIZE = 128

LANES = 128
SUBLANES = 8
VMEM_LIMIT_BYTES = 56 * 1024 * 1024

_OFF_Q = 0
_OFF_K = _OFF_Q + SB_W
_OFF_V = _OFF_K + SB_W
_OFF_XL = _OFF_V + SB_W
_OFF_GATE = _OFF_XL + LRU_W
_OFF_QG = _OFF_GATE + LRU_W
_OFF_KG = _OFF_QG + GLA_QK
_OFF_VG = _OFF_KG + GLA_QK
_OFF_RG = _OFF_VG + GLA_W
_OFF_ALR = _OFF_RG + GLA_W
D_IN_PAD = _OFF_ALR + LANES

FF_CHUNK = 256
ROW_TILE = 512
SB_BLOCK = 256
GLA_CHUNK = 128
DECODE_PAGES_PER_STEP = 32


def _idiv(x, d):
    assert d & (d - 1) == 0
    return x >> (d.bit_length() - 1)


def _imod(x, d):
    assert d & (d - 1) == 0
    return x & (d - 1)


def _dot(a, b):
    return jnp.dot(a, b, preferred_element_type=F32)


def _dot_nt(a, b):
    return lax.dot_general(a, b, (((1,), (1,)), ((), ())), preferred_element_type=F32)


def _dot_tn(a, b):
    return lax.dot_general(a, b, (((0,), (0,)), ((), ())), preferred_element_type=F32)


def _split_dot(x, w, passes):
    hi = x.astype(BF16)
    acc = _dot(hi, w)
    rem = x - hi.astype(F32)
    for _ in range(passes - 1):
        lo = rem.astype(BF16)
        acc = acc + _dot(lo, w)
        rem = rem - lo.astype(F32)
    return acc


def _split_dot_left(w, x, passes):
    hi = x.astype(BF16)
    acc = _dot(w, hi)
    rem = x - hi.astype(F32)
    for _ in range(passes - 1):
        lo = rem.astype(BF16)
        acc = acc + _dot(w, lo)
        rem = rem - lo.astype(F32)
    return acc


def _rms(x, g):
    ms = jnp.mean(x * x, axis=-1, keepdims=True)
    return x * lax.rsqrt(ms + EPS) * g


def _softplus(z):
    return jnp.maximum(z, 0.0) + jnp.log1p(jnp.exp(-jnp.abs(z)))


def _log_sigmoid(z):
    return jnp.minimum(z, 0.0) - jnp.log1p(jnp.exp(-jnp.abs(z)))


def _gelu_tanh(x):
    return 0.5 * x * (1.0 + jnp.tanh(0.7978845608028654 * (x + 0.044715 * (x * x * x))))


def _head_rms_cols(x, e64, g):
    cols = []
    for c in range(x.shape[1] // LANES):
        xc = x[:, c * LANES:(c + 1) * LANES]
        ms = _split_dot(xc * xc, e64, 2)
        cols.append(xc * lax.rsqrt(ms + EPS) * g[:, c * LANES:(c + 1) * LANES])
    return cols


def _resident(shape):
    nd = len(shape)
    return pl.BlockSpec(shape, lambda *_: (0,) * nd, pipeline_mode=pl.Buffered(1))


def _layer_resident(shape, layer):
    nd = len(shape)
    return pl.BlockSpec((None,) + tuple(shape), lambda *_: (layer,) + (0,) * nd, pipeline_mode=pl.Buffered(1))


def _params(semantics):
    return pltpu.CompilerParams(dimension_semantics=semantics, vmem_limit_bytes=VMEM_LIMIT_BYTES)


def _ffn_kernel(x_ref, g_ref, wg_ref, wu_ref, wd_ref, o_ref, h_ref, acc_ref):
    x = x_ref[...]
    h_ref[...] = _rms(x, g_ref[...]).astype(BF16)
    acc_ref[...] = jnp.zeros_like(acc_ref)

    def body(c, carry):
        cols = pl.ds(pl.multiple_of(c * FF_CHUNK, FF_CHUNK), FF_CHUNK)
        h = h_ref[...]
        gate = _dot(h, wg_ref[:, cols].astype(BF16))
        up = _dot(h, wu_ref[:, cols].astype(BF16))
        act = (gate * jax.nn.sigmoid(gate) * up).astype(BF16)
        acc_ref[...] += _dot(act, wd_ref[cols, :].astype(BF16))
        return carry

    lax.fori_loop(0, D_FF // FF_CHUNK, body, 0, unroll=True)
    o_ref[...] = x + 0.5 * acc_ref[...]


def _ffn(x, g, wg, wu, wd, layer, tm):
    n = x.shape[0]
    return pl.pallas_call(
        _ffn_kernel,
        grid=(n // tm,),
        in_specs=[
            pl.BlockSpec((tm, D_MODEL), lambda i: (i, 0)),
            _resident((1, D_MODEL)),
            _layer_resident((D_MODEL, D_FF), layer),
            _layer_resident((D_MODEL, D_FF), layer),
            _layer_resident((D_FF, D_MODEL), layer),
        ],
        out_specs=pl.BlockSpec((tm, D_MODEL), lambda i: (i, 0)),
        out_shape=jax.ShapeDtypeStruct((n, D_MODEL), F32),
        scratch_shapes=[pltpu.VMEM((tm, D_MODEL), BF16), pltpu.VMEM((tm, D_MODEL), F32)],
        compiler_params=_params(("parallel",)),
        name="ffn",
    )(x, g, wg, wu, wd)


def _inproj_kernel(*refs, transposed_kv):
    if transposed_kv:
        (x_ref, g_ref, w_ref, gq_ref, gk_ref, wal_ref, bal_ref, e64_ref, wkvt_ref,
         q_ref, k_ref, v_ref, xl_ref, gate_ref, qg_ref, kg_ref, vg_ref, rg_ref, la_ref, h_ref) = refs
    else:
        (x_ref, g_ref, w_ref, gq_ref, gk_ref, wal_ref, bal_ref, e64_ref,
         q_ref, k_ref, v_ref, xl_ref, gate_ref, qg_ref, kg_ref, vg_ref, rg_ref, la_ref, h_ref) = refs
    h_ref[...] = _rms(x_ref[...], g_ref[...]).astype(BF16)

    def proj(lo, width):
        return _dot(h_ref[...], w_ref[:, lo:lo + width])

    e64 = e64_ref[...]
    q_cols = _head_rms_cols(proj(_OFF_Q, SB_W), e64, gq_ref[...])
    for c, col in enumerate(q_cols):
        q_ref[:, c * LANES:(c + 1) * LANES] = col * (HEAD_DIM ** -0.5)
    if transposed_kv:
        rows = h_ref.shape[0]
        k_t = _dot_nt(wkvt_ref[:SB_W, :], h_ref[...]).reshape(SB_HEADS, HEAD_DIM, rows)
        ms = jnp.mean(k_t * k_t, axis=1, keepdims=True)
        k_ref[...] = (k_t * lax.rsqrt(ms + EPS)).reshape(SB_W, rows) * gk_ref[...]
        v_ref[...] = _dot_nt(wkvt_ref[SB_W:, :], h_ref[...])
    else:
        k_cols = _head_rms_cols(proj(_OFF_K, SB_W), e64, gk_ref[...])
        for c, col in enumerate(k_cols):
            k_ref[:, c * LANES:(c + 1) * LANES] = col
        v_ref[...] = proj(_OFF_V, SB_W)
    xl_ref[...] = proj(_OFF_XL, LRU_W)
    gate_ref[...] = _gelu_tanh(proj(_OFF_GATE, LRU_W))
    qg_ref[...] = proj(_OFF_QG, GLA_QK)
    kg_ref[...] = proj(_OFF_KG, GLA_QK)
    vg_ref[...] = proj(_OFF_VG, GLA_W)
    r = proj(_OFF_RG, GLA_W)
    rg_ref[...] = r * jax.nn.sigmoid(r)
    a_lr = proj(_OFF_ALR, LANES)
    xa = _dot(a_lr.astype(BF16), wal_ref[...]) + bal_ref[...]
    la_ref[...] = _log_sigmoid(xa) * (1.0 / GLA_TAU)


def _inproj(x, g, w_in, gq, gk, wal, bal, e64, tm, kv_slabs=None, layer=None, w_kvt=None, seq=None):
    n = x.shape[0]
    row = lambda w: pl.BlockSpec((tm, w), lambda i: (i, 0))
    in_specs = [
        row(D_MODEL),
        _resident((1, D_MODEL)),
        _resident((D_MODEL, D_IN_PAD)),
        _resident((1, SB_W)),
        _resident(gk.shape),
        _resident((LANES, LANES)),
        _resident((1, LANES)),
        _resident((LANES, LANES)),
    ]
    args = [x, g, w_in, gq, gk, wal, bal, e64]
    small = lambda w: jax.ShapeDtypeStruct((n, w), F32)
    transposed_kv = kv_slabs is not None
    if not transposed_kv:
        kv_specs = [row(SB_W), row(SB_W)]
        kv_shapes = [small(SB_W), small(SB_W)]
        aliases = {}
        kernel = functools.partial(_inproj_kernel, transposed_kv=False)
    else:
        k_all, v_all = kv_slabs
        tiles = seq // tm
        slab = pl.BlockSpec((None, None, SB_W, tm), lambda i: (layer, i // tiles, 0, i % tiles))
        kv_specs = [slab, slab]
        kv_shapes = [jax.ShapeDtypeStruct(k_all.shape, F32), jax.ShapeDtypeStruct(v_all.shape, F32)]
        in_specs += [_resident((2 * SB_W, D_MODEL)),
                     pl.BlockSpec(memory_space=pl.ANY), pl.BlockSpec(memory_space=pl.ANY)]
        args += [w_kvt, k_all, v_all]
        aliases = {len(args) - 2: 1, len(args) - 1: 2}

        def kernel(*refs):
            _inproj_kernel(*refs[:9], *refs[11:], transposed_kv=True)

    out_specs = [row(SB_W)] + kv_specs + [row(LRU_W), row(LRU_W), row(GLA_QK), row(GLA_QK),
                                          row(GLA_W), row(GLA_W), row(GLA_QK)]
    out_shape = [small(SB_W)] + kv_shapes + [small(LRU_W), small(LRU_W), small(GLA_QK), small(GLA_QK),
                                             small(GLA_W), small(GLA_W), small(GLA_QK)]
    return pl.pallas_call(
        kernel,
        grid=(n // tm,),
        in_specs=in_specs,
        out_specs=out_specs,
        out_shape=out_shape,
        input_output_aliases=aliases,
        scratch_shapes=[pltpu.VMEM((tm, D_MODEL), BF16)],
        compiler_params=_params(("parallel",)),
        name="inproj",
    )(*args)


def _merge_kernel(x_ref, osb_ref, olru_ref, ogla_ref, gate_ref, rg_ref, gmo_ref, e64_ref, wout_ref, o_ref):
    e64 = e64_ref[...]
    gmo = gmo_ref[...]
    sb = _head_rms_cols(osb_ref[...], e64, gmo[:, :SB_W])
    lru = _head_rms_cols(olru_ref[...], e64, gmo[:, SB_W:SB_W + LRU_W])
    gla = _head_rms_cols(ogla_ref[...], e64, gmo[:, SB_W + LRU_W:])
    gate = gate_ref[...]
    rg = rg_ref[...]
    lru = [col * gate[:, c * LANES:(c + 1) * LANES] for c, col in enumerate(lru)]
    gla = [col * rg[:, c * LANES:(c + 1) * LANES] for c, col in enumerate(gla)]
    acc = x_ref[...]
    for c, col in enumerate(sb + lru + gla):
        acc = acc + _dot(col.astype(BF16), wout_ref[c * LANES:(c + 1) * LANES, :])
    o_ref[...] = acc


def _merge(x, osb, olru, ogla, gate, rg, gmo, e64, wout, tm):
    n = x.shape[0]
    row = lambda w: pl.BlockSpec((tm, w), lambda i: (i, 0))
    return pl.pallas_call(
        _merge_kernel,
        grid=(n // tm,),
        in_specs=[row(D_MODEL), row(SB_W), row(LRU_W), row(GLA_W), row(LRU_W), row(GLA_W),
                  _resident((1, D_MIX)), _resident((LANES, LANES)), _resident((D_MIX, D_MODEL))],
        out_specs=row(D_MODEL),
        out_shape=jax.ShapeDtypeStruct((n, D_MODEL), F32),
        compiler_params=_params(("parallel",)),
        name="merge",
    )(x, osb, olru, ogla, gate, rg, gmo, e64, wout)


def _softplus_plain(z):
    return jnp.maximum(z, 0.0) + jnp.log(1.0 + jnp.exp(-jnp.abs(z)))


def _sb_tile_stages(q_h, kt_b, vt_h, bias, upper, carry, mask):
    st = {}

    def logits():
        st["z"] = _dot(q_h, kt_b) + bias

    def log_terms():
        z = st["z"]
        sp = _softplus_plain(z)
        l1m = -sp if mask is None else jnp.where(mask, -sp, 0.0)
        st["lb"] = z - sp
        st["edge"] = l1m[:, 0:1]
        hi = l1m.astype(BF16)
        st["hi"] = hi

    def suffix_sums():
        st["within"] = _dot(st["hi"], upper)

    def weights():
        within = st["within"]
        a = jnp.exp(st["lb"] + within + carry)
        if mask is not None:
            a = jnp.where(mask, a, 0.0)
        st["a"] = a.astype(BF16)
        st["carry"] = carry + within[:, 0:1] + st["edge"]

    def values():
        return _dot_nt(st["a"], vt_h), st["carry"]

    return [logits, log_terms, suffix_sums, weights, values]


def _run_skewed(pipelines):
    depth = len(pipelines[0])
    results = [None] * len(pipelines)
    for step in range(len(pipelines) + depth - 1):
        for p, stages in enumerate(pipelines):
            s = step - p
            if 0 <= s < depth:
                results[p] = stages[s]()
    return results


def _sb_prompt_kernel(bias_ref, q_ref, kt_ref, vt_ref, o_ref, kb_ref, vh_ref, *, blk):
    pairs = SB_W // LANES
    i = pl.program_id(1)
    first = lax.broadcasted_iota(jnp.int32, (1, LANES), 1) < HEAD_DIM

    @pl.when(i == 0)
    def _stage():
        kb_ref[...] = kt_ref[...].astype(BF16)
        first_rows = lax.broadcasted_iota(jnp.int32, (LANES, vt_ref.shape[1]), 0) < HEAD_DIM
        for p in range(pairs):
            v = vt_ref[p * LANES:(p + 1) * LANES, :]
            vh_ref[2 * p] = jnp.where(first_rows, v, 0.0).astype(BF16)
            vh_ref[2 * p + 1] = jnp.where(first_rows, 0.0, v).astype(BF16)

    q_heads = []
    biases = []
    for p in range(pairs):
        q = q_ref[:, p * LANES:(p + 1) * LANES]
        q_heads += [jnp.where(first, q, 0.0).astype(BF16), jnp.where(first, 0.0, q).astype(BF16)]
        biases += [bias_ref[2 * p], bias_ref[2 * p + 1]]
    r_i = lax.broadcasted_iota(jnp.int32, (blk, blk), 0)
    c_i = lax.broadcasted_iota(jnp.int32, (blk, blk), 1)
    upper = jnp.where(r_i > c_i, 1.0, 0.0).astype(BF16)
    causal = c_i < r_i

    def tile(j, carries, accs, mask):
        cols = pl.ds(pl.multiple_of(j * blk, blk), blk)
        pipelines = []
        for h in range(2 * pairs):
            kt_b = kb_ref[(h // 2) * LANES:(h // 2 + 1) * LANES, cols]
            pipelines.append(_sb_tile_stages(q_heads[h], kt_b, vh_ref[h, :, cols], biases[h], upper,
                                             carries[h], mask))
        results = _run_skewed(pipelines)
        new_accs = list(accs)
        for h, (out, _) in enumerate(results):
            new_accs[h // 2] = new_accs[h // 2] + out
        return tuple(c for _, c in results), tuple(new_accs)

    zero_c = jnp.zeros((blk, 1), F32)
    zero_a = jnp.zeros((blk, LANES), F32)
    state = tile(i, (zero_c,) * (2 * pairs), (zero_a,) * pairs, causal)

    def body(it, state):
        return tile(i - 1 - it, state[0], state[1], None)

    _, accs = lax.fori_loop(0, i, body, state)
    for p in range(pairs):
        o_ref[:, p * LANES:(p + 1) * LANES] = accs[p]


def _sb_prompt(q, kt_all, vt_all, bias, layer):
    _, batch, _, seq = kt_all.shape
    blk = min(SB_BLOCK, seq)
    nq = seq // blk
    kv_spec = pl.BlockSpec((None, None, SB_W, seq), lambda b, i, *_: (layer, b, 0, 0))
    return pl.pallas_call(
        functools.partial(_sb_prompt_kernel, blk=blk),
        grid_spec=pltpu.PrefetchScalarGridSpec(
            num_scalar_prefetch=1,
            grid=(batch, nq),
            in_specs=[pl.BlockSpec((blk, SB_W), lambda b, i, *_: (b * nq + i, 0)), kv_spec, kv_spec],
            out_specs=pl.BlockSpec((blk, SB_W), lambda b, i, *_: (b * nq + i, 0)),
            scratch_shapes=[pltpu.VMEM((SB_W, seq), BF16), pltpu.VMEM((SB_HEADS, LANES, seq), BF16)],
        ),
        out_shape=jax.ShapeDtypeStruct((batch * seq, SB_W), F32),
        compiler_params=_params(("parallel", "arbitrary")),
        name="sb_prompt",
    )(bias, q, kt_all, vt_all)


def _sb_decode_kernel(pt_ref, q_ref, bias_ref, *refs, n_pages, group):
    k_refs = refs[:group]
    v_refs = refs[group:2 * group]
    o_ref = refs[2 * group]
    z_ref, a_ref, acc_ref = refs[2 * group + 1:]
    s = pl.program_id(1)
    steps = n_pages // group
    rows = n_pages * SB_HEADS

    @pl.when(s < steps)
    def _scores():
        q_col = q_ref[...]
        for g in range(group):
            prod = (k_refs[g][...] * q_col).reshape(SB_HEADS, HEAD_DIM, PAGE_SIZE)
            z = jnp.sum(prod, axis=1) + bias_ref[...]
            row0 = pl.multiple_of((s * group + g) * SB_HEADS, SB_HEADS)
            z_ref[pl.ds(row0, SB_HEADS), :] = z

    @pl.when(s == steps - 1)
    def _weights():
        z = z_ref[...]
        sp = _softplus(z)
        l1m = -sp
        r_i = lax.broadcasted_iota(jnp.int32, (PAGE_SIZE, PAGE_SIZE), 0)
        c_i = lax.broadcasted_iota(jnp.int32, (PAGE_SIZE, PAGE_SIZE), 1)
        upper = (r_i > c_i).astype(BF16)
        ones = jnp.ones((PAGE_SIZE, PAGE_SIZE), BF16)
        within = _split_dot(l1m, upper, 3)
        page_total = _split_dot(l1m, ones, 3)
        pr = lax.broadcasted_iota(jnp.int32, (rows, rows), 0)
        pc = lax.broadcasted_iota(jnp.int32, (rows, rows), 1)
        same_head = _imod(pc, SB_HEADS) == _imod(pr, SB_HEADS)
        later_page = jnp.where(_idiv(pc, SB_HEADS) > _idiv(pr, SB_HEADS),
                               jnp.where(same_head, 1.0, 0.0), 0.0).astype(BF16)
        later = _split_dot_left(later_page, page_total, 3)
        a_ref[...] = jnp.exp(z - sp + within + later)
        acc_ref[...] = jnp.zeros_like(acc_ref)

    @pl.when(s >= steps)
    def _values():
        acc = acc_ref[...]
        for g in range(group):
            row0 = pl.multiple_of(((s - steps) * group + g) * SB_HEADS, SB_HEADS)
            a_g = a_ref[pl.ds(row0, SB_HEADS), :]
            a_rows = jnp.concatenate(
                [jnp.broadcast_to(a_g[h:h + 1, :], (HEAD_DIM, PAGE_SIZE)) for h in range(SB_HEADS)], axis=0)
            acc = acc + v_refs[g][...] * a_rows
        acc_ref[...] = acc

    @pl.when(s == 2 * steps - 1)
    def _emit():
        o_ref[...] = jnp.sum(acc_ref[...], axis=1, keepdims=True)


def _sb_decode(q_col, bias_b, cache_kt, cache_vt, page_table, layer):
    b, n_pages = page_table.shape
    group = min(DECODE_PAGES_PER_STEP, n_pages)
    steps = n_pages // group

    def k_map(g):
        return lambda bi, s, pt: (layer, pt[bi, jnp.minimum(s, steps - 1) * group + g], 0, 0)

    def v_map(g):
        def index(bi, s, pt):
            in_values = s >= steps
            row = jnp.where(in_values, bi, jnp.maximum(bi - 1, 0))
            col = jnp.where(in_values, s - steps, steps - 1) * group + g
            return (layer, pt[row, col], 0, 0)
        return index

    page = lambda m: pl.BlockSpec((None, None, SB_W, PAGE_SIZE), m)
    in_specs = [pl.BlockSpec((None, SB_W, 1), lambda bi, s, pt: (bi, 0, 0)),
                pl.BlockSpec((SB_HEADS, PAGE_SIZE), lambda bi, s, pt: (0, 0))]
    in_specs += [page(k_map(g)) for g in range(group)]
    in_specs += [page(v_map(g)) for g in range(group)]
    rows = n_pages * SB_HEADS
    return pl.pallas_call(
        functools.partial(_sb_decode_kernel, n_pages=n_pages, group=group),
        grid_spec=pltpu.PrefetchScalarGridSpec(
            num_scalar_prefetch=1,
            grid=(b, 2 * steps),
            in_specs=in_specs,
            out_specs=pl.BlockSpec((None, SB_W, 1), lambda bi, s, pt: (bi, 0, 0)),
            scratch_shapes=[pltpu.VMEM((rows, PAGE_SIZE), F32),
                            pltpu.VMEM((rows, PAGE_SIZE), F32),
                            pltpu.VMEM((SB_W, PAGE_SIZE), F32)],
        ),
        out_shape=jax.ShapeDtypeStruct((b, SB_W, 1), F32),
        compiler_params=_params(("parallel", "arbitrary")),
        name="sb_decode",
    )(page_table, q_col, bias_b, *([cache_kt] * group), *([cache_vt] * group))


def _lru_gates(xc, wa_ref, ba_ref, wi_ref, bi_ref, lam_ref):
    xb = xc.astype(BF16)
    r = jax.nn.sigmoid(_dot(xb, wa_ref[...]) + ba_ref[...])
    i = jax.nn.sigmoid(_dot(xb, wi_ref[...]) + bi_ref[...])
    log_a = -LRU_C * r * _softplus(-lam_ref[...])
    a = jnp.exp(log_a)
    t = jnp.tanh(log_a)
    one_minus_a2 = -2.0 * t / (1.0 - t)
    return a, jnp.sqrt(one_minus_a2) * (i * xc)


def _lru_prompt_kernel(x_ref, cw_ref, cb_ref, wa_ref, ba_ref, wi_ref, bi_ref, lam_ref,
                       o_ref, hl_ref, cn_ref, *, seq):
    x = x_ref[...]
    rows = lax.broadcasted_iota(jnp.int32, (seq, LRU_W), 0)
    xc = cb_ref[...] + cw_ref[LRU_CONV - 1:LRU_CONV, :] * x
    for j in range(1, LRU_CONV):
        shifted = jnp.where(rows >= j, pltpu.roll(x, j, 0), 0.0)
        xc = xc + cw_ref[LRU_CONV - 1 - j:LRU_CONV - j, :] * shifted
    a, g = _lru_gates(xc, wa_ref, ba_ref, wi_ref, bi_ref, lam_ref)
    step = 1
    while step < seq:
        valid = rows >= step
        a_prev = pltpu.roll(a, step, 0)
        g_prev = pltpu.roll(g, step, 0)
        g = jnp.where(valid, a * g_prev + g, g)
        a = jnp.where(valid, a * a_prev, a)
        step *= 2
    o_ref[...] = g
    hl_ref[...] = g[seq - 1:seq, :]
    cn_ref[...] = x[seq - (LRU_CONV - 1):, :]


def _lru_prompt(xl, cw, cb, wa, ba, wi, bi, lam, batch, seq):
    vec = _resident((1, LRU_W))
    mat = _resident((LRU_W, LRU_W))
    return pl.pallas_call(
        functools.partial(_lru_prompt_kernel, seq=seq),
        grid=(batch,),
        in_specs=[pl.BlockSpec((seq, LRU_W), lambda b: (b, 0)), _resident((LRU_CONV, LRU_W)), vec,
                  mat, vec, mat, vec, vec],
        out_specs=[pl.BlockSpec((seq, LRU_W), lambda b: (b, 0)),
                   pl.BlockSpec((None, 1, LRU_W), lambda b: (b, 0, 0)),
                   pl.BlockSpec((None, LRU_CONV - 1, LRU_W), lambda b: (b, 0, 0))],
        out_shape=[jax.ShapeDtypeStruct((batch * seq, LRU_W), F32),
                   jax.ShapeDtypeStruct((batch, 1, LRU_W), F32),
                   jax.ShapeDtypeStruct((batch, LRU_CONV - 1, LRU_W), F32)],
        compiler_params=_params(("parallel",)),
        name="lru_prompt",
    )(xl, cw, cb, wa, ba, wi, bi, lam)


def _gla_prompt_kernel(q_ref, k_ref, v_ref, la_ref, o_ref, sfin_ref, s_ref, *, chunk, n_chunks):
    ci = pl.program_id(1)

    @pl.when(ci == 0)
    def _init():
        s_ref[...] = jnp.zeros_like(s_ref)

    qs = q_ref[...] * (GLA_DK ** -0.5)
    k = k_ref[...]
    v = v_ref[...]
    r_i = lax.broadcasted_iota(jnp.int32, (chunk, chunk), 0)
    c_i = lax.broadcasted_iota(jnp.int32, (chunk, chunk), 1)
    lower = (c_i <= r_i).astype(BF16)
    bc = _split_dot_left(lower, la_ref[...], 3)
    rows = lax.broadcasted_iota(jnp.int32, (chunk, GLA_QK), 0)
    qk_head = _idiv(lax.broadcasted_iota(jnp.int32, (1, GLA_QK), 1), GLA_DK)
    v_head = _idiv(lax.broadcasted_iota(jnp.int32, (1, GLA_W), 1), GLA_DV)
    own = (_idiv(lax.broadcasted_iota(jnp.int32, (GLA_QK, GLA_W), 0), GLA_DK)
           == _idiv(lax.broadcasted_iota(jnp.int32, (GLA_QK, GLA_W), 1), GLA_DV))

    state = s_ref[...]
    out = _dot((qs * jnp.exp(bc)).astype(BF16), state.astype(BF16))

    scores = [jnp.zeros((chunk, chunk), F32) for _ in range(GLA_HEADS)]
    level = 2 * SUBLANES
    while level <= chunk:
        half = level // 2
        mid = jnp.concatenate(
            [jnp.broadcast_to(bc[b * level + half - 1:b * level + half, :], (level, GLA_QK))
             for b in range(chunk // level)], axis=0)
        second = (rows & (level - 1)) >= half
        q_dec = jnp.where(second, qs * jnp.exp(bc - mid), 0.0)
        k_dec = jnp.where(second, 0.0, k * jnp.exp(mid - bc)).astype(BF16)
        same_block = _idiv(r_i, level) == _idiv(c_i, level)
        for h in range(GLA_HEADS):
            sc = _dot_nt(jnp.where(qk_head == h, q_dec, 0.0).astype(BF16), k_dec)
            if level < chunk:
                sc = jnp.where(same_block, sc, 0.0)
            scores[h] = scores[h] + sc
        level *= 2
    for h in range(GLA_HEADS):
        out = out + _dot(scores[h].astype(BF16), jnp.where(v_head == h, v, 0.0).astype(BF16))

    expand = jnp.where(own, 1.0, 0.0).astype(BF16)
    out = out + _split_dot(qs * k, expand, 2) * v
    for j in range(1, SUBLANES):
        ok = (rows & (SUBLANES - 1)) >= j
        p = jnp.where(ok, qs * pltpu.roll(k, j, 0) * jnp.exp(bc - pltpu.roll(bc, j, 0)), 0.0)
        out = out + _split_dot(p, expand, 2) * pltpu.roll(v, j, 0)
    o_ref[...] = out

    last = bc[chunk - 1:chunk, :]
    kv = _dot_tn((k * jnp.exp(last - bc)).astype(BF16), v.astype(BF16))
    dec_col = jnp.transpose(jnp.broadcast_to(jnp.exp(last), (GLA_QK, GLA_QK)))
    dec = jnp.concatenate([dec_col] * (GLA_W // GLA_QK), axis=1)
    new_state = dec * state + jnp.where(own, kv, 0.0)
    s_ref[...] = new_state

    @pl.when(ci == n_chunks - 1)
    def _emit():
        for h in range(GLA_HEADS):
            sfin_ref[h * GLA_DK:(h + 1) * GLA_DK, :] = (
                new_state[h * GLA_DK:(h + 1) * GLA_DK, h * GLA_DV:(h + 1) * GLA_DV])


def _gla_prompt(qg, kg, vg, la, batch, seq):
    chunk = min(GLA_CHUNK, seq)
    n_chunks = seq // chunk
    row = lambda w: pl.BlockSpec((chunk, w), lambda b, c: (b * n_chunks + c, 0))
    return pl.pallas_call(
        functools.partial(_gla_prompt_kernel, chunk=chunk, n_chunks=n_chunks),
        grid=(batch, n_chunks),
        in_specs=[row(GLA_QK), row(GLA_QK), row(GLA_W), row(GLA_QK)],
        out_specs=[row(GLA_W), pl.BlockSpec((None, GLA_QK, GLA_DV), lambda b, c: (b, 0, 0))],
        out_shape=[jax.ShapeDtypeStruct((batch * seq, GLA_W), F32),
                   jax.ShapeDtypeStruct((batch, GLA_QK, GLA_DV), F32)],
        scratch_shapes=[pltpu.VMEM((GLA_QK, GLA_W), F32)],
        compiler_params=_params(("parallel", "arbitrary")),
        name="gla_prompt",
    )(qg, kg, vg, la)


def _step_kernel(x_ref, conv_ref, h0_ref, cw_ref, cb_ref, wa_ref, ba_ref, wi_ref, bi_ref, lam_ref,
                 qc_ref, kc_ref, lac_ref, v_ref, s0_ref,
                 h_ref, cn_ref, og_ref, s_ref):
    x = x_ref[...]
    xc = cb_ref[...] + cw_ref[LRU_CONV - 1:LRU_CONV, :] * x
    for j in range(LRU_CONV - 1):
        xc = xc + cw_ref[j:j + 1, :] * conv_ref[j]
    a, g = _lru_gates(xc, wa_ref, ba_ref, wi_ref, bi_ref, lam_ref)
    h_ref[...] = a * h0_ref[...] + g
    for j in range(LRU_CONV - 2):
        cn_ref[j] = conv_ref[j + 1]
    cn_ref[LRU_CONV - 2] = x

    alpha = jnp.exp(lac_ref[...])
    kc = kc_ref[...]
    qc = qc_ref[...] * (GLA_DK ** -0.5)
    for h in range(GLA_HEADS):
        rs = slice(h * GLA_DK, (h + 1) * GLA_DK)
        new = alpha[:, rs, :] * s0_ref[:, rs, :] + kc[:, rs, :] * v_ref[:, h:h + 1, :]
        s_ref[:, rs, :] = new
        og_ref[:, h:h + 1, :] = jnp.sum(qc[:, rs, :] * new, axis=1, keepdims=True)


def _step(xl, conv0, h0, cw, cb, wa, ba, wi, bi, lam, q_col, k_col, la_col, v3, s0):
    b = xl.shape[0]
    return pl.pallas_call(
        _step_kernel,
        out_shape=[jax.ShapeDtypeStruct((b, LRU_W), F32),
                   jax.ShapeDtypeStruct((LRU_CONV - 1, b, LRU_W), F32),
                   jax.ShapeDtypeStruct((b, GLA_HEADS, GLA_DV), F32),
                   jax.ShapeDtypeStruct((b, GLA_QK, GLA_DV), F32)],
        compiler_params=pltpu.CompilerParams(vmem_limit_bytes=VMEM_LIMIT_BYTES),
        name="sample_step",
    )(xl, conv0, h0, cw, cb, wa, ba, wi, bi, lam, q_col, k_col, la_col, v3, s0)


def _block_diag(w):
    nb, bi, bj = w.shape
    eye = jnp.eye(nb, dtype=w.dtype)
    return (eye[:, None, :, None] * w[:, :, None, :]).reshape(nb * bi, nb * bj)


def _row_tile(n):
    return min(ROW_TILE, n)


def kernel(x_prompt, x_sample, cache_k, cache_v, page_table, state_lru_h, state_lru_conv, state_gla,
           g_ffn1, w_ffn1_gate, w_ffn1_up, w_ffn1_down, g_mix, w_in, g_qnorm, g_knorm, sb_bias,
           conv_w, conv_b, lru_wa, lru_ba, lru_wi, lru_bi, lru_lambda, gla_w_alpha, gla_b_alpha,
           g_mix_out, w_out, g_ffn2, w_ffn2_gate, w_ffn2_up, w_ffn2_down):
    depth = w_in.shape[0]
    bp, seq, _ = x_prompt.shape
    bs, dec_seq, _ = x_sample.shape
    assert dec_seq == 1
    n_p = bp * seq
    n_phys = cache_k.shape[1]

    w_in_p = jnp.pad(w_in, ((0, 0), (0, 0), (0, D_IN_PAD - w_in.shape[2]))).astype(BF16)
    w_kvt = w_in[:, :, _OFF_K:_OFF_XL].transpose(0, 2, 1).astype(BF16)
    w_al = jnp.pad(gla_w_alpha, ((0, 0), (0, LANES - GLA_RANK), (0, 0))).astype(BF16)
    w_out_b = w_out.astype(BF16)
    wa_bd = jax.vmap(_block_diag)(lru_wa).astype(BF16)
    wi_bd = jax.vmap(_block_diag)(lru_wi).astype(BF16)
    gq = jnp.tile(g_qnorm, (1, SB_HEADS))[:, None, :]
    gk = jnp.tile(g_knorm, (1, SB_HEADS))[:, None, :]
    gk_col = jnp.tile(g_knorm, (1, SB_HEADS))[:, :, None]
    lane = jnp.arange(LANES)
    e64 = ((lane[:, None] // HEAD_DIM == lane[None, :] // HEAD_DIM).astype(F32) / HEAD_DIM).astype(BF16)
    vec = lambda t, l: t[l][None, :]

    cache_kt = cache_k.transpose(0, 1, 3, 4, 2).reshape(depth, n_phys, SB_W, PAGE_SIZE)
    cache_vt = cache_v.transpose(0, 1, 3, 4, 2).reshape(depth, n_phys, SB_W, PAGE_SIZE)

    xp = x_prompt.reshape(n_p, D_MODEL)
    xs = x_sample.reshape(bs, D_MODEL)
    tm_p = _row_tile(n_p)
    tm_s = _row_tile(bs)
    kt_all = jnp.zeros((depth, bp, SB_W, seq), F32)
    vt_all = jnp.zeros((depth, bp, SB_W, seq), F32)
    outs = {name: [] for name in ("ks", "vs", "hp", "hs", "cp", "cs", "sp", "ss")}

    for l in range(depth):
        lru_w = (conv_w[l], vec(conv_b, l), wa_bd[l], vec(lru_ba, l), wi_bd[l], vec(lru_bi, l),
                 vec(lru_lambda, l))
        inproj_w = (vec(g_mix, l), w_in_p[l], gq[l])
        inproj_w2 = (w_al[l], vec(gla_b_alpha, l), e64)

        ffn1_w = (vec(g_ffn1, l), w_ffn1_gate, w_ffn1_up, w_ffn1_down, l)
        ffn2_w = (vec(g_ffn2, l), w_ffn2_gate, w_ffn2_up, w_ffn2_down, l)
        xp = _ffn(xp, *ffn1_w, tm_p)
        q, kt_all, vt_all, xl, gate, qg, kg, vg, rg, la = _inproj(
            xp, *inproj_w, gk_col[l], *inproj_w2, tm_p, (kt_all, vt_all), l, w_kvt[l], seq)
        osb = _sb_prompt(q, kt_all, vt_all, sb_bias[l], l)
        olru, h_last, conv_new = _lru_prompt(xl, *lru_w, bp, seq)
        ogla, s_fin = _gla_prompt(qg, kg, vg, la, bp, seq)
        xp = _merge(xp, osb, olru, ogla, gate, rg, vec(g_mix_out, l), e64, w_out_b[l], tm_p)
        xp = _ffn(xp, *ffn2_w, tm_p)
        outs["hp"].append(h_last.reshape(bp, LRU_W))
        outs["cp"].append(conv_new)
        outs["sp"].append(s_fin.reshape(bp, GLA_HEADS, GLA_DK, GLA_DV))

        xs = _ffn(xs, *ffn1_w, tm_s)
        q, k_new, v_new, xl, gate, qg, kg, vg, rg, la = _inproj(xs, *inproj_w, gk[l], *inproj_w2, tm_s)
        bias_b = jnp.broadcast_to(sb_bias[l][:, None], (SB_HEADS, PAGE_SIZE))
        osb = _sb_decode(q.reshape(bs, SB_W, 1), bias_b, cache_kt, cache_vt, page_table, l)
        h_new, conv_new, ogla, s_new = _step(
            xl, state_lru_conv[l].transpose(1, 0, 2), state_lru_h[l], *lru_w,
            qg.reshape(bs, GLA_QK, 1), kg.reshape(bs, GLA_QK, 1), la.reshape(bs, GLA_QK, 1),
            vg.reshape(bs, GLA_HEADS, GLA_DV), state_gla[l].reshape(bs, GLA_QK, GLA_DV))
        xs = _merge(xs, osb.reshape(bs, SB_W), h_new, ogla.reshape(bs, GLA_W), gate, rg,
                    vec(g_mix_out, l), e64, w_out_b[l], tm_s)
        xs = _ffn(xs, *ffn2_w, tm_s)
        outs["ks"].append(k_new.reshape(bs, 1, SB_HEADS, HEAD_DIM))
        outs["vs"].append(v_new.reshape(bs, 1, SB_HEADS, HEAD_DIM))
        outs["hs"].append(h_new)
        outs["cs"].append(conv_new.transpose(1, 0, 2))
        outs["ss"].append(s_new.reshape(bs, GLA_HEADS, GLA_DK, GLA_DV))

    st = lambda name: jnp.stack(outs[name])
    return (xp.reshape(bp, seq, D_MODEL), xs.reshape(bs, 1, D_MODEL),
            kt_all.reshape(depth, bp, SB_HEADS, HEAD_DIM, seq).transpose(0, 1, 4, 2, 3),
            vt_all.reshape(depth, bp, SB_HEADS, HEAD_DIM, seq).transpose(0, 1, 4, 2, 3),
            st("ks"), st("vs"), st("hp"), st("hs"), st("cp"), st("cs"), st("sp"), st("ss"))
```

```python
import functools

import jax
import jax.numpy as jnp
from jax import lax
from jax.experimental import pallas as pl
from jax.experimental.pallas import tpu as pltpu

F32 = jnp.float32
BF16 = jnp.bfloat16

D_MODEL = 1024
HEAD_DIM = 64
SB_W = D_MODEL // 2
SB_HEADS = SB_W // HEAD_DIM
LRU_W = D_MODEL // 4
LRU_BLOCKS = LRU_W // HEAD_DIM
LRU_CONV = 4
LRU_C = 8.0
GLA_W = D_MODEL // 4
GLA_HEADS = GLA_W // HEAD_DIM
GLA_DV = HEAD_DIM
GLA_DK = HEAD_DIM // 2
GLA_QK = GLA_HEADS * GLA_DK
GLA_RANK = 16
GLA_TAU = 16.0
D_MIX = SB_W + LRU_W + GLA_W
D_FF = 2816
EPS = 1e-6
PAGE_SIZE = 128

LANES = 128
SUBLANES = 8
VMEM_LIMIT_BYTES = 56 * 1024 * 1024

_OFF_Q = 0
_OFF_K = _OFF_Q + SB_W
_OFF_V = _OFF_K + SB_W
_OFF_XL = _OFF_V + SB_W
_OFF_GATE = _OFF_XL + LRU_W
_OFF_QG = _OFF_GATE + LRU_W
_OFF_KG = _OFF_QG + GLA_QK
_OFF_VG = _OFF_KG + GLA_QK
_OFF_RG = _OFF_VG + GLA_W
_OFF_ALR = _OFF_RG + GLA_W
D_IN_PAD = _OFF_ALR + LANES

FF_CHUNK = 256
HEAD_MEAN_WIDTH = 256
ROW_TILE = 512
SB_BLOCK = 256
GLA_CHUNK = 128
DECODE_PAGES_PER_STEP = 32


def _idiv(x, d):
    assert d & (d - 1) == 0
    return x >> (d.bit_length() - 1)


def _imod(x, d):
    assert d & (d - 1) == 0
    return x & (d - 1)


def _dot(a, b):
    return jnp.dot(a, b, preferred_element_type=F32)


def _dot_nt(a, b):
    return lax.dot_general(a, b, (((1,), (1,)), ((), ())), preferred_element_type=F32)


def _dot_tn(a, b):
    return lax.dot_general(a, b, (((0,), (0,)), ((), ())), preferred_element_type=F32)


def _split_dot(x, w, passes):
    hi = x.astype(BF16)
    acc = _dot(hi, w)
    rem = x - hi.astype(F32)
    for _ in range(passes - 1):
        lo = rem.astype(BF16)
        acc = acc + _dot(lo, w)
        rem = rem - lo.astype(F32)
    return acc


def _split_dot_left(w, x, passes):
    hi = x.astype(BF16)
    acc = _dot(w, hi)
    rem = x - hi.astype(F32)
    for _ in range(passes - 1):
        lo = rem.astype(BF16)
        acc = acc + _dot(w, lo)
        rem = rem - lo.astype(F32)
    return acc


def _rms(x, g):
    ms = jnp.mean(x * x, axis=-1, keepdims=True)
    return x * lax.rsqrt(ms + EPS) * g


def _softplus(z):
    return jnp.maximum(z, 0.0) + jnp.log1p(jnp.exp(-jnp.abs(z)))


def _log_sigmoid(z):
    return jnp.minimum(z, 0.0) - jnp.log1p(jnp.exp(-jnp.abs(z)))


def _gelu_tanh(x):
    return 0.5 * x * (1.0 + jnp.tanh(0.7978845608028654 * (x + 0.044715 * (x * x * x))))


def _head_rms_cols(x, e64, g):
    cw = e64.shape[0]
    cols = []
    for c in range(x.shape[1] // cw):
        xc = x[:, c * cw:(c + 1) * cw]
        ms = _split_dot(xc * xc, e64, 2)
        cols.append(xc * lax.rsqrt(ms + EPS) * g[:, c * cw:(c + 1) * cw])
    return cols


def _resident(shape):
    nd = len(shape)
    return pl.BlockSpec(shape, lambda *_: (0,) * nd, pipeline_mode=pl.Buffered(1))


def _layer_resident(shape, layer):
    nd = len(shape)
    return pl.BlockSpec((None,) + tuple(shape), lambda *_: (layer,) + (0,) * nd, pipeline_mode=pl.Buffered(1))


def _params(semantics):
    return pltpu.CompilerParams(dimension_semantics=semantics, vmem_limit_bytes=VMEM_LIMIT_BYTES)


def _ffn_kernel(x_ref, g_ref, wg_ref, wu_ref, wd_ref, o_ref, h_ref, acc_ref):
    x = x_ref[...]
    h_ref[...] = _rms(x, g_ref[...]).astype(BF16)
    acc_ref[...] = jnp.zeros_like(acc_ref)

    def body(c, carry):
        cols = pl.ds(pl.multiple_of(c * FF_CHUNK, FF_CHUNK), FF_CHUNK)
        h = h_ref[...]
        gate = _dot(h, wg_ref[:, cols].astype(BF16))
        up = _dot(h, wu_ref[:, cols].astype(BF16))
        act = (gate * jax.nn.sigmoid(gate) * up).astype(BF16)
        acc_ref[...] += _dot(act, wd_ref[cols, :].astype(BF16))
        return carry

    lax.fori_loop(0, D_FF // FF_CHUNK, body, 0, unroll=True)
    o_ref[...] = x + 0.5 * acc_ref[...]


def _ffn(x, g, wg, wu, wd, layer, tm):
    n = x.shape[0]
    return pl.pallas_call(
        _ffn_kernel,
        grid=(n // tm,),
        in_specs=[
            pl.BlockSpec((tm, D_MODEL), lambda i: (i, 0)),
            _resident((1, D_MODEL)),
            _layer_resident((D_MODEL, D_FF), layer),
            _layer_resident((D_MODEL, D_FF), layer),
            _layer_resident((D_FF, D_MODEL), layer),
        ],
        out_specs=pl.BlockSpec((tm, D_MODEL), lambda i: (i, 0)),
        out_shape=jax.ShapeDtypeStruct((n, D_MODEL), F32),
        scratch_shapes=[pltpu.VMEM((tm, D_MODEL), BF16), pltpu.VMEM((tm, D_MODEL), F32)],
        compiler_params=_params(("parallel",)),
        name="ffn",
    )(x, g, wg, wu, wd)


def _inproj_kernel(*refs, transposed_kv):
    if transposed_kv:
        (x_ref, g_ref, w_ref, gq_ref, gk_ref, wal_ref, bal_ref, e64_ref, wkvt_ref,
         q_ref, k_ref, v_ref, xl_ref, gate_ref, qg_ref, kg_ref, vg_ref, rg_ref, la_ref, h_ref) = refs
    else:
        (x_ref, g_ref, w_ref, gq_ref, gk_ref, wal_ref, bal_ref, e64_ref,
         q_ref, k_ref, v_ref, xl_ref, gate_ref, qg_ref, kg_ref, vg_ref, rg_ref, la_ref, h_ref) = refs
    h_ref[...] = _rms(x_ref[...], g_ref[...]).astype(BF16)

    def proj(lo, width):
        return _dot(h_ref[...], w_ref[:, lo:lo + width])

    e64 = e64_ref[...]
    q_cols = _head_rms_cols(proj(_OFF_Q, SB_W), e64, gq_ref[...])
    cw = e64.shape[0]
    for c, col in enumerate(q_cols):
        q_ref[:, c * cw:(c + 1) * cw] = col * (HEAD_DIM ** -0.5)
    if transposed_kv:
        rows = h_ref.shape[0]
        k_t = _dot_nt(wkvt_ref[:SB_W, :], h_ref[...]).reshape(SB_HEADS, HEAD_DIM, rows)
        ms = jnp.mean(k_t * k_t, axis=1, keepdims=True)
        k_ref[...] = (k_t * lax.rsqrt(ms + EPS)).reshape(SB_W, rows) * gk_ref[...]
        v_ref[...] = _dot_nt(wkvt_ref[SB_W:, :], h_ref[...])
    else:
        k_cols = _head_rms_cols(proj(_OFF_K, SB_W), e64, gk_ref[...])
        for c, col in enumerate(k_cols):
            k_ref[:, c * cw:(c + 1) * cw] = col
        v_ref[...] = proj(_OFF_V, SB_W)
    xl_ref[...] = proj(_OFF_XL, LRU_W)
    gate_ref[...] = _gelu_tanh(proj(_OFF_GATE, LRU_W))
    qg_ref[...] = proj(_OFF_QG, GLA_QK)
    kg_ref[...] = proj(_OFF_KG, GLA_QK)
    vg_ref[...] = proj(_OFF_VG, GLA_W)
    r = proj(_OFF_RG, GLA_W)
    rg_ref[...] = r * jax.nn.sigmoid(r)
    a_lr = proj(_OFF_ALR, LANES)
    xa = _dot(a_lr.astype(BF16), wal_ref[...]) + bal_ref[...]
    la_ref[...] = _log_sigmoid(xa) * (1.0 / GLA_TAU)


def _inproj(x, g, w_in, gq, gk, wal, bal, e64, tm, kv_slabs=None, layer=None, w_kvt=None, seq=None):
    n = x.shape[0]
    row = lambda w: pl.BlockSpec((tm, w), lambda i: (i, 0))
    in_specs = [
        row(D_MODEL),
        _resident((1, D_MODEL)),
        _resident((D_MODEL, D_IN_PAD)),
        _resident((1, SB_W)),
        _resident(gk.shape),
        _resident((LANES, LANES)),
        _resident((1, LANES)),
        _resident(e64.shape),
    ]
    args = [x, g, w_in, gq, gk, wal, bal, e64]
    small = lambda w: jax.ShapeDtypeStruct((n, w), F32)
    transposed_kv = kv_slabs is not None
    if not transposed_kv:
        kv_specs = [row(SB_W), row(SB_W)]
        kv_shapes = [small(SB_W), small(SB_W)]
        aliases = {}
        kernel = functools.partial(_inproj_kernel, transposed_kv=False)
    else:
        k_all, v_all = kv_slabs
        tiles = seq // tm
        slab = pl.BlockSpec((None, None, SB_W, tm), lambda i: (layer, i // tiles, 0, i % tiles))
        kv_specs = [slab, slab]
        kv_shapes = [jax.ShapeDtypeStruct(k_all.shape, F32), jax.ShapeDtypeStruct(v_all.shape, F32)]
        in_specs += [_resident((2 * SB_W, D_MODEL)),
                     pl.BlockSpec(memory_space=pl.ANY), pl.BlockSpec(memory_space=pl.ANY)]
        args += [w_kvt, k_all, v_all]
        aliases = {len(args) - 2: 1, len(args) - 1: 2}

        def kernel(*refs):
            _inproj_kernel(*refs[:9], *refs[11:], transposed_kv=True)

    out_specs = [row(SB_W)] + kv_specs + [row(LRU_W), row(LRU_W), row(GLA_QK), row(GLA_QK),
                                          row(GLA_W), row(GLA_W), row(GLA_QK)]
    out_shape = [small(SB_W)] + kv_shapes + [small(LRU_W), small(LRU_W), small(GLA_QK), small(GLA_QK),
                                             small(GLA_W), small(GLA_W), small(GLA_QK)]
    return pl.pallas_call(
        kernel,
        grid=(n // tm,),
        in_specs=in_specs,
        out_specs=out_specs,
        out_shape=out_shape,
        input_output_aliases=aliases,
        scratch_shapes=[pltpu.VMEM((tm, D_MODEL), BF16)],
        compiler_params=_params(("parallel",)),
        name="inproj",
    )(*args)


def _merge_kernel(x_ref, osb_ref, olru_ref, ogla_ref, gate_ref, rg_ref, gmo_ref, e64_ref, wout_ref, o_ref):
    e64 = e64_ref[...]
    gmo = gmo_ref[...]
    sb = _head_rms_cols(osb_ref[...], e64, gmo[:, :SB_W])
    lru = _head_rms_cols(olru_ref[...], e64, gmo[:, SB_W:SB_W + LRU_W])
    gla = _head_rms_cols(ogla_ref[...], e64, gmo[:, SB_W + LRU_W:])
    gate = gate_ref[...]
    rg = rg_ref[...]
    cw = e64.shape[0]
    lru = [col * gate[:, c * cw:(c + 1) * cw] for c, col in enumerate(lru)]
    gla = [col * rg[:, c * cw:(c + 1) * cw] for c, col in enumerate(gla)]
    acc = x_ref[...]
    for c, col in enumerate(sb + lru + gla):
        acc = acc + _dot(col.astype(BF16), wout_ref[c * cw:(c + 1) * cw, :])
    o_ref[...] = acc


def _merge(x, osb, olru, ogla, gate, rg, gmo, e64, wout, tm):
    n = x.shape[0]
    row = lambda w: pl.BlockSpec((tm, w), lambda i: (i, 0))
    return pl.pallas_call(
        _merge_kernel,
        grid=(n // tm,),
        in_specs=[row(D_MODEL), row(SB_W), row(LRU_W), row(GLA_W), row(LRU_W), row(GLA_W),
                  _resident((1, D_MIX)), _resident(e64.shape), _resident((D_MIX, D_MODEL))],
        out_specs=row(D_MODEL),
        out_shape=jax.ShapeDtypeStruct((n, D_MODEL), F32),
        compiler_params=_params(("parallel",)),
        name="merge",
    )(x, osb, olru, ogla, gate, rg, gmo, e64, wout)


_LOG2_E = 1.4426950408889634


def _softplus_plain(z):
    return jnp.maximum(z, 0.0) + jnp.log(1.0 + jnp.exp2(jnp.abs(z) * (-_LOG2_E)))


def _sb_tile_stages(q_h, kt_b, vt_h, bias, upper, tail, mask):
    st = {}

    def logits():
        st["z"] = _dot(q_h, kt_b) + bias

    def log_terms():
        z = st["z"]
        sp = _softplus_plain(z)
        st["lb"] = z - sp
        if mask is not None:
            sp = jnp.where(mask, sp, 0.0)
        st["edge"] = sp[:, 0:1]
        st["sp"] = sp.astype(BF16)

    def suffix_sums():
        st["within"] = _dot(st["sp"], upper)

    def weights():
        within = st["within"]
        a = jnp.exp(st["lb"] - within - tail)
        if mask is not None:
            a = jnp.where(mask, a, 0.0)
        st["a"] = a.astype(BF16)
        st["tail"] = tail + within[:, 0:1] + st["edge"]

    def values():
        return _dot_nt(st["a"], vt_h), st["tail"]

    return [logits, log_terms, suffix_sums, weights, values]


def _run_skewed(pipelines):
    depth = len(pipelines[0])
    results = [None] * len(pipelines)
    for step in range(len(pipelines) + depth - 1):
        for p, stages in enumerate(pipelines):
            s = step - p
            if 0 <= s < depth:
                results[p] = stages[s]()
    return results


def _sb_prompt_kernel(bias_ref, q_ref, kt_ref, vt_ref, o_ref, kb_ref, vh_ref, *, blk):
    pairs = SB_W // LANES
    i = pl.program_id(1)
    first = lax.broadcasted_iota(jnp.int32, (1, LANES), 1) < HEAD_DIM

    @pl.when(i == 0)
    def _stage():
        kb_ref[...] = kt_ref[...].astype(BF16)
        first_rows = lax.broadcasted_iota(jnp.int32, (LANES, vt_ref.shape[1]), 0) < HEAD_DIM
        for p in range(pairs):
            v = vt_ref[p * LANES:(p + 1) * LANES, :]
            vh_ref[2 * p] = jnp.where(first_rows, v, 0.0).astype(BF16)
            vh_ref[2 * p + 1] = jnp.where(first_rows, 0.0, v).astype(BF16)

    q_heads = []
    biases = []
    for p in range(pairs):
        q = q_ref[:, p * LANES:(p + 1) * LANES]
        q_heads += [jnp.where(first, q, 0.0).astype(BF16), jnp.where(first, 0.0, q).astype(BF16)]
        biases += [bias_ref[2 * p], bias_ref[2 * p + 1]]
    r_i = lax.broadcasted_iota(jnp.int32, (blk, blk), 0)
    c_i = lax.broadcasted_iota(jnp.int32, (blk, blk), 1)
    upper = jnp.where(r_i > c_i, 1.0, 0.0).astype(BF16)
    causal = c_i < r_i

    def tile(j, carries, accs, mask):
        cols = pl.ds(pl.multiple_of(j * blk, blk), blk)
        pipelines = []
        for h in range(2 * pairs):
            kt_b = kb_ref[(h // 2) * LANES:(h // 2 + 1) * LANES, cols]
            pipelines.append(_sb_tile_stages(q_heads[h], kt_b, vh_ref[h, :, cols], biases[h], upper,
                                             carries[h], mask))
        results = _run_skewed(pipelines)
        new_accs = list(accs)
        for h, (out, _) in enumerate(results):
            new_accs[h // 2] = new_accs[h // 2] + out
        return tuple(c for _, c in results), tuple(new_accs)

    zero_c = jnp.zeros((blk, 1), F32)
    zero_a = jnp.zeros((blk, LANES), F32)
    state = tile(i, (zero_c,) * (2 * pairs), (zero_a,) * pairs, causal)

    def body(it, state):
        return tile(i - 1 - it, state[0], state[1], None)

    _, accs = lax.fori_loop(0, i, body, state)
    for p in range(pairs):
        o_ref[:, p * LANES:(p + 1) * LANES] = accs[p]


def _sb_prompt(q, kt_all, vt_all, bias, layer):
    _, batch, _, seq = kt_all.shape
    blk = min(SB_BLOCK, seq)
    nq = seq // blk
    kv_spec = pl.BlockSpec((None, None, SB_W, seq), lambda b, i, *_: (layer, b, 0, 0))
    return pl.pallas_call(
        functools.partial(_sb_prompt_kernel, blk=blk),
        grid_spec=pltpu.PrefetchScalarGridSpec(
            num_scalar_prefetch=1,
            grid=(batch, nq),
            in_specs=[pl.BlockSpec((blk, SB_W), lambda b, i, *_: (b * nq + i, 0)), kv_spec, kv_spec],
            out_specs=pl.BlockSpec((blk, SB_W), lambda b, i, *_: (b * nq + i, 0)),
            scratch_shapes=[pltpu.VMEM((SB_W, seq), BF16), pltpu.VMEM((SB_HEADS, LANES, seq), BF16)],
        ),
        out_shape=jax.ShapeDtypeStruct((batch * seq, SB_W), F32),
        compiler_params=_params(("parallel", "arbitrary")),
        name="sb_prompt",
    )(bias, q, kt_all, vt_all)


def _sb_decode_kernel(pt_ref, q_ref, bias_ref, *refs, n_pages, group):
    k_refs = refs[:group]
    v_refs = refs[group:2 * group]
    o_ref = refs[2 * group]
    z_ref, a_ref, acc_ref = refs[2 * group + 1:]
    s = pl.program_id(1)
    steps = n_pages // group
    rows = n_pages * SB_HEADS

    @pl.when(s < steps)
    def _scores():
        q_col = q_ref[...]
        for g in range(group):
            prod = (k_refs[g][...] * q_col).reshape(SB_HEADS, HEAD_DIM, PAGE_SIZE)
            z = jnp.sum(prod, axis=1) + bias_ref[...]
            row0 = pl.multiple_of((s * group + g) * SB_HEADS, SB_HEADS)
            z_ref[pl.ds(row0, SB_HEADS), :] = z

    @pl.when(s == steps - 1)
    def _weights():
        z = z_ref[...]
        sp = _softplus(z)
        l1m = -sp
        r_i = lax.broadcasted_iota(jnp.int32, (PAGE_SIZE, PAGE_SIZE), 0)
        c_i = lax.broadcasted_iota(jnp.int32, (PAGE_SIZE, PAGE_SIZE), 1)
        upper = (r_i > c_i).astype(BF16)
        ones = jnp.ones((PAGE_SIZE, PAGE_SIZE), BF16)
        within = _split_dot(l1m, upper, 3)
        page_total = _split_dot(l1m, ones, 3)
        pr = lax.broadcasted_iota(jnp.int32, (rows, rows), 0)
        pc = lax.broadcasted_iota(jnp.int32, (rows, rows), 1)
        same_head = _imod(pc, SB_HEADS) == _imod(pr, SB_HEADS)
        later_page = jnp.where(_idiv(pc, SB_HEADS) > _idiv(pr, SB_HEADS),
                               jnp.where(same_head, 1.0, 0.0), 0.0).astype(BF16)
        later = _split_dot_left(later_page, page_total, 3)
        a_ref[...] = jnp.exp(z - sp + within + later)
        acc_ref[...] = jnp.zeros_like(acc_ref)

    @pl.when(s >= steps)
    def _values():
        acc = acc_ref[...]
        for g in range(group):
            row0 = pl.multiple_of(((s - steps) * group + g) * SB_HEADS, SB_HEADS)
            a_g = a_ref[pl.ds(row0, SB_HEADS), :]
            a_rows = jnp.concatenate(
                [jnp.broadcast_to(a_g[h:h + 1, :], (HEAD_DIM, PAGE_SIZE)) for h in range(SB_HEADS)], axis=0)
            acc = acc + v_refs[g][...] * a_rows
        acc_ref[...] = acc

    @pl.when(s == 2 * steps - 1)
    def _emit():
        o_ref[...] = jnp.sum(acc_ref[...], axis=1, keepdims=True)


def _sb_decode(q_col, bias_b, cache_kt, cache_vt, page_table, layer):
    b, n_pages = page_table.shape
    group = min(DECODE_PAGES_PER_STEP, n_pages)
    steps = n_pages // group

    def k_map(g):
        return lambda bi, s, pt: (layer, pt[bi, jnp.minimum(s, steps - 1) * group + g], 0, 0)

    def v_map(g):
        def index(bi, s, pt):
            in_values = s >= steps
            row = jnp.where(in_values, bi, jnp.maximum(bi - 1, 0))
            col = jnp.where(in_values, s - steps, steps - 1) * group + g
            return (layer, pt[row, col], 0, 0)
        return index

    page = lambda m: pl.BlockSpec((None, None, SB_W, PAGE_SIZE), m)
    in_specs = [pl.BlockSpec((None, SB_W, 1), lambda bi, s, pt: (bi, 0, 0)),
                pl.BlockSpec((SB_HEADS, PAGE_SIZE), lambda bi, s, pt: (0, 0))]
    in_specs += [page(k_map(g)) for g in range(group)]
    in_specs += [page(v_map(g)) for g in range(group)]
    rows = n_pages * SB_HEADS
    return pl.pallas_call(
        functools.partial(_sb_decode_kernel, n_pages=n_pages, group=group),
        grid_spec=pltpu.PrefetchScalarGridSpec(
            num_scalar_prefetch=1,
            grid=(b, 2 * steps),
            in_specs=in_specs,
            out_specs=pl.BlockSpec((None, SB_W, 1), lambda bi, s, pt: (bi, 0, 0)),
            scratch_shapes=[pltpu.VMEM((rows, PAGE_SIZE), F32),
                            pltpu.VMEM((rows, PAGE_SIZE), F32),
                            pltpu.VMEM((SB_W, PAGE_SIZE), F32)],
        ),
        out_shape=jax.ShapeDtypeStruct((b, SB_W, 1), F32),
        compiler_params=_params(("parallel", "arbitrary")),
        name="sb_decode",
    )(page_table, q_col, bias_b, *([cache_kt] * group), *([cache_vt] * group))


def _lru_gates(xc, wa_ref, ba_ref, wi_ref, bi_ref, lam_ref):
    xb = xc.astype(BF16)
    r = jax.nn.sigmoid(_dot(xb, wa_ref[...]) + ba_ref[...])
    i = jax.nn.sigmoid(_dot(xb, wi_ref[...]) + bi_ref[...])
    log_a = -LRU_C * r * _softplus(-lam_ref[...])
    a = jnp.exp(log_a)
    t = jnp.tanh(log_a)
    one_minus_a2 = -2.0 * t / (1.0 - t)
    return a, jnp.sqrt(one_minus_a2) * (i * xc)


def _lru_prompt_kernel(x_ref, cw_ref, cb_ref, wa_ref, ba_ref, wi_ref, bi_ref, lam_ref,
                       o_ref, hl_ref, cn_ref, *, seq):
    x = x_ref[...]
    rows = lax.broadcasted_iota(jnp.int32, (seq, LRU_W), 0)
    xc = cb_ref[...] + cw_ref[LRU_CONV - 1:LRU_CONV, :] * x
    for j in range(1, LRU_CONV):
        shifted = jnp.where(rows >= j, pltpu.roll(x, j, 0), 0.0)
        xc = xc + cw_ref[LRU_CONV - 1 - j:LRU_CONV - j, :] * shifted
    a, g = _lru_gates(xc, wa_ref, ba_ref, wi_ref, bi_ref, lam_ref)
    step = 1
    while step < seq:
        valid = rows >= step
        a_prev = pltpu.roll(a, step, 0)
        g_prev = pltpu.roll(g, step, 0)
        g = jnp.where(valid, a * g_prev + g, g)
        a = jnp.where(valid, a * a_prev, a)
        step *= 2
    o_ref[...] = g
    hl_ref[...] = g[seq - 1:seq, :]
    cn_ref[...] = x[seq - (LRU_CONV - 1):, :]


def _lru_prompt(xl, cw, cb, wa, ba, wi, bi, lam, batch, seq):
    vec = _resident((1, LRU_W))
    mat = _resident((LRU_W, LRU_W))
    return pl.pallas_call(
        functools.partial(_lru_prompt_kernel, seq=seq),
        grid=(batch,),
        in_specs=[pl.BlockSpec((seq, LRU_W), lambda b: (b, 0)), _resident((LRU_CONV, LRU_W)), vec,
                  mat, vec, mat, vec, vec],
        out_specs=[pl.BlockSpec((seq, LRU_W), lambda b: (b, 0)),
                   pl.BlockSpec((None, 1, LRU_W), lambda b: (b, 0, 0)),
                   pl.BlockSpec((None, LRU_CONV - 1, LRU_W), lambda b: (b, 0, 0))],
        out_shape=[jax.ShapeDtypeStruct((batch * seq, LRU_W), F32),
                   jax.ShapeDtypeStruct((batch, 1, LRU_W), F32),
                   jax.ShapeDtypeStruct((batch, LRU_CONV - 1, LRU_W), F32)],
        compiler_params=_params(("parallel",)),
        name="lru_prompt",
    )(xl, cw, cb, wa, ba, wi, bi, lam)


def _gla_prompt_kernel(q_ref, k_ref, v_ref, la_ref, o_ref, sfin_ref, s_ref, *, chunk, n_chunks):
    ci = pl.program_id(1)

    @pl.when(ci == 0)
    def _init():
        s_ref[...] = jnp.zeros_like(s_ref)

    qs = q_ref[...] * (GLA_DK ** -0.5)
    k = k_ref[...]
    v = v_ref[...]
    r_i = lax.broadcasted_iota(jnp.int32, (chunk, chunk), 0)
    c_i = lax.broadcasted_iota(jnp.int32, (chunk, chunk), 1)
    lower = (c_i <= r_i).astype(BF16)
    bc = _split_dot_left(lower, la_ref[...], 3)
    rows = lax.broadcasted_iota(jnp.int32, (chunk, GLA_QK), 0)
    qk_head = _idiv(lax.broadcasted_iota(jnp.int32, (1, GLA_QK), 1), GLA_DK)
    v_head = _idiv(lax.broadcasted_iota(jnp.int32, (1, GLA_W), 1), GLA_DV)
    own = (_idiv(lax.broadcasted_iota(jnp.int32, (GLA_QK, GLA_W), 0), GLA_DK)
           == _idiv(lax.broadcasted_iota(jnp.int32, (GLA_QK, GLA_W), 1), GLA_DV))

    state = s_ref[...]
    out = _dot((qs * jnp.exp(bc)).astype(BF16), state.astype(BF16))

    scores = [jnp.zeros((chunk, chunk), F32) for _ in range(GLA_HEADS)]
    level = 2 * SUBLANES
    while level <= chunk:
        half = level // 2
        mid = jnp.concatenate(
            [jnp.broadcast_to(bc[b * level + half - 1:b * level + half, :], (level, GLA_QK))
             for b in range(chunk // level)], axis=0)
        second = (rows & (level - 1)) >= half
        q_dec = jnp.where(second, qs * jnp.exp(bc - mid), 0.0)
        k_dec = jnp.where(second, 0.0, k * jnp.exp(mid - bc)).astype(BF16)
        same_block = _idiv(r_i, level) == _idiv(c_i, level)
        for h in range(GLA_HEADS):
            sc = _dot_nt(jnp.where(qk_head == h, q_dec, 0.0).astype(BF16), k_dec)
            if level < chunk:
                sc = jnp.where(same_block, sc, 0.0)
            scores[h] = scores[h] + sc
        level *= 2
    for h in range(GLA_HEADS):
        out = out + _dot(scores[h].astype(BF16), jnp.where(v_head == h, v, 0.0).astype(BF16))

    expand = jnp.where(own, 1.0, 0.0).astype(BF16)
    out = out + _split_dot(qs * k, expand, 2) * v
    for j in range(1, SUBLANES):
        ok = (rows & (SUBLANES - 1)) >= j
        p = jnp.where(ok, qs * pltpu.roll(k, j, 0) * jnp.exp(bc - pltpu.roll(bc, j, 0)), 0.0)
        out = out + _split_dot(p, expand, 2) * pltpu.roll(v, j, 0)
    o_ref[...] = out

    last = bc[chunk - 1:chunk, :]
    kv = _dot_tn((k * jnp.exp(last - bc)).astype(BF16), v.astype(BF16))
    dec_col = jnp.transpose(jnp.broadcast_to(jnp.exp(last), (GLA_QK, GLA_QK)))
    dec = jnp.concatenate([dec_col] * (GLA_W // GLA_QK), axis=1)
    new_state = dec * state + jnp.where(own, kv, 0.0)
    s_ref[...] = new_state

    @pl.when(ci == n_chunks - 1)
    def _emit():
        for h in range(GLA_HEADS):
            sfin_ref[h * GLA_DK:(h + 1) * GLA_DK, :] = (
                new_state[h * GLA_DK:(h + 1) * GLA_DK, h * GLA_DV:(h + 1) * GLA_DV])


def _gla_prompt(qg, kg, vg, la, batch, seq):
    chunk = min(GLA_CHUNK, seq)
    n_chunks = seq // chunk
    row = lambda w: pl.BlockSpec((chunk, w), lambda b, c: (b * n_chunks + c, 0))
    return pl.pallas_call(
        functools.partial(_gla_prompt_kernel, chunk=chunk, n_chunks=n_chunks),
        grid=(batch, n_chunks),
        in_specs=[row(GLA_QK), row(GLA_QK), row(GLA_W), row(GLA_QK)],
        out_specs=[row(GLA_W), pl.BlockSpec((None, GLA_QK, GLA_DV), lambda b, c: (b, 0, 0))],
        out_shape=[jax.ShapeDtypeStruct((batch * seq, GLA_W), F32),
                   jax.ShapeDtypeStruct((batch, GLA_QK, GLA_DV), F32)],
        scratch_shapes=[pltpu.VMEM((GLA_QK, GLA_W), F32)],
        compiler_params=_params(("parallel", "arbitrary")),
        name="gla_prompt",
    )(qg, kg, vg, la)


def _step_kernel(x_ref, conv_ref, h0_ref, cw_ref, cb_ref, wa_ref, ba_ref, wi_ref, bi_ref, lam_ref,
                 qc_ref, kc_ref, lac_ref, v_ref, s0_ref,
                 h_ref, cn_ref, og_ref, s_ref):
    x = x_ref[...]
    xc = cb_ref[...] + cw_ref[LRU_CONV - 1:LRU_CONV, :] * x
    for j in range(LRU_CONV - 1):
        xc = xc + cw_ref[j:j + 1, :] * conv_ref[j]
    a, g = _lru_gates(xc, wa_ref, ba_ref, wi_ref, bi_ref, lam_ref)
    h_ref[...] = a * h0_ref[...] + g
    for j in range(LRU_CONV - 2):
        cn_ref[j] = conv_ref[j + 1]
    cn_ref[LRU_CONV - 2] = x

    alpha = jnp.exp(lac_ref[...])
    kc = kc_ref[...]
    qc = qc_ref[...] * (GLA_DK ** -0.5)
    for h in range(GLA_HEADS):
        rs = slice(h * GLA_DK, (h + 1) * GLA_DK)
        new = alpha[:, rs, :] * s0_ref[:, rs, :] + kc[:, rs, :] * v_ref[:, h:h + 1, :]
        s_ref[:, rs, :] = new
        og_ref[:, h:h + 1, :] = jnp.sum(qc[:, rs, :] * new, axis=1, keepdims=True)


def _step(xl, conv0, h0, cw, cb, wa, ba, wi, bi, lam, q_col, k_col, la_col, v3, s0):
    b = xl.shape[0]
    return pl.pallas_call(
        _step_kernel,
        out_shape=[jax.ShapeDtypeStruct((b, LRU_W), F32),
                   jax.ShapeDtypeStruct((LRU_CONV - 1, b, LRU_W), F32),
                   jax.ShapeDtypeStruct((b, GLA_HEADS, GLA_DV), F32),
                   jax.ShapeDtypeStruct((b, GLA_QK, GLA_DV), F32)],
        compiler_params=pltpu.CompilerParams(vmem_limit_bytes=VMEM_LIMIT_BYTES),
        name="sample_step",
    )(xl, conv0, h0, cw, cb, wa, ba, wi, bi, lam, q_col, k_col, la_col, v3, s0)


def _block_diag(w):
    nb, bi, bj = w.shape
    eye = jnp.eye(nb, dtype=w.dtype)
    return (eye[:, None, :, None] * w[:, :, None, :]).reshape(nb * bi, nb * bj)


def _row_tile(n):
    return min(ROW_TILE, n)


def kernel(x_prompt, x_sample, cache_k, cache_v, page_table, state_lru_h, state_lru_conv, state_gla,
           g_ffn1, w_ffn1_gate, w_ffn1_up, w_ffn1_down, g_mix, w_in, g_qnorm, g_knorm, sb_bias,
           conv_w, conv_b, lru_wa, lru_ba, lru_wi, lru_bi, lru_lambda, gla_w_alpha, gla_b_alpha,
           g_mix_out, w_out, g_ffn2, w_ffn2_gate, w_ffn2_up, w_ffn2_down):
    depth = w_in.shape[0]
    bp, seq, _ = x_prompt.shape
    bs, dec_seq, _ = x_sample.shape
    assert dec_seq == 1
    n_p = bp * seq
    n_phys = cache_k.shape[1]

    w_in_p = jnp.pad(w_in, ((0, 0), (0, 0), (0, D_IN_PAD - w_in.shape[2]))).astype(BF16)
    w_kvt = w_in[:, :, _OFF_K:_OFF_XL].transpose(0, 2, 1).astype(BF16)
    w_al = jnp.pad(gla_w_alpha, ((0, 0), (0, LANES - GLA_RANK), (0, 0))).astype(BF16)
    w_out_b = w_out.astype(BF16)
    wa_bd = jax.vmap(_block_diag)(lru_wa).astype(BF16)
    wi_bd = jax.vmap(_block_diag)(lru_wi).astype(BF16)
    gq = jnp.tile(g_qnorm, (1, SB_HEADS))[:, None, :]
    gk = jnp.tile(g_knorm, (1, SB_HEADS))[:, None, :]
    gk_col = jnp.tile(g_knorm, (1, SB_HEADS))[:, :, None]
    lane = jnp.arange(HEAD_MEAN_WIDTH)
    e64 = ((lane[:, None] // HEAD_DIM == lane[None, :] // HEAD_DIM).astype(F32) / HEAD_DIM).astype(BF16)
    vec = lambda t, l: t[l][None, :]

    cache_kt = cache_k.transpose(0, 1, 3, 4, 2).reshape(depth, n_phys, SB_W, PAGE_SIZE)
    cache_vt = cache_v.transpose(0, 1, 3, 4, 2).reshape(depth, n_phys, SB_W, PAGE_SIZE)

    xp = x_prompt.reshape(n_p, D_MODEL)
    xs = x_sample.reshape(bs, D_MODEL)
    tm_p = _row_tile(n_p)
    tm_s = _row_tile(bs)
    kt_all = jnp.zeros((depth, bp, SB_W, seq), F32)
    vt_all = jnp.zeros((depth, bp, SB_W, seq), F32)
    outs = {name: [] for name in ("ks", "vs", "hp", "hs", "cp", "cs", "sp", "ss")}

    for l in range(depth):
        lru_w = (conv_w[l], vec(conv_b, l), wa_bd[l], vec(lru_ba, l), wi_bd[l], vec(lru_bi, l),
                 vec(lru_lambda, l))
        inproj_w = (vec(g_mix, l), w_in_p[l], gq[l])
        inproj_w2 = (w_al[l], vec(gla_b_alpha, l), e64)

        ffn1_w = (vec(g_ffn1, l), w_ffn1_gate, w_ffn1_up, w_ffn1_down, l)
        ffn2_w = (vec(g_ffn2, l), w_ffn2_gate, w_ffn2_up, w_ffn2_down, l)
        xp = _ffn(xp, *ffn1_w, tm_p)
        q, kt_all, vt_all, xl, gate, qg, kg, vg, rg, la = _inproj(
            xp, *inproj_w, gk_col[l], *inproj_w2, tm_p, (kt_all, vt_all), l, w_kvt[l], seq)
        osb = _sb_prompt(q, kt_all, vt_all, sb_bias[l], l)
        olru, h_last, conv_new = _lru_prompt(xl, *lru_w, bp, seq)
        ogla, s_fin = _gla_prompt(qg, kg, vg, la, bp, seq)
        xp = _merge(xp, osb, olru, ogla, gate, rg, vec(g_mix_out, l), e64, w_out_b[l], tm_p)
        xp = _ffn(xp, *ffn2_w, tm_p)
        outs["hp"].append(h_last.reshape(bp, LRU_W))
        outs["cp"].append(conv_new)
        outs["sp"].append(s_fin.reshape(bp, GLA_HEADS, GLA_DK, GLA_DV))

        xs = _ffn(xs, *ffn1_w, tm_s)
        q, k_new, v_new, xl, gate, qg, kg, vg, rg, la = _inproj(xs, *inproj_w, gk[l], *inproj_w2, tm_s)
        bias_b = jnp.broadcast_to(sb_bias[l][:, None], (SB_HEADS, PAGE_SIZE))
        osb = _sb_decode(q.reshape(bs, SB_W, 1), bias_b, cache_kt, cache_vt, page_table, l)
        h_new, conv_new, ogla, s_new = _step(
            xl, state_lru_conv[l].transpose(1, 0, 2), state_lru_h[l], *lru_w,
            qg.reshape(bs, GLA_QK, 1), kg.reshape(bs, GLA_QK, 1), la.reshape(bs, GLA_QK, 1),
            vg.reshape(bs, GLA_HEADS, GLA_DV), state_gla[l].reshape(bs, GLA_QK, GLA_DV))
        xs = _merge(xs, osb.reshape(bs, SB_W), h_new, ogla.reshape(bs, GLA_W), gate, rg,
                    vec(g_mix_out, l), e64, w_out_b[l], tm_s)
        xs = _ffn(xs, *ffn2_w, tm_s)
        outs["ks"].append(k_new.reshape(bs, 1, SB_HEADS, HEAD_DIM))
        outs["vs"].append(v_new.reshape(bs, 1, SB_HEADS, HEAD_DIM))
        outs["hs"].append(h_new)
        outs["cs"].append(conv_new.transpose(1, 0, 2))
        outs["ss"].append(s_new.reshape(bs, GLA_HEADS, GLA_DK, GLA_DV))

    st = lambda name: jnp.stack(outs[name])
    return (xp.reshape(bp, seq, D_MODEL), xs.reshape(bs, 1, D_MODEL),
            kt_all.reshape(depth, bp, SB_HEADS, HEAD_DIM, seq).transpose(0, 1, 4, 2, 3),
            vt_all.reshape(depth, bp, SB_HEADS, HEAD_DIM, seq).transpose(0, 1, 4, 2, 3),
            st("ks"), st("vs"), st("hp"), st("hs"), st("cp"), st("cs"), st("sp"), st("ss"))
```

```python
import functools

import jax
import jax.numpy as jnp
from jax import lax
from jax.experimental import pallas as pl
from jax.experimental.pallas import tpu as pltpu

F32 = jnp.float32
BF16 = jnp.bfloat16

D_MODEL = 1024
HEAD_DIM = 64
SB_W = D_MODEL // 2
SB_HEADS = SB_W // HEAD_DIM
LRU_W = D_MODEL // 4
LRU_BLOCKS = LRU_W // HEAD_DIM
LRU_CONV = 4
LRU_C = 8.0
GLA_W = D_MODEL // 4
GLA_HEADS = GLA_W // HEAD_DIM
GLA_DV = HEAD_DIM
GLA_DK = HEAD_DIM // 2
GLA_QK = GLA_HEADS * GLA_DK
GLA_RANK = 16
GLA_TAU = 16.0
D_MIX = SB_W + LRU_W + GLA_W
D_FF = 2816
EPS = 1e-6
PAGE_SIZE = 128

LANES = 128
SUBLANES = 8
VMEM_LIMIT_BYTES = 56 * 1024 * 1024

_OFF_Q = 0
_OFF_K = _OFF_Q + SB_W
_OFF_V = _OFF_K + SB_W
_OFF_XL = _OFF_V + SB_W
_OFF_GATE = _OFF_XL + LRU_W
_OFF_QG = _OFF_GATE + LRU_W
_OFF_KG = _OFF_QG + GLA_QK
_OFF_VG = _OFF_KG + GLA_QK
_OFF_RG = _OFF_VG + GLA_W
_OFF_ALR = _OFF_RG + GLA_W
D_IN_PAD = _OFF_ALR + LANES

FF_CHUNK = 256
HEAD_MEAN_WIDTH = 256
ROW_TILE = 512
SB_BLOCK = 256
GLA_CHUNK = 128
GLA_CHUNKS_PER_STEP = 4
DECODE_PAGES_PER_STEP = 32


def _idiv(x, d):
    assert d & (d - 1) == 0
    return x >> (d.bit_length() - 1)


def _imod(x, d):
    assert d & (d - 1) == 0
    return x & (d - 1)


def _dot(a, b):
    return jnp.dot(a, b, preferred_element_type=F32)


def _dot_nt(a, b):
    return lax.dot_general(a, b, (((1,), (1,)), ((), ())), preferred_element_type=F32)


def _dot_tn(a, b):
    return lax.dot_general(a, b, (((0,), (0,)), ((), ())), preferred_element_type=F32)


def _split_dot(x, w, passes):
    hi = x.astype(BF16)
    acc = _dot(hi, w)
    rem = x - hi.astype(F32)
    for _ in range(passes - 1):
        lo = rem.astype(BF16)
        acc = acc + _dot(lo, w)
        rem = rem - lo.astype(F32)
    return acc


def _split_dot_left(w, x, passes):
    hi = x.astype(BF16)
    acc = _dot(w, hi)
    rem = x - hi.astype(F32)
    for _ in range(passes - 1):
        lo = rem.astype(BF16)
        acc = acc + _dot(w, lo)
        rem = rem - lo.astype(F32)
    return acc


def _rms(x, g):
    ms = jnp.mean(x * x, axis=-1, keepdims=True)
    return x * lax.rsqrt(ms + EPS) * g


def _softplus(z):
    return jnp.maximum(z, 0.0) + jnp.log1p(jnp.exp(-jnp.abs(z)))


def _log_sigmoid(z):
    return jnp.minimum(z, 0.0) - jnp.log1p(jnp.exp(-jnp.abs(z)))


def _gelu_tanh(x):
    return 0.5 * x * (1.0 + jnp.tanh(0.7978845608028654 * (x + 0.044715 * (x * x * x))))


def _head_rms_cols(x, e64, g):
    cw = e64.shape[0]
    cols = []
    for c in range(x.shape[1] // cw):
        xc = x[:, c * cw:(c + 1) * cw]
        ms = _split_dot(xc * xc, e64, 2)
        cols.append(xc * lax.rsqrt(ms + EPS) * g[:, c * cw:(c + 1) * cw])
    return cols


def _resident(shape):
    nd = len(shape)
    return pl.BlockSpec(shape, lambda *_: (0,) * nd, pipeline_mode=pl.Buffered(1))


def _layer_resident(shape, layer):
    nd = len(shape)
    return pl.BlockSpec((None,) + tuple(shape), lambda *_: (layer,) + (0,) * nd, pipeline_mode=pl.Buffered(1))


def _params(semantics):
    return pltpu.CompilerParams(dimension_semantics=semantics, vmem_limit_bytes=VMEM_LIMIT_BYTES)


def _ffn_kernel(x_ref, g_ref, wg_ref, wu_ref, wd_ref, o_ref, h_ref, acc_ref):
    x = x_ref[...]
    h_ref[...] = _rms(x, g_ref[...]).astype(BF16)
    acc_ref[...] = jnp.zeros_like(acc_ref)

    def body(c, carry):
        cols = pl.ds(pl.multiple_of(c * FF_CHUNK, FF_CHUNK), FF_CHUNK)
        h = h_ref[...]
        gate = _dot(h, wg_ref[:, cols].astype(BF16))
        up = _dot(h, wu_ref[:, cols].astype(BF16))
        act = (gate * jax.nn.sigmoid(gate) * up).astype(BF16)
        acc_ref[...] += _dot(act, wd_ref[cols, :].astype(BF16))
        return carry

    lax.fori_loop(0, D_FF // FF_CHUNK, body, 0, unroll=True)
    o_ref[...] = x + 0.5 * acc_ref[...]


def _ffn(x, g, wg, wu, wd, layer, tm):
    n = x.shape[0]
    return pl.pallas_call(
        _ffn_kernel,
        grid=(n // tm,),
        in_specs=[
            pl.BlockSpec((tm, D_MODEL), lambda i: (i, 0)),
            _resident((1, D_MODEL)),
            _layer_resident((D_MODEL, D_FF), layer),
            _layer_resident((D_MODEL, D_FF), layer),
            _layer_resident((D_FF, D_MODEL), layer),
        ],
        out_specs=pl.BlockSpec((tm, D_MODEL), lambda i: (i, 0)),
        out_shape=jax.ShapeDtypeStruct((n, D_MODEL), F32),
        scratch_shapes=[pltpu.VMEM((tm, D_MODEL), BF16), pltpu.VMEM((tm, D_MODEL), F32)],
        compiler_params=_params(("parallel",)),
        name="ffn",
    )(x, g, wg, wu, wd)


def _inproj_kernel(*refs, transposed_kv):
    if transposed_kv:
        (x_ref, g_ref, w_ref, gq_ref, gk_ref, wal_ref, bal_ref, e64_ref, wkvt_ref,
         q_ref, k_ref, v_ref, xl_ref, gate_ref, qg_ref, kg_ref, vg_ref, rg_ref, la_ref, h_ref) = refs
    else:
        (x_ref, g_ref, w_ref, gq_ref, gk_ref, wal_ref, bal_ref, e64_ref,
         q_ref, k_ref, v_ref, xl_ref, gate_ref, qg_ref, kg_ref, vg_ref, rg_ref, la_ref, h_ref) = refs
    h_ref[...] = _rms(x_ref[...], g_ref[...]).astype(BF16)

    def proj(lo, width):
        return _dot(h_ref[...], w_ref[:, lo:lo + width])

    e64 = e64_ref[...]
    q_cols = _head_rms_cols(proj(_OFF_Q, SB_W), e64, gq_ref[...])
    cw = e64.shape[0]
    for c, col in enumerate(q_cols):
        q_ref[:, c * cw:(c + 1) * cw] = col * (HEAD_DIM ** -0.5)
    if transposed_kv:
        rows = h_ref.shape[0]
        k_t = _dot_nt(wkvt_ref[:SB_W, :], h_ref[...]).reshape(SB_HEADS, HEAD_DIM, rows)
        ms = jnp.mean(k_t * k_t, axis=1, keepdims=True)
        k_ref[...] = (k_t * lax.rsqrt(ms + EPS)).reshape(SB_W, rows) * gk_ref[...]
        v_ref[...] = _dot_nt(wkvt_ref[SB_W:, :], h_ref[...])
    else:
        k_cols = _head_rms_cols(proj(_OFF_K, SB_W), e64, gk_ref[...])
        for c, col in enumerate(k_cols):
            k_ref[:, c * cw:(c + 1) * cw] = col
        v_ref[...] = proj(_OFF_V, SB_W)
    xl_ref[...] = proj(_OFF_XL, LRU_W)
    gate_ref[...] = _gelu_tanh(proj(_OFF_GATE, LRU_W))
    qg_ref[...] = proj(_OFF_QG, GLA_QK)
    kg_ref[...] = proj(_OFF_KG, GLA_QK)
    vg_ref[...] = proj(_OFF_VG, GLA_W)
    r = proj(_OFF_RG, GLA_W)
    rg_ref[...] = r * jax.nn.sigmoid(r)
    a_lr = proj(_OFF_ALR, LANES)
    xa = _dot(a_lr.astype(BF16), wal_ref[...]) + bal_ref[...]
    la_ref[...] = _log_sigmoid(xa) * (1.0 / GLA_TAU)


def _inproj(x, g, w_in, gq, gk, wal, bal, e64, tm, kv_slabs=None, layer=None, w_kvt=None, seq=None):
    n = x.shape[0]
    row = lambda w: pl.BlockSpec((tm, w), lambda i: (i, 0))
    in_specs = [
        row(D_MODEL),
        _resident((1, D_MODEL)),
        _resident((D_MODEL, D_IN_PAD)),
        _resident((1, SB_W)),
        _resident(gk.shape),
        _resident((LANES, LANES)),
        _resident((1, LANES)),
        _resident(e64.shape),
    ]
    args = [x, g, w_in, gq, gk, wal, bal, e64]
    small = lambda w: jax.ShapeDtypeStruct((n, w), F32)
    transposed_kv = kv_slabs is not None
    if not transposed_kv:
        kv_specs = [row(SB_W), row(SB_W)]
        kv_shapes = [small(SB_W), small(SB_W)]
        aliases = {}
        kernel = functools.partial(_inproj_kernel, transposed_kv=False)
    else:
        k_all, v_all = kv_slabs
        tiles = seq // tm
        slab = pl.BlockSpec((None, None, SB_W, tm), lambda i: (layer, i // tiles, 0, i % tiles))
        kv_specs = [slab, slab]
        kv_shapes = [jax.ShapeDtypeStruct(k_all.shape, F32), jax.ShapeDtypeStruct(v_all.shape, F32)]
        in_specs += [_resident((2 * SB_W, D_MODEL)),
                     pl.BlockSpec(memory_space=pl.ANY), pl.BlockSpec(memory_space=pl.ANY)]
        args += [w_kvt, k_all, v_all]
        aliases = {len(args) - 2: 1, len(args) - 1: 2}

        def kernel(*refs):
            _inproj_kernel(*refs[:9], *refs[11:], transposed_kv=True)

    out_specs = [row(SB_W)] + kv_specs + [row(LRU_W), row(LRU_W), row(GLA_QK), row(GLA_QK),
                                          row(GLA_W), row(GLA_W), row(GLA_QK)]
    out_shape = [small(SB_W)] + kv_shapes + [small(LRU_W), small(LRU_W), small(GLA_QK), small(GLA_QK),
                                             small(GLA_W), small(GLA_W), small(GLA_QK)]
    return pl.pallas_call(
        kernel,
        grid=(n // tm,),
        in_specs=in_specs,
        out_specs=out_specs,
        out_shape=out_shape,
        input_output_aliases=aliases,
        scratch_shapes=[pltpu.VMEM((tm, D_MODEL), BF16)],
        compiler_params=_params(("parallel",)),
        name="inproj",
    )(*args)


def _merge_kernel(x_ref, osb_ref, olru_ref, ogla_ref, gate_ref, rg_ref, gmo_ref, e64_ref, wout_ref, o_ref):
    e64 = e64_ref[...]
    gmo = gmo_ref[...]
    sb = _head_rms_cols(osb_ref[...], e64, gmo[:, :SB_W])
    lru = _head_rms_cols(olru_ref[...], e64, gmo[:, SB_W:SB_W + LRU_W])
    gla = _head_rms_cols(ogla_ref[...], e64, gmo[:, SB_W + LRU_W:])
    gate = gate_ref[...]
    rg = rg_ref[...]
    cw = e64.shape[0]
    lru = [col * gate[:, c * cw:(c + 1) * cw] for c, col in enumerate(lru)]
    gla = [col * rg[:, c * cw:(c + 1) * cw] for c, col in enumerate(gla)]
    acc = x_ref[...]
    for c, col in enumerate(sb + lru + gla):
        acc = acc + _dot(col.astype(BF16), wout_ref[c * cw:(c + 1) * cw, :])
    o_ref[...] = acc


def _merge(x, osb, olru, ogla, gate, rg, gmo, e64, wout, tm):
    n = x.shape[0]
    row = lambda w: pl.BlockSpec((tm, w), lambda i: (i, 0))
    return pl.pallas_call(
        _merge_kernel,
        grid=(n // tm,),
        in_specs=[row(D_MODEL), row(SB_W), row(LRU_W), row(GLA_W), row(LRU_W), row(GLA_W),
                  _resident((1, D_MIX)), _resident(e64.shape), _resident((D_MIX, D_MODEL))],
        out_specs=row(D_MODEL),
        out_shape=jax.ShapeDtypeStruct((n, D_MODEL), F32),
        compiler_params=_params(("parallel",)),
        name="merge",
    )(x, osb, olru, ogla, gate, rg, gmo, e64, wout)


_LOG2_E = 1.4426950408889634


def _softplus_plain(z):
    return jnp.maximum(z, 0.0) + jnp.log(1.0 + jnp.exp2(jnp.abs(z) * (-_LOG2_E)))


def _sb_tile_stages(q_h, kt_b, vt_h, bias, upper, tail, mask):
    st = {}

    def logits():
        st["z"] = _dot(q_h, kt_b) + bias

    def log_terms():
        z = st["z"]
        sp = _softplus_plain(z)
        st["lb"] = z - sp
        if mask is not None:
            sp = jnp.where(mask, sp, 0.0)
        st["edge"] = sp[:, 0:1]
        st["sp"] = sp.astype(BF16)

    def suffix_sums():
        st["within"] = _dot(st["sp"], upper)

    def weights():
        within = st["within"]
        a = jnp.exp(st["lb"] - within - tail)
        if mask is not None:
            a = jnp.where(mask, a, 0.0)
        st["a"] = a.astype(BF16)
        st["tail"] = tail + within[:, 0:1] + st["edge"]

    def values():
        return _dot_nt(st["a"], vt_h), st["tail"]

    return [logits, log_terms, suffix_sums, weights, values]


def _run_skewed(pipelines):
    depth = len(pipelines[0])
    results = [None] * len(pipelines)
    for step in range(len(pipelines) + depth - 1):
        for p, stages in enumerate(pipelines):
            s = step - p
            if 0 <= s < depth:
                results[p] = stages[s]()
    return results


def _sb_prompt_kernel(bias_ref, q_ref, kt_ref, vt_ref, o_ref, kb_ref, vh_ref, *, blk):
    pairs = SB_W // LANES
    i = pl.program_id(1)
    first = lax.broadcasted_iota(jnp.int32, (1, LANES), 1) < HEAD_DIM

    @pl.when(i == 0)
    def _stage():
        kb_ref[...] = kt_ref[...].astype(BF16)
        first_rows = lax.broadcasted_iota(jnp.int32, (LANES, vt_ref.shape[1]), 0) < HEAD_DIM
        for p in range(pairs):
            v = vt_ref[p * LANES:(p + 1) * LANES, :]
            vh_ref[2 * p] = jnp.where(first_rows, v, 0.0).astype(BF16)
            vh_ref[2 * p + 1] = jnp.where(first_rows, 0.0, v).astype(BF16)

    q_heads = []
    biases = []
    for p in range(pairs):
        q = q_ref[:, p * LANES:(p + 1) * LANES]
        q_heads += [jnp.where(first, q, 0.0).astype(BF16), jnp.where(first, 0.0, q).astype(BF16)]
        biases += [bias_ref[2 * p], bias_ref[2 * p + 1]]
    r_i = lax.broadcasted_iota(jnp.int32, (blk, blk), 0)
    c_i = lax.broadcasted_iota(jnp.int32, (blk, blk), 1)
    upper = jnp.where(r_i > c_i, 1.0, 0.0).astype(BF16)
    causal = c_i < r_i

    def tile(j, carries, accs, mask):
        cols = pl.ds(pl.multiple_of(j * blk, blk), blk)
        pipelines = []
        for h in range(2 * pairs):
            kt_b = kb_ref[(h // 2) * LANES:(h // 2 + 1) * LANES, cols]
            pipelines.append(_sb_tile_stages(q_heads[h], kt_b, vh_ref[h, :, cols], biases[h], upper,
                                             carries[h], mask))
        results = _run_skewed(pipelines)
        new_accs = list(accs)
        for h, (out, _) in enumerate(results):
            new_accs[h // 2] = new_accs[h // 2] + out
        return tuple(c for _, c in results), tuple(new_accs)

    zero_c = jnp.zeros((blk, 1), F32)
    zero_a = jnp.zeros((blk, LANES), F32)
    state = tile(i, (zero_c,) * (2 * pairs), (zero_a,) * pairs, causal)

    def body(it, state):
        return tile(i - 1 - it, state[0], state[1], None)

    _, accs = lax.fori_loop(0, i, body, state)
    for p in range(pairs):
        o_ref[:, p * LANES:(p + 1) * LANES] = accs[p]


def _sb_prompt(q, kt_all, vt_all, bias, layer):
    _, batch, _, seq = kt_all.shape
    blk = min(SB_BLOCK, seq)
    nq = seq // blk
    kv_spec = pl.BlockSpec((None, None, SB_W, seq), lambda b, i, *_: (layer, b, 0, 0))
    return pl.pallas_call(
        functools.partial(_sb_prompt_kernel, blk=blk),
        grid_spec=pltpu.PrefetchScalarGridSpec(
            num_scalar_prefetch=1,
            grid=(batch, nq),
            in_specs=[pl.BlockSpec((blk, SB_W), lambda b, i, *_: (b * nq + i, 0)), kv_spec, kv_spec],
            out_specs=pl.BlockSpec((blk, SB_W), lambda b, i, *_: (b * nq + i, 0)),
            scratch_shapes=[pltpu.VMEM((SB_W, seq), BF16), pltpu.VMEM((SB_HEADS, LANES, seq), BF16)],
        ),
        out_shape=jax.ShapeDtypeStruct((batch * seq, SB_W), F32),
        compiler_params=_params(("parallel", "arbitrary")),
        name="sb_prompt",
    )(bias, q, kt_all, vt_all)


def _sb_decode_kernel(pt_ref, q_ref, bias_ref, *refs, n_pages, group):
    k_refs = refs[:group]
    v_refs = refs[group:2 * group]
    o_ref = refs[2 * group]
    z_ref, a_ref, acc_ref = refs[2 * group + 1:]
    s = pl.program_id(1)
    steps = n_pages // group
    rows = n_pages * SB_HEADS

    @pl.when(s < steps)
    def _scores():
        q_col = q_ref[...]
        for g in range(group):
            prod = (k_refs[g][...] * q_col).reshape(SB_HEADS, HEAD_DIM, PAGE_SIZE)
            z = jnp.sum(prod, axis=1) + bias_ref[...]
            row0 = pl.multiple_of((s * group + g) * SB_HEADS, SB_HEADS)
            z_ref[pl.ds(row0, SB_HEADS), :] = z

    @pl.when(s == steps - 1)
    def _weights():
        z = z_ref[...]
        sp = _softplus(z)
        l1m = -sp
        r_i = lax.broadcasted_iota(jnp.int32, (PAGE_SIZE, PAGE_SIZE), 0)
        c_i = lax.broadcasted_iota(jnp.int32, (PAGE_SIZE, PAGE_SIZE), 1)
        upper = (r_i > c_i).astype(BF16)
        ones = jnp.ones((PAGE_SIZE, PAGE_SIZE), BF16)
        within = _split_dot(l1m, upper, 3)
        page_total = _split_dot(l1m, ones, 3)
        pr = lax.broadcasted_iota(jnp.int32, (rows, rows), 0)
        pc = lax.broadcasted_iota(jnp.int32, (rows, rows), 1)
        same_head = _imod(pc, SB_HEADS) == _imod(pr, SB_HEADS)
        later_page = jnp.where(_idiv(pc, SB_HEADS) > _idiv(pr, SB_HEADS),
                               jnp.where(same_head, 1.0, 0.0), 0.0).astype(BF16)
        later = _split_dot_left(later_page, page_total, 3)
        a_ref[...] = jnp.exp(z - sp + within + later)
        acc_ref[...] = jnp.zeros_like(acc_ref)

    @pl.when(s >= steps)
    def _values():
        acc = acc_ref[...]
        for g in range(group):
            row0 = pl.multiple_of(((s - steps) * group + g) * SB_HEADS, SB_HEADS)
            a_g = a_ref[pl.ds(row0, SB_HEADS), :]
            a_rows = jnp.concatenate(
                [jnp.broadcast_to(a_g[h:h + 1, :], (HEAD_DIM, PAGE_SIZE)) for h in range(SB_HEADS)], axis=0)
            acc = acc + v_refs[g][...] * a_rows
        acc_ref[...] = acc

    @pl.when(s == 2 * steps - 1)
    def _emit():
        o_ref[...] = jnp.sum(acc_ref[...], axis=1, keepdims=True)


def _sb_decode(q_col, bias_b, cache_kt, cache_vt, page_table, layer):
    b, n_pages = page_table.shape
    group = min(DECODE_PAGES_PER_STEP, n_pages)
    steps = n_pages // group

    def k_map(g):
        return lambda bi, s, pt: (layer, pt[bi, jnp.minimum(s, steps - 1) * group + g], 0, 0)

    def v_map(g):
        def index(bi, s, pt):
            in_values = s >= steps
            row = jnp.where(in_values, bi, jnp.maximum(bi - 1, 0))
            col = jnp.where(in_values, s - steps, steps - 1) * group + g
            return (layer, pt[row, col], 0, 0)
        return index

    page = lambda m: pl.BlockSpec((None, None, SB_W, PAGE_SIZE), m)
    in_specs = [pl.BlockSpec((None, SB_W, 1), lambda bi, s, pt: (bi, 0, 0)),
                pl.BlockSpec((SB_HEADS, PAGE_SIZE), lambda bi, s, pt: (0, 0))]
    in_specs += [page(k_map(g)) for g in range(group)]
    in_specs += [page(v_map(g)) for g in range(group)]
    rows = n_pages * SB_HEADS
    return pl.pallas_call(
        functools.partial(_sb_decode_kernel, n_pages=n_pages, group=group),
        grid_spec=pltpu.PrefetchScalarGridSpec(
            num_scalar_prefetch=1,
            grid=(b, 2 * steps),
            in_specs=in_specs,
            out_specs=pl.BlockSpec((None, SB_W, 1), lambda bi, s, pt: (bi, 0, 0)),
            scratch_shapes=[pltpu.VMEM((rows, PAGE_SIZE), F32),
                            pltpu.VMEM((rows, PAGE_SIZE), F32),
                            pltpu.VMEM((SB_W, PAGE_SIZE), F32)],
        ),
        out_shape=jax.ShapeDtypeStruct((b, SB_W, 1), F32),
        compiler_params=_params(("parallel", "arbitrary")),
        name="sb_decode",
    )(page_table, q_col, bias_b, *([cache_kt] * group), *([cache_vt] * group))


def _lru_gates(xc, wa_ref, ba_ref, wi_ref, bi_ref, lam_ref):
    xb = xc.astype(BF16)
    r = jax.nn.sigmoid(_dot(xb, wa_ref[...]) + ba_ref[...])
    i = jax.nn.sigmoid(_dot(xb, wi_ref[...]) + bi_ref[...])
    log_a = -LRU_C * r * _softplus(-lam_ref[...])
    a = jnp.exp(log_a)
    t = jnp.tanh(log_a)
    one_minus_a2 = -2.0 * t / (1.0 - t)
    return a, jnp.sqrt(one_minus_a2) * (i * xc)


def _lru_prompt_kernel(x_ref, cw_ref, cb_ref, wa_ref, ba_ref, wi_ref, bi_ref, lam_ref,
                       o_ref, hl_ref, cn_ref, *, seq):
    x = x_ref[...]
    rows = lax.broadcasted_iota(jnp.int32, (seq, LRU_W), 0)
    xc = cb_ref[...] + cw_ref[LRU_CONV - 1:LRU_CONV, :] * x
    for j in range(1, LRU_CONV):
        shifted = jnp.where(rows >= j, pltpu.roll(x, j, 0), 0.0)
        xc = xc + cw_ref[LRU_CONV - 1 - j:LRU_CONV - j, :] * shifted
    a, g = _lru_gates(xc, wa_ref, ba_ref, wi_ref, bi_ref, lam_ref)
    step = 1
    while step < seq:
        valid = rows >= step
        a_prev = pltpu.roll(a, step, 0)
        g_prev = pltpu.roll(g, step, 0)
        g = jnp.where(valid, a * g_prev + g, g)
        a = jnp.where(valid, a * a_prev, a)
        step *= 2
    o_ref[...] = g
    hl_ref[...] = g[seq - 1:seq, :]
    cn_ref[...] = x[seq - (LRU_CONV - 1):, :]


def _lru_prompt(xl, cw, cb, wa, ba, wi, bi, lam, batch, seq):
    vec = _resident((1, LRU_W))
    mat = _resident((LRU_W, LRU_W))
    return pl.pallas_call(
        functools.partial(_lru_prompt_kernel, seq=seq),
        grid=(batch,),
        in_specs=[pl.BlockSpec((seq, LRU_W), lambda b: (b, 0)), _resident((LRU_CONV, LRU_W)), vec,
                  mat, vec, mat, vec, vec],
        out_specs=[pl.BlockSpec((seq, LRU_W), lambda b: (b, 0)),
                   pl.BlockSpec((None, 1, LRU_W), lambda b: (b, 0, 0)),
                   pl.BlockSpec((None, LRU_CONV - 1, LRU_W), lambda b: (b, 0, 0))],
        out_shape=[jax.ShapeDtypeStruct((batch * seq, LRU_W), F32),
                   jax.ShapeDtypeStruct((batch, 1, LRU_W), F32),
                   jax.ShapeDtypeStruct((batch, LRU_CONV - 1, LRU_W), F32)],
        compiler_params=_params(("parallel",)),
        name="lru_prompt",
    )(xl, cw, cb, wa, ba, wi, bi, lam)


def _block_mid_rows(bc, level):
    n_rows, width = bc.shape
    half = level // 2
    if level >= 2 * SUBLANES:
        return jnp.concatenate(
            [jnp.broadcast_to(bc[b * level + half - 1:b * level + half, :], (level, width))
             for b in range(n_rows // level)], axis=0)
    groups = n_rows // SUBLANES
    bc3 = bc.reshape(groups, SUBLANES, width)
    sub = lax.broadcasted_iota(jnp.int32, (groups, SUBLANES, width), 1)
    mid = None
    for b in range(SUBLANES // level):
        piece = jnp.broadcast_to(bc3[:, b * level + half - 1:b * level + half, :], bc3.shape)
        mid = piece if mid is None else jnp.where(_idiv(sub, level) == b, piece, mid)
    return mid.reshape(n_rows, width)


def _gla_chunk(qs, k, v, la, state):
    chunk = qs.shape[0]
    r_i = lax.broadcasted_iota(jnp.int32, (chunk, chunk), 0)
    c_i = lax.broadcasted_iota(jnp.int32, (chunk, chunk), 1)
    lower = (c_i <= r_i).astype(BF16)
    bc = _split_dot_left(lower, la, 3)
    rows = lax.broadcasted_iota(jnp.int32, (chunk, GLA_QK), 0)
    qk_head = _idiv(lax.broadcasted_iota(jnp.int32, (1, GLA_QK), 1), GLA_DK)
    v_head = _idiv(lax.broadcasted_iota(jnp.int32, (1, GLA_W), 1), GLA_DV)
    own = (_idiv(lax.broadcasted_iota(jnp.int32, (GLA_QK, GLA_W), 0), GLA_DK)
           == _idiv(lax.broadcasted_iota(jnp.int32, (GLA_QK, GLA_W), 1), GLA_DV))

    out = _dot((qs * jnp.exp(bc)).astype(BF16), state.astype(BF16))

    scores = [jnp.zeros((chunk, chunk), F32) for _ in range(GLA_HEADS)]
    level = 2
    while level <= chunk:
        half = level // 2
        mid = _block_mid_rows(bc, level)
        second = (rows & (level - 1)) >= half
        q_dec = jnp.where(second, qs * jnp.exp(bc - mid), 0.0)
        k_dec = jnp.where(second, 0.0, k * jnp.exp(mid - bc)).astype(BF16)
        same_block = _idiv(r_i, level) == _idiv(c_i, level)
        for h in range(GLA_HEADS):
            sc = _dot_nt(jnp.where(qk_head == h, q_dec, 0.0).astype(BF16), k_dec)
            if level < chunk:
                sc = jnp.where(same_block, sc, 0.0)
            scores[h] = scores[h] + sc
        level *= 2
    for h in range(GLA_HEADS):
        out = out + _dot(scores[h].astype(BF16), jnp.where(v_head == h, v, 0.0).astype(BF16))

    expand = jnp.where(own, 1.0, 0.0).astype(BF16)
    out = out + _split_dot(qs * k, expand, 2) * v

    last = bc[chunk - 1:chunk, :]
    kv = _dot_tn((k * jnp.exp(last - bc)).astype(BF16), v.astype(BF16))
    dec_col = jnp.transpose(jnp.broadcast_to(jnp.exp(last), (GLA_QK, GLA_QK)))
    dec = jnp.concatenate([dec_col] * (GLA_W // GLA_QK), axis=1)
    return out, dec * state + jnp.where(own, kv, 0.0)


def _gla_prompt_kernel(q_ref, k_ref, v_ref, la_ref, o_ref, sfin_ref, s_ref, *, chunk, n_steps):
    ci = pl.program_id(1)

    @pl.when(ci == 0)
    def _init():
        s_ref[...] = jnp.zeros_like(s_ref)

    state = s_ref[...]
    for u in range(q_ref.shape[0] // chunk):
        rows = slice(u * chunk, (u + 1) * chunk)
        out, state = _gla_chunk(q_ref[rows, :] * (GLA_DK ** -0.5), k_ref[rows, :], v_ref[rows, :],
                                la_ref[rows, :], state)
        o_ref[rows, :] = out
    s_ref[...] = state

    @pl.when(ci == n_steps - 1)
    def _emit():
        for h in range(GLA_HEADS):
            sfin_ref[h * GLA_DK:(h + 1) * GLA_DK, :] = (
                state[h * GLA_DK:(h + 1) * GLA_DK, h * GLA_DV:(h + 1) * GLA_DV])


def _gla_prompt(qg, kg, vg, la, batch, seq):
    chunk = min(GLA_CHUNK, seq)
    block = chunk * GLA_CHUNKS_PER_STEP if seq % (chunk * GLA_CHUNKS_PER_STEP) == 0 else chunk
    n_steps = seq // block
    row = lambda w: pl.BlockSpec((block, w), lambda b, c: (b * n_steps + c, 0))
    return pl.pallas_call(
        functools.partial(_gla_prompt_kernel, chunk=chunk, n_steps=n_steps),
        grid=(batch, n_steps),
        in_specs=[row(GLA_QK), row(GLA_QK), row(GLA_W), row(GLA_QK)],
        out_specs=[row(GLA_W), pl.BlockSpec((None, GLA_QK, GLA_DV), lambda b, c: (b, 0, 0))],
        out_shape=[jax.ShapeDtypeStruct((batch * seq, GLA_W), F32),
                   jax.ShapeDtypeStruct((batch, GLA_QK, GLA_DV), F32)],
        scratch_shapes=[pltpu.VMEM((GLA_QK, GLA_W), F32)],
        compiler_params=_params(("parallel", "arbitrary")),
        name="gla_prompt",
    )(qg, kg, vg, la)


def _step_kernel(x_ref, conv_ref, h0_ref, cw_ref, cb_ref, wa_ref, ba_ref, wi_ref, bi_ref, lam_ref,
                 qc_ref, kc_ref, lac_ref, v_ref, s0_ref,
                 h_ref, cn_ref, og_ref, s_ref):
    x = x_ref[...]
    xc = cb_ref[...] + cw_ref[LRU_CONV - 1:LRU_CONV, :] * x
    for j in range(LRU_CONV - 1):
        xc = xc + cw_ref[j:j + 1, :] * conv_ref[j]
    a, g = _lru_gates(xc, wa_ref, ba_ref, wi_ref, bi_ref, lam_ref)
    h_ref[...] = a * h0_ref[...] + g
    for j in range(LRU_CONV - 2):
        cn_ref[j] = conv_ref[j + 1]
    cn_ref[LRU_CONV - 2] = x

    alpha = jnp.exp(lac_ref[...])
    kc = kc_ref[...]
    qc = qc_ref[...] * (GLA_DK ** -0.5)
    for h in range(GLA_HEADS):
        rs = slice(h * GLA_DK, (h + 1) * GLA_DK)
        new = alpha[:, rs, :] * s0_ref[:, rs, :] + kc[:, rs, :] * v_ref[:, h:h + 1, :]
        s_ref[:, rs, :] = new
        og_ref[:, h:h + 1, :] = jnp.sum(qc[:, rs, :] * new, axis=1, keepdims=True)


def _step(xl, conv0, h0, cw, cb, wa, ba, wi, bi, lam, q_col, k_col, la_col, v3, s0):
    b = xl.shape[0]
    return pl.pallas_call(
        _step_kernel,
        out_shape=[jax.ShapeDtypeStruct((b, LRU_W), F32),
                   jax.ShapeDtypeStruct((LRU_CONV - 1, b, LRU_W), F32),
                   jax.ShapeDtypeStruct((b, GLA_HEADS, GLA_DV), F32),
                   jax.ShapeDtypeStruct((b, GLA_QK, GLA_DV), F32)],
        compiler_params=pltpu.CompilerParams(vmem_limit_bytes=VMEM_LIMIT_BYTES),
        name="sample_step",
    )(xl, conv0, h0, cw, cb, wa, ba, wi, bi, lam, q_col, k_col, la_col, v3, s0)


def _block_diag(w):
    nb, bi, bj = w.shape
    eye = jnp.eye(nb, dtype=w.dtype)
    return (eye[:, None, :, None] * w[:, :, None, :]).reshape(nb * bi, nb * bj)


def _row_tile(n):
    return min(ROW_TILE, n)


def kernel(x_prompt, x_sample, cache_k, cache_v, page_table, state_lru_h, state_lru_conv, state_gla,
           g_ffn1, w_ffn1_gate, w_ffn1_up, w_ffn1_down, g_mix, w_in, g_qnorm, g_knorm, sb_bias,
           conv_w, conv_b, lru_wa, lru_ba, lru_wi, lru_bi, lru_lambda, gla_w_alpha, gla_b_alpha,
           g_mix_out, w_out, g_ffn2, w_ffn2_gate, w_ffn2_up, w_ffn2_down):
    depth = w_in.shape[0]
    bp, seq, _ = x_prompt.shape
    bs, dec_seq, _ = x_sample.shape
    assert dec_seq == 1
    n_p = bp * seq
    n_phys = cache_k.shape[1]

    w_in_p = jnp.pad(w_in, ((0, 0), (0, 0), (0, D_IN_PAD - w_in.shape[2]))).astype(BF16)
    w_kvt = w_in[:, :, _OFF_K:_OFF_XL].transpose(0, 2, 1).astype(BF16)
    w_al = jnp.pad(gla_w_alpha, ((0, 0), (0, LANES - GLA_RANK), (0, 0))).astype(BF16)
    w_out_b = w_out.astype(BF16)
    wa_bd = jax.vmap(_block_diag)(lru_wa).astype(BF16)
    wi_bd = jax.vmap(_block_diag)(lru_wi).astype(BF16)
    gq = jnp.tile(g_qnorm, (1, SB_HEADS))[:, None, :]
    gk = jnp.tile(g_knorm, (1, SB_HEADS))[:, None, :]
    gk_col = jnp.tile(g_knorm, (1, SB_HEADS))[:, :, None]
    lane = jnp.arange(HEAD_MEAN_WIDTH)
    e64 = ((lane[:, None] // HEAD_DIM == lane[None, :] // HEAD_DIM).astype(F32) / HEAD_DIM).astype(BF16)
    vec = lambda t, l: t[l][None, :]

    cache_kt = cache_k.transpose(0, 1, 3, 4, 2).reshape(depth, n_phys, SB_W, PAGE_SIZE)
    cache_vt = cache_v.transpose(0, 1, 3, 4, 2).reshape(depth, n_phys, SB_W, PAGE_SIZE)

    xp = x_prompt.reshape(n_p, D_MODEL)
    xs = x_sample.reshape(bs, D_MODEL)
    tm_p = _row_tile(n_p)
    tm_s = _row_tile(bs)
    kt_all = jnp.zeros((depth, bp, SB_W, seq), F32)
    vt_all = jnp.zeros((depth, bp, SB_W, seq), F32)
    outs = {name: [] for name in ("ks", "vs", "hp", "hs", "cp", "cs", "sp", "ss")}

    for l in range(depth):
        lru_w = (conv_w[l], vec(conv_b, l), wa_bd[l], vec(lru_ba, l), wi_bd[l], vec(lru_bi, l),
                 vec(lru_lambda, l))
        inproj_w = (vec(g_mix, l), w_in_p[l], gq[l])
        inproj_w2 = (w_al[l], vec(gla_b_alpha, l), e64)

        ffn1_w = (vec(g_ffn1, l), w_ffn1_gate, w_ffn1_up, w_ffn1_down, l)
        ffn2_w = (vec(g_ffn2, l), w_ffn2_gate, w_ffn2_up, w_ffn2_down, l)
        xp = _ffn(xp, *ffn1_w, tm_p)
        q, kt_all, vt_all, xl, gate, qg, kg, vg, rg, la = _inproj(
            xp, *inproj_w, gk_col[l], *inproj_w2, tm_p, (kt_all, vt_all), l, w_kvt[l], seq)
        osb = _sb_prompt(q, kt_all, vt_all, sb_bias[l], l)
        olru, h_last, conv_new = _lru_prompt(xl, *lru_w, bp, seq)
        ogla, s_fin = _gla_prompt(qg, kg, vg, la, bp, seq)
        xp = _merge(xp, osb, olru, ogla, gate, rg, vec(g_mix_out, l), e64, w_out_b[l], tm_p)
        xp = _ffn(xp, *ffn2_w, tm_p)
        outs["hp"].append(h_last.reshape(bp, LRU_W))
        outs["cp"].append(conv_new)
        outs["sp"].append(s_fin.reshape(bp, GLA_HEADS, GLA_DK, GLA_DV))

        xs = _ffn(xs, *ffn1_w, tm_s)
        q, k_new, v_new, xl, gate, qg, kg, vg, rg, la = _inproj(xs, *inproj_w, gk[l], *inproj_w2, tm_s)
        bias_b = jnp.broadcast_to(sb_bias[l][:, None], (SB_HEADS, PAGE_SIZE))
        osb = _sb_decode(q.reshape(bs, SB_W, 1), bias_b, cache_kt, cache_vt, page_table, l)
        h_new, conv_new, ogla, s_new = _step(
            xl, state_lru_conv[l].transpose(1, 0, 2), state_lru_h[l], *lru_w,
            qg.reshape(bs, GLA_QK, 1), kg.reshape(bs, GLA_QK, 1), la.reshape(bs, GLA_QK, 1),
            vg.reshape(bs, GLA_HEADS, GLA_DV), state_gla[l].reshape(bs, GLA_QK, GLA_DV))
        xs = _merge(xs, osb.reshape(bs, SB_W), h_new, ogla.reshape(bs, GLA_W), gate, rg,
                    vec(g_mix_out, l), e64, w_out_b[l], tm_s)
        xs = _ffn(xs, *ffn2_w, tm_s)
        outs["ks"].append(k_new.reshape(bs, 1, SB_HEADS, HEAD_DIM))
        outs["vs"].append(v_new.reshape(bs, 1, SB_HEADS, HEAD_DIM))
        outs["hs"].append(h_new)
        outs["cs"].append(conv_new.transpose(1, 0, 2))
        outs["ss"].append(s_new.reshape(bs, GLA_HEADS, GLA_DK, GLA_DV))

    st = lambda name: jnp.stack(outs[name])
    return (xp.reshape(bp, seq, D_MODEL), xs.reshape(bs, 1, D_MODEL),
            kt_all.reshape(depth, bp, SB_HEADS, HEAD_DIM, seq).transpose(0, 1, 4, 2, 3),
            vt_all.reshape(depth, bp, SB_HEADS, HEAD_DIM, seq).transpose(0, 1, 4, 2, 3),
            st("ks"), st("vs"), st("hp"), st("hs"), st("cp"), st("cs"), st("sp"), st("ss"))
```

```python
import functools

import jax
import jax.numpy as jnp
from jax import lax
from jax.experimental import pallas as pl
from jax.experimental.pallas import tpu as pltpu

F32 = jnp.float32
BF16 = jnp.bfloat16

D_MODEL = 1024
HEAD_DIM = 64
SB_W = D_MODEL // 2
SB_HEADS = SB_W // HEAD_DIM
LRU_W = D_MODEL // 4
LRU_BLOCKS = LRU_W // HEAD_DIM
LRU_CONV = 4
LRU_C = 8.0
GLA_W = D_MODEL // 4
GLA_HEADS = GLA_W // HEAD_DIM
GLA_DV = HEAD_DIM
GLA_DK = HEAD_DIM // 2
GLA_QK = GLA_HEADS * GLA_DK
GLA_RANK = 16
GLA_TAU = 16.0
D_MIX = SB_W + LRU_W + GLA_W
D_FF = 2816
EPS = 1e-6
PAGE_SIZE = 128

LANES = 128
SUBLANES = 8
VMEM_LIMIT_BYTES = 56 * 1024 * 1024

_OFF_Q = 0
_OFF_K = _OFF_Q + SB_W
_OFF_V = _OFF_K + SB_W
_OFF_XL = _OFF_V + SB_W
_OFF_GATE = _OFF_XL + LRU_W
_OFF_QG = _OFF_GATE + LRU_W
_OFF_KG = _OFF_QG + GLA_QK
_OFF_VG = _OFF_KG + GLA_QK
_OFF_RG = _OFF_VG + GLA_W
_OFF_ALR = _OFF_RG + GLA_W
D_IN_PAD = _OFF_ALR + LANES

FF_CHUNK = 256
HEAD_MEAN_WIDTH = 256
ROW_TILE = 512
SB_BLOCK = 256
SB_BLOCKS_PER_ITER = 2
GLA_CHUNK = 128
GLA_CHUNKS_PER_STEP = 4
DECODE_PAGES_PER_STEP = 32


def _idiv(x, d):
    assert d & (d - 1) == 0
    return x >> (d.bit_length() - 1)


def _imod(x, d):
    assert d & (d - 1) == 0
    return x & (d - 1)


def _dot(a, b):
    return jnp.dot(a, b, preferred_element_type=F32)


def _dot_nt(a, b):
    return lax.dot_general(a, b, (((1,), (1,)), ((), ())), preferred_element_type=F32)


def _dot_tn(a, b):
    return lax.dot_general(a, b, (((0,), (0,)), ((), ())), preferred_element_type=F32)


def _split_dot(x, w, passes):
    hi = x.astype(BF16)
    acc = _dot(hi, w)
    rem = x - hi.astype(F32)
    for _ in range(passes - 1):
        lo = rem.astype(BF16)
        acc = acc + _dot(lo, w)
        rem = rem - lo.astype(F32)
    return acc


def _split_dot_left(w, x, passes):
    hi = x.astype(BF16)
    acc = _dot(w, hi)
    rem = x - hi.astype(F32)
    for _ in range(passes - 1):
        lo = rem.astype(BF16)
        acc = acc + _dot(w, lo)
        rem = rem - lo.astype(F32)
    return acc


def _rms(x, g):
    ms = jnp.mean(x * x, axis=-1, keepdims=True)
    return x * lax.rsqrt(ms + EPS) * g


def _softplus(z):
    return jnp.maximum(z, 0.0) + jnp.log1p(jnp.exp(-jnp.abs(z)))


def _log_sigmoid(z):
    return jnp.minimum(z, 0.0) - jnp.log1p(jnp.exp(-jnp.abs(z)))


def _gelu_tanh(x):
    return 0.5 * x * (1.0 + jnp.tanh(0.7978845608028654 * (x + 0.044715 * (x * x * x))))


def _head_rms_cols(x, e64, g):
    cw = e64.shape[0]
    cols = []
    for c in range(x.shape[1] // cw):
        xc = x[:, c * cw:(c + 1) * cw]
        ms = _split_dot(xc * xc, e64, 2)
        cols.append(xc * lax.rsqrt(ms + EPS) * g[:, c * cw:(c + 1) * cw])
    return cols


def _resident(shape):
    nd = len(shape)
    return pl.BlockSpec(shape, lambda *_: (0,) * nd, pipeline_mode=pl.Buffered(1))


def _layer_resident(shape, layer):
    nd = len(shape)
    return pl.BlockSpec((None,) + tuple(shape), lambda *_: (layer,) + (0,) * nd, pipeline_mode=pl.Buffered(1))


def _params(semantics):
    return pltpu.CompilerParams(dimension_semantics=semantics, vmem_limit_bytes=VMEM_LIMIT_BYTES)


def _ffn_rows(x, g_ref, wg_ref, wu_ref, wd_ref, h_ref, acc_ref):
    h_ref[...] = _rms(x, g_ref[...]).astype(BF16)
    acc_ref[...] = jnp.zeros_like(acc_ref)

    def body(c, carry):
        cols = pl.ds(pl.multiple_of(c * FF_CHUNK, FF_CHUNK), FF_CHUNK)
        h = h_ref[...]
        gate = _dot(h, wg_ref[:, cols].astype(BF16))
        up = _dot(h, wu_ref[:, cols].astype(BF16))
        act = (gate * jax.nn.sigmoid(gate) * up).astype(BF16)
        acc_ref[...] += _dot(act, wd_ref[cols, :].astype(BF16))
        return carry

    lax.fori_loop(0, D_FF // FF_CHUNK, body, 0, unroll=True)
    return x + 0.5 * acc_ref[...]


def _ffn_kernel(xp_ref, xs_ref, g_ref, wg_ref, wu_ref, wd_ref, op_ref, os_ref, h_ref, acc_ref):
    weights = (g_ref, wg_ref, wu_ref, wd_ref)
    op_ref[...] = _ffn_rows(xp_ref[...], *weights, h_ref, acc_ref)

    @pl.when(pl.program_id(0) == pl.num_programs(0) - 1)
    def _sample_rows():
        rows = xs_ref.shape[0]
        os_ref[...] = _ffn_rows(xs_ref[...], *weights, h_ref.at[:rows], acc_ref.at[:rows])


def _ffn(xp, xs, g, wg, wu, wd, layer, tm):
    n = xp.shape[0]
    ns = xs.shape[0]
    assert ns <= tm
    return pl.pallas_call(
        _ffn_kernel,
        grid=(n // tm,),
        in_specs=[
            pl.BlockSpec((tm, D_MODEL), lambda i: (i, 0)),
            _resident((ns, D_MODEL)),
            _resident((1, D_MODEL)),
            _layer_resident((D_MODEL, D_FF), layer),
            _layer_resident((D_MODEL, D_FF), layer),
            _layer_resident((D_FF, D_MODEL), layer),
        ],
        out_specs=[pl.BlockSpec((tm, D_MODEL), lambda i: (i, 0)),
                   pl.BlockSpec((ns, D_MODEL), lambda i: (0, 0))],
        out_shape=[jax.ShapeDtypeStruct((n, D_MODEL), F32), jax.ShapeDtypeStruct((ns, D_MODEL), F32)],
        scratch_shapes=[pltpu.VMEM((tm, D_MODEL), BF16), pltpu.VMEM((tm, D_MODEL), F32)],
        compiler_params=_params(("arbitrary",)),
        name="ffn",
    )(xp, xs, g, wg, wu, wd)


def _inproj_kernel(*refs, transposed_kv):
    if transposed_kv:
        (x_ref, g_ref, w_ref, gq_ref, gk_ref, wal_ref, bal_ref, e64_ref, wkvt_ref,
         q_ref, k_ref, v_ref, xl_ref, gate_ref, qg_ref, kg_ref, vg_ref, rg_ref, la_ref, h_ref) = refs
    else:
        (x_ref, g_ref, w_ref, gq_ref, gk_ref, wal_ref, bal_ref, e64_ref,
         q_ref, k_ref, v_ref, xl_ref, gate_ref, qg_ref, kg_ref, vg_ref, rg_ref, la_ref, h_ref) = refs
    h_ref[...] = _rms(x_ref[...], g_ref[...]).astype(BF16)

    def proj(lo, width):
        return _dot(h_ref[...], w_ref[:, lo:lo + width])

    e64 = e64_ref[...]
    q_cols = _head_rms_cols(proj(_OFF_Q, SB_W), e64, gq_ref[...])
    cw = e64.shape[0]
    for c, col in enumerate(q_cols):
        q_ref[:, c * cw:(c + 1) * cw] = col * (HEAD_DIM ** -0.5)
    if transposed_kv:
        rows = h_ref.shape[0]
        k_t = _dot_nt(wkvt_ref[:SB_W, :], h_ref[...]).reshape(SB_HEADS, HEAD_DIM, rows)
        ms = jnp.mean(k_t * k_t, axis=1, keepdims=True)
        k_ref[...] = (k_t * lax.rsqrt(ms + EPS)).reshape(SB_W, rows) * gk_ref[...]
        v_ref[...] = _dot_nt(wkvt_ref[SB_W:, :], h_ref[...])
    else:
        k_cols = _head_rms_cols(proj(_OFF_K, SB_W), e64, gk_ref[...])
        for c, col in enumerate(k_cols):
            k_ref[:, c * cw:(c + 1) * cw] = col
        v_ref[...] = proj(_OFF_V, SB_W)
    xl_ref[...] = proj(_OFF_XL, LRU_W)
    gate_ref[...] = _gelu_tanh(proj(_OFF_GATE, LRU_W))
    qg_ref[...] = proj(_OFF_QG, GLA_QK)
    kg_ref[...] = proj(_OFF_KG, GLA_QK)
    vg_ref[...] = proj(_OFF_VG, GLA_W)
    r = proj(_OFF_RG, GLA_W)
    rg_ref[...] = r * jax.nn.sigmoid(r)
    a_lr = proj(_OFF_ALR, LANES)
    xa = _dot(a_lr.astype(BF16), wal_ref[...]) + bal_ref[...]
    la_ref[...] = _log_sigmoid(xa) * (1.0 / GLA_TAU)


def _inproj(x, g, w_in, gq, gk, wal, bal, e64, tm, kv_slabs=None, layer=None, w_kvt=None, seq=None):
    n = x.shape[0]
    row = lambda w: pl.BlockSpec((tm, w), lambda i: (i, 0))
    in_specs = [
        row(D_MODEL),
        _resident((1, D_MODEL)),
        _resident((D_MODEL, D_IN_PAD)),
        _resident((1, SB_W)),
        _resident(gk.shape),
        _resident((LANES, LANES)),
        _resident((1, LANES)),
        _resident(e64.shape),
    ]
    args = [x, g, w_in, gq, gk, wal, bal, e64]
    small = lambda w: jax.ShapeDtypeStruct((n, w), F32)
    transposed_kv = kv_slabs is not None
    if not transposed_kv:
        kv_specs = [row(SB_W), row(SB_W)]
        kv_shapes = [small(SB_W), small(SB_W)]
        aliases = {}
        kernel = functools.partial(_inproj_kernel, transposed_kv=False)
    else:
        k_all, v_all = kv_slabs
        tiles = seq // tm
        slab = pl.BlockSpec((None, None, SB_W, tm), lambda i: (layer, i // tiles, 0, i % tiles))
        kv_specs = [slab, slab]
        kv_shapes = [jax.ShapeDtypeStruct(k_all.shape, F32), jax.ShapeDtypeStruct(v_all.shape, F32)]
        in_specs += [_resident((2 * SB_W, D_MODEL)),
                     pl.BlockSpec(memory_space=pl.ANY), pl.BlockSpec(memory_space=pl.ANY)]
        args += [w_kvt, k_all, v_all]
        aliases = {len(args) - 2: 1, len(args) - 1: 2}

        def kernel(*refs):
            _inproj_kernel(*refs[:9], *refs[11:], transposed_kv=True)

    out_specs = [row(SB_W)] + kv_specs + [row(LRU_W), row(LRU_W), row(GLA_QK), row(GLA_QK),
                                          row(GLA_W), row(GLA_W), row(GLA_QK)]
    out_shape = [small(SB_W)] + kv_shapes + [small(LRU_W), small(LRU_W), small(GLA_QK), small(GLA_QK),
                                             small(GLA_W), small(GLA_W), small(GLA_QK)]
    return pl.pallas_call(
        kernel,
        grid=(n // tm,),
        in_specs=in_specs,
        out_specs=out_specs,
        out_shape=out_shape,
        input_output_aliases=aliases,
        scratch_shapes=[pltpu.VMEM((tm, D_MODEL), BF16)],
        compiler_params=_params(("parallel",)),
        name="inproj",
    )(*args)


def _merge_kernel(x_ref, osb_ref, olru_ref, ogla_ref, gate_ref, rg_ref, gmo_ref, e64_ref, wout_ref, o_ref):
    e64 = e64_ref[...]
    gmo = gmo_ref[...]
    sb = _head_rms_cols(osb_ref[...], e64, gmo[:, :SB_W])
    lru = _head_rms_cols(olru_ref[...], e64, gmo[:, SB_W:SB_W + LRU_W])
    gla = _head_rms_cols(ogla_ref[...], e64, gmo[:, SB_W + LRU_W:])
    gate = gate_ref[...]
    rg = rg_ref[...]
    cw = e64.shape[0]
    lru = [col * gate[:, c * cw:(c + 1) * cw] for c, col in enumerate(lru)]
    gla = [col * rg[:, c * cw:(c + 1) * cw] for c, col in enumerate(gla)]
    acc = x_ref[...]
    for c, col in enumerate(sb + lru + gla):
        acc = acc + _dot(col.astype(BF16), wout_ref[c * cw:(c + 1) * cw, :])
    o_ref[...] = acc


def _merge(x, osb, olru, ogla, gate, rg, gmo, e64, wout, tm):
    n = x.shape[0]
    row = lambda w: pl.BlockSpec((tm, w), lambda i: (i, 0))
    return pl.pallas_call(
        _merge_kernel,
        grid=(n // tm,),
        in_specs=[row(D_MODEL), row(SB_W), row(LRU_W), row(GLA_W), row(LRU_W), row(GLA_W),
                  _resident((1, D_MIX)), _resident(e64.shape), _resident((D_MIX, D_MODEL))],
        out_specs=row(D_MODEL),
        out_shape=jax.ShapeDtypeStruct((n, D_MODEL), F32),
        compiler_params=_params(("parallel",)),
        name="merge",
    )(x, osb, olru, ogla, gate, rg, gmo, e64, wout)


_LOG2_E = 1.4426950408889634


def _softplus_plain(z):
    return jnp.maximum(z, 0.0) + jnp.log(1.0 + jnp.exp2(jnp.abs(z) * (-_LOG2_E)))


def _sb_tile_stages(q_h, kt_b, vt_h, bias, upper, tail, mask, tail_out=None):
    st = {}

    def logits():
        st["z"] = _dot(q_h, kt_b) + bias

    def log_terms():
        z = st["z"]
        sp = _softplus_plain(z)
        st["lb"] = z - sp
        if mask is not None:
            sp = jnp.where(mask, sp, 0.0)
        st["edge"] = sp[:, 0:1]
        st["sp"] = sp.astype(BF16)

    def suffix_sums():
        st["within"] = _dot(st["sp"], upper)

    def weights():
        within = st["within"]
        tail_in = tail[0] if isinstance(tail, list) else tail
        a = jnp.exp(st["lb"] - within - tail_in)
        if mask is not None:
            a = jnp.where(mask, a, 0.0)
        st["a"] = a.astype(BF16)
        st["tail"] = tail_in + within[:, 0:1] + st["edge"]
        if tail_out is not None:
            tail_out.append(st["tail"])

    def values():
        return _dot_nt(st["a"], vt_h), st["tail"]

    return [logits, log_terms, suffix_sums, weights, values]


def _run_skewed(pipelines):
    depth = len(pipelines[0])
    results = [None] * len(pipelines)
    for step in range(len(pipelines) + depth - 1):
        for p, stages in enumerate(pipelines):
            s = step - p
            if 0 <= s < depth:
                results[p] = stages[s]()
    return results


def _sb_prompt_kernel(bias_ref, q_ref, kt_ref, vt_ref, o_ref, kb_ref, vh_ref, *, blk):
    pairs = SB_W // LANES
    i = pl.program_id(1)
    first = lax.broadcasted_iota(jnp.int32, (1, LANES), 1) < HEAD_DIM

    @pl.when(i == 0)
    def _stage():
        kb_ref[...] = kt_ref[...].astype(BF16)
        first_rows = lax.broadcasted_iota(jnp.int32, (LANES, vt_ref.shape[1]), 0) < HEAD_DIM
        for p in range(pairs):
            v = vt_ref[p * LANES:(p + 1) * LANES, :]
            vh_ref[2 * p] = jnp.where(first_rows, v, 0.0).astype(BF16)
            vh_ref[2 * p + 1] = jnp.where(first_rows, 0.0, v).astype(BF16)

    q_heads = []
    biases = []
    for p in range(pairs):
        q = q_ref[:, p * LANES:(p + 1) * LANES]
        q_heads += [jnp.where(first, q, 0.0).astype(BF16), jnp.where(first, 0.0, q).astype(BF16)]
        biases += [bias_ref[2 * p], bias_ref[2 * p + 1]]
    r_i = lax.broadcasted_iota(jnp.int32, (blk, blk), 0)
    c_i = lax.broadcasted_iota(jnp.int32, (blk, blk), 1)
    upper = jnp.where(r_i > c_i, 1.0, 0.0).astype(BF16)
    causal = c_i < r_i

    def tiles(j, n_blocks, tails, accs, mask):
        pipelines = []
        for b in range(n_blocks):
            cols = pl.ds(pl.multiple_of((j - b) * blk, blk), blk)
            handed_on = [[] for _ in range(2 * pairs)]
            for h in range(2 * pairs):
                kt_b = kb_ref[(h // 2) * LANES:(h // 2 + 1) * LANES, cols]
                pipelines.append(_sb_tile_stages(q_heads[h], kt_b, vh_ref[h, :, cols], biases[h], upper,
                                                 tails[h], mask, handed_on[h]))
            tails = handed_on
        results = _run_skewed(pipelines)
        new_accs = list(accs)
        for p, (out, _) in enumerate(results):
            h = p % (2 * pairs)
            new_accs[h // 2] = new_accs[h // 2] + out
        return tuple(t for _, t in results[-2 * pairs:]), tuple(new_accs)

    zero_c = jnp.zeros((blk, 1), F32)
    zero_a = jnp.zeros((blk, LANES), F32)
    state = tiles(i, 1, (zero_c,) * (2 * pairs), (zero_a,) * pairs, causal)

    def body(it, state):
        return tiles(i - 1 - SB_BLOCKS_PER_ITER * it, SB_BLOCKS_PER_ITER, state[0], state[1], None)

    state = lax.fori_loop(0, i // SB_BLOCKS_PER_ITER, body, state)
    for left in range(1, SB_BLOCKS_PER_ITER):
        state = lax.cond(i % SB_BLOCKS_PER_ITER == left,
                         lambda st, left=left: tiles(left - 1, left, st[0], st[1], None),
                         lambda st: st, state)
    _, accs = state
    for p in range(pairs):
        o_ref[:, p * LANES:(p + 1) * LANES] = accs[p]


def _sb_prompt(q, kt_all, vt_all, bias, layer):
    _, batch, _, seq = kt_all.shape
    blk = min(SB_BLOCK, seq)
    nq = seq // blk
    kv_spec = pl.BlockSpec((None, None, SB_W, seq), lambda b, i, *_: (layer, b, 0, 0))
    return pl.pallas_call(
        functools.partial(_sb_prompt_kernel, blk=blk),
        grid_spec=pltpu.PrefetchScalarGridSpec(
            num_scalar_prefetch=1,
            grid=(batch, nq),
            in_specs=[pl.BlockSpec((blk, SB_W), lambda b, i, *_: (b * nq + i, 0)), kv_spec, kv_spec],
            out_specs=pl.BlockSpec((blk, SB_W), lambda b, i, *_: (b * nq + i, 0)),
            scratch_shapes=[pltpu.VMEM((SB_W, seq), BF16), pltpu.VMEM((SB_HEADS, LANES, seq), BF16)],
        ),
        out_shape=jax.ShapeDtypeStruct((batch * seq, SB_W), F32),
        compiler_params=_params(("parallel", "arbitrary")),
        name="sb_prompt",
    )(bias, q, kt_all, vt_all)


def _sb_decode_kernel(pt_ref, q_ref, bias_ref, *refs, n_pages, group):
    k_refs = refs[:group]
    v_refs = refs[group:2 * group]
    o_ref = refs[2 * group]
    z_ref, a_ref, acc_ref = refs[2 * group + 1:]
    s = pl.program_id(1)
    steps = n_pages // group
    rows = n_pages * SB_HEADS

    @pl.when(s < steps)
    def _scores():
        q_col = q_ref[...]
        for g in range(group):
            prod = (k_refs[g][...] * q_col).reshape(SB_HEADS, HEAD_DIM, PAGE_SIZE)
            z = jnp.sum(prod, axis=1) + bias_ref[...]
            row0 = pl.multiple_of((s * group + g) * SB_HEADS, SB_HEADS)
            z_ref[pl.ds(row0, SB_HEADS), :] = z

    @pl.when(s == steps - 1)
    def _weights():
        z = z_ref[...]
        sp = _softplus(z)
        l1m = -sp
        r_i = lax.broadcasted_iota(jnp.int32, (PAGE_SIZE, PAGE_SIZE), 0)
        c_i = lax.broadcasted_iota(jnp.int32, (PAGE_SIZE, PAGE_SIZE), 1)
        upper = (r_i > c_i).astype(BF16)
        ones = jnp.ones((PAGE_SIZE, PAGE_SIZE), BF16)
        within = _split_dot(l1m, upper, 3)
        page_total = _split_dot(l1m, ones, 3)
        pr = lax.broadcasted_iota(jnp.int32, (rows, rows), 0)
        pc = lax.broadcasted_iota(jnp.int32, (rows, rows), 1)
        same_head = _imod(pc, SB_HEADS) == _imod(pr, SB_HEADS)
        later_page = jnp.where(_idiv(pc, SB_HEADS) > _idiv(pr, SB_HEADS),
                               jnp.where(same_head, 1.0, 0.0), 0.0).astype(BF16)
        later = _split_dot_left(later_page, page_total, 3)
        a_ref[...] = jnp.exp(z - sp + within + later)
        acc_ref[...] = jnp.zeros_like(acc_ref)

    @pl.when(s >= steps)
    def _values():
        acc = acc_ref[...]
        for g in range(group):
            row0 = pl.multiple_of(((s - steps) * group + g) * SB_HEADS, SB_HEADS)
            a_g = a_ref[pl.ds(row0, SB_HEADS), :]
            a_rows = jnp.concatenate(
                [jnp.broadcast_to(a_g[h:h + 1, :], (HEAD_DIM, PAGE_SIZE)) for h in range(SB_HEADS)], axis=0)
            acc = acc + v_refs[g][...] * a_rows
        acc_ref[...] = acc

    @pl.when(s == 2 * steps - 1)
    def _emit():
        o_ref[...] = jnp.sum(acc_ref[...], axis=1, keepdims=True)


def _sb_decode(q_col, bias_b, cache_kt, cache_vt, page_table, layer):
    b, n_pages = page_table.shape
    group = min(DECODE_PAGES_PER_STEP, n_pages)
    steps = n_pages // group

    def k_map(g):
        return lambda bi, s, pt: (layer, pt[bi, jnp.minimum(s, steps - 1) * group + g], 0, 0)

    def v_map(g):
        def index(bi, s, pt):
            in_values = s >= steps
            row = jnp.where(in_values, bi, jnp.maximum(bi - 1, 0))
            col = jnp.where(in_values, s - steps, steps - 1) * group + g
            return (layer, pt[row, col], 0, 0)
        return index

    page = lambda m: pl.BlockSpec((None, None, SB_W, PAGE_SIZE), m)
    in_specs = [pl.BlockSpec((None, SB_W, 1), lambda bi, s, pt: (bi, 0, 0)),
                pl.BlockSpec((SB_HEADS, PAGE_SIZE), lambda bi, s, pt: (0, 0))]
    in_specs += [page(k_map(g)) for g in range(group)]
    in_specs += [page(v_map(g)) for g in range(group)]
    rows = n_pages * SB_HEADS
    return pl.pallas_call(
        functools.partial(_sb_decode_kernel, n_pages=n_pages, group=group),
        grid_spec=pltpu.PrefetchScalarGridSpec(
            num_scalar_prefetch=1,
            grid=(b, 2 * steps),
            in_specs=in_specs,
            out_specs=pl.BlockSpec((None, SB_W, 1), lambda bi, s, pt: (bi, 0, 0)),
            scratch_shapes=[pltpu.VMEM((rows, PAGE_SIZE), F32),
                            pltpu.VMEM((rows, PAGE_SIZE), F32),
                            pltpu.VMEM((SB_W, PAGE_SIZE), F32)],
        ),
        out_shape=jax.ShapeDtypeStruct((b, SB_W, 1), F32),
        compiler_params=_params(("parallel", "arbitrary")),
        name="sb_decode",
    )(page_table, q_col, bias_b, *([cache_kt] * group), *([cache_vt] * group))


def _lru_gates(xc, wa_ref, ba_ref, wi_ref, bi_ref, lam_ref):
    xb = xc.astype(BF16)
    r = jax.nn.sigmoid(_dot(xb, wa_ref[...]) + ba_ref[...])
    i = jax.nn.sigmoid(_dot(xb, wi_ref[...]) + bi_ref[...])
    log_a = -LRU_C * r * _softplus(-lam_ref[...])
    a = jnp.exp(log_a)
    t = jnp.tanh(log_a)
    one_minus_a2 = -2.0 * t / (1.0 - t)
    return a, jnp.sqrt(one_minus_a2) * (i * xc)


def _lru_prompt_kernel(x_ref, cw_ref, cb_ref, wa_ref, ba_ref, wi_ref, bi_ref, lam_ref,
                       o_ref, hl_ref, cn_ref, *, seq):
    x = x_ref[...]
    rows = lax.broadcasted_iota(jnp.int32, (seq, LRU_W), 0)
    xc = cb_ref[...] + cw_ref[LRU_CONV - 1:LRU_CONV, :] * x
    for j in range(1, LRU_CONV):
        shifted = jnp.where(rows >= j, pltpu.roll(x, j, 0), 0.0)
        xc = xc + cw_ref[LRU_CONV - 1 - j:LRU_CONV - j, :] * shifted
    a, g = _lru_gates(xc, wa_ref, ba_ref, wi_ref, bi_ref, lam_ref)
    step = 1
    while step < seq:
        valid = rows >= step
        a_prev = pltpu.roll(a, step, 0)
        g_prev = pltpu.roll(g, step, 0)
        g = jnp.where(valid, a * g_prev + g, g)
        a = jnp.where(valid, a * a_prev, a)
        step *= 2
    o_ref[...] = g
    hl_ref[...] = g[seq - 1:seq, :]
    cn_ref[...] = x[seq - (LRU_CONV - 1):, :]


def _lru_prompt(xl, cw, cb, wa, ba, wi, bi, lam, batch, seq):
    vec = _resident((1, LRU_W))
    mat = _resident((LRU_W, LRU_W))
    return pl.pallas_call(
        functools.partial(_lru_prompt_kernel, seq=seq),
        grid=(batch,),
        in_specs=[pl.BlockSpec((seq, LRU_W), lambda b: (b, 0)), _resident((LRU_CONV, LRU_W)), vec,
                  mat, vec, mat, vec, vec],
        out_specs=[pl.BlockSpec((seq, LRU_W), lambda b: (b, 0)),
                   pl.BlockSpec((None, 1, LRU_W), lambda b: (b, 0, 0)),
                   pl.BlockSpec((None, LRU_CONV - 1, LRU_W), lambda b: (b, 0, 0))],
        out_shape=[jax.ShapeDtypeStruct((batch * seq, LRU_W), F32),
                   jax.ShapeDtypeStruct((batch, 1, LRU_W), F32),
                   jax.ShapeDtypeStruct((batch, LRU_CONV - 1, LRU_W), F32)],
        compiler_params=_params(("parallel",)),
        name="lru_prompt",
    )(xl, cw, cb, wa, ba, wi, bi, lam)


def _block_mid_rows(bc, level):
    n_rows, width = bc.shape
    half = level // 2
    if level >= 2 * SUBLANES:
        return jnp.concatenate(
            [jnp.broadcast_to(bc[b * level + half - 1:b * level + half, :], (level, width))
             for b in range(n_rows // level)], axis=0)
    groups = n_rows // SUBLANES
    bc3 = bc.reshape(groups, SUBLANES, width)
    sub = lax.broadcasted_iota(jnp.int32, (groups, SUBLANES, width), 1)
    mid = None
    for b in range(SUBLANES // level):
        piece = jnp.broadcast_to(bc3[:, b * level + half - 1:b * level + half, :], bc3.shape)
        mid = piece if mid is None else jnp.where(_idiv(sub, level) == b, piece, mid)
    return mid.reshape(n_rows, width)


def _gla_chunk(qs, k, v, la, state):
    chunk = qs.shape[0]
    r_i = lax.broadcasted_iota(jnp.int32, (chunk, chunk), 0)
    c_i = lax.broadcasted_iota(jnp.int32, (chunk, chunk), 1)
    lower = (c_i <= r_i).astype(BF16)
    bc = _split_dot_left(lower, la, 3)
    rows = lax.broadcasted_iota(jnp.int32, (chunk, GLA_QK), 0)
    qk_head = _idiv(lax.broadcasted_iota(jnp.int32, (1, GLA_QK), 1), GLA_DK)
    v_head = _idiv(lax.broadcasted_iota(jnp.int32, (1, GLA_W), 1), GLA_DV)
    own = (_idiv(lax.broadcasted_iota(jnp.int32, (GLA_QK, GLA_W), 0), GLA_DK)
           == _idiv(lax.broadcasted_iota(jnp.int32, (GLA_QK, GLA_W), 1), GLA_DV))

    out = _dot((qs * jnp.exp(bc)).astype(BF16), state.astype(BF16))

    scores = [jnp.zeros((chunk, chunk), F32) for _ in range(GLA_HEADS)]
    level = 2
    while level <= chunk:
        half = level // 2
        mid = _block_mid_rows(bc, level)
        second = (rows & (level - 1)) >= half
        q_dec = jnp.where(second, qs * jnp.exp(bc - mid), 0.0)
        k_dec = jnp.where(second, 0.0, k * jnp.exp(mid - bc)).astype(BF16)
        same_block = _idiv(r_i, level) == _idiv(c_i, level)
        for h in range(GLA_HEADS):
            sc = _dot_nt(jnp.where(qk_head == h, q_dec, 0.0).astype(BF16), k_dec)
            if level < chunk:
                sc = jnp.where(same_block, sc, 0.0)
            scores[h] = scores[h] + sc
        level *= 2
    for h in range(GLA_HEADS):
        out = out + _dot(scores[h].astype(BF16), jnp.where(v_head == h, v, 0.0).astype(BF16))

    expand = jnp.where(own, 1.0, 0.0).astype(BF16)
    out = out + _split_dot(qs * k, expand, 2) * v

    last = bc[chunk - 1:chunk, :]
    kv = _dot_tn((k * jnp.exp(last - bc)).astype(BF16), v.astype(BF16))
    dec_col = jnp.transpose(jnp.broadcast_to(jnp.exp(last), (GLA_QK, GLA_QK)))
    dec = jnp.concatenate([dec_col] * (GLA_W // GLA_QK), axis=1)
    return out, dec * state + jnp.where(own, kv, 0.0)


def _gla_prompt_kernel(q_ref, k_ref, v_ref, la_ref, o_ref, sfin_ref, s_ref, *, chunk, n_steps):
    ci = pl.program_id(1)

    @pl.when(ci == 0)
    def _init():
        s_ref[...] = jnp.zeros_like(s_ref)

    state = s_ref[...]
    for u in range(q_ref.shape[0] // chunk):
        rows = slice(u * chunk, (u + 1) * chunk)
        out, state = _gla_chunk(q_ref[rows, :] * (GLA_DK ** -0.5), k_ref[rows, :], v_ref[rows, :],
                                la_ref[rows, :], state)
        o_ref[rows, :] = out
    s_ref[...] = state

    @pl.when(ci == n_steps - 1)
    def _emit():
        for h in range(GLA_HEADS):
            sfin_ref[h * GLA_DK:(h + 1) * GLA_DK, :] = (
                state[h * GLA_DK:(h + 1) * GLA_DK, h * GLA_DV:(h + 1) * GLA_DV])


def _gla_prompt(qg, kg, vg, la, batch, seq):
    chunk = min(GLA_CHUNK, seq)
    block = chunk * GLA_CHUNKS_PER_STEP if seq % (chunk * GLA_CHUNKS_PER_STEP) == 0 else chunk
    n_steps = seq // block
    row = lambda w: pl.BlockSpec((block, w), lambda b, c: (b * n_steps + c, 0))
    return pl.pallas_call(
        functools.partial(_gla_prompt_kernel, chunk=chunk, n_steps=n_steps),
        grid=(batch, n_steps),
        in_specs=[row(GLA_QK), row(GLA_QK), row(GLA_W), row(GLA_QK)],
        out_specs=[row(GLA_W), pl.BlockSpec((None, GLA_QK, GLA_DV), lambda b, c: (b, 0, 0))],
        out_shape=[jax.ShapeDtypeStruct((batch * seq, GLA_W), F32),
                   jax.ShapeDtypeStruct((batch, GLA_QK, GLA_DV), F32)],
        scratch_shapes=[pltpu.VMEM((GLA_QK, GLA_W), F32)],
        compiler_params=_params(("parallel", "arbitrary")),
        name="gla_prompt",
    )(qg, kg, vg, la)


def _step_kernel(x_ref, conv_ref, h0_ref, cw_ref, cb_ref, wa_ref, ba_ref, wi_ref, bi_ref, lam_ref,
                 qc_ref, kc_ref, lac_ref, v_ref, s0_ref,
                 h_ref, cn_ref, og_ref, s_ref):
    x = x_ref[...]
    xc = cb_ref[...] + cw_ref[LRU_CONV - 1:LRU_CONV, :] * x
    for j in range(LRU_CONV - 1):
        xc = xc + cw_ref[j:j + 1, :] * conv_ref[j]
    a, g = _lru_gates(xc, wa_ref, ba_ref, wi_ref, bi_ref, lam_ref)
    h_ref[...] = a * h0_ref[...] + g
    for j in range(LRU_CONV - 2):
        cn_ref[j] = conv_ref[j + 1]
    cn_ref[LRU_CONV - 2] = x

    alpha = jnp.exp(lac_ref[...])
    kc = kc_ref[...]
    qc = qc_ref[...] * (GLA_DK ** -0.5)
    for h in range(GLA_HEADS):
        rs = slice(h * GLA_DK, (h + 1) * GLA_DK)
        new = alpha[:, rs, :] * s0_ref[:, rs, :] + kc[:, rs, :] * v_ref[:, h:h + 1, :]
        s_ref[:, rs, :] = new
        og_ref[:, h:h + 1, :] = jnp.sum(qc[:, rs, :] * new, axis=1, keepdims=True)


def _step(xl, conv0, h0, cw, cb, wa, ba, wi, bi, lam, q_col, k_col, la_col, v3, s0):
    b = xl.shape[0]
    return pl.pallas_call(
        _step_kernel,
        out_shape=[jax.ShapeDtypeStruct((b, LRU_W), F32),
                   jax.ShapeDtypeStruct((LRU_CONV - 1, b, LRU_W), F32),
                   jax.ShapeDtypeStruct((b, GLA_HEADS, GLA_DV), F32),
                   jax.ShapeDtypeStruct((b, GLA_QK, GLA_DV), F32)],
        compiler_params=pltpu.CompilerParams(vmem_limit_bytes=VMEM_LIMIT_BYTES),
        name="sample_step",
    )(xl, conv0, h0, cw, cb, wa, ba, wi, bi, lam, q_col, k_col, la_col, v3, s0)


def _block_diag(w):
    nb, bi, bj = w.shape
    eye = jnp.eye(nb, dtype=w.dtype)
    return (eye[:, None, :, None] * w[:, :, None, :]).reshape(nb * bi, nb * bj)


def _row_tile(n):
    return min(ROW_TILE, n)


def kernel(x_prompt, x_sample, cache_k, cache_v, page_table, state_lru_h, state_lru_conv, state_gla,
           g_ffn1, w_ffn1_gate, w_ffn1_up, w_ffn1_down, g_mix, w_in, g_qnorm, g_knorm, sb_bias,
           conv_w, conv_b, lru_wa, lru_ba, lru_wi, lru_bi, lru_lambda, gla_w_alpha, gla_b_alpha,
           g_mix_out, w_out, g_ffn2, w_ffn2_gate, w_ffn2_up, w_ffn2_down):
    depth = w_in.shape[0]
    bp, seq, _ = x_prompt.shape
    bs, dec_seq, _ = x_sample.shape
    assert dec_seq == 1
    n_p = bp * seq
    n_phys = cache_k.shape[1]

    w_in_p = jnp.pad(w_in, ((0, 0), (0, 0), (0, D_IN_PAD - w_in.shape[2]))).astype(BF16)
    w_kvt = w_in[:, :, _OFF_K:_OFF_XL].transpose(0, 2, 1).astype(BF16)
    w_al = jnp.pad(gla_w_alpha, ((0, 0), (0, LANES - GLA_RANK), (0, 0))).astype(BF16)
    w_out_b = w_out.astype(BF16)
    wa_bd = jax.vmap(_block_diag)(lru_wa).astype(BF16)
    wi_bd = jax.vmap(_block_diag)(lru_wi).astype(BF16)
    gq = jnp.tile(g_qnorm, (1, SB_HEADS))[:, None, :]
    gk = jnp.tile(g_knorm, (1, SB_HEADS))[:, None, :]
    gk_col = jnp.tile(g_knorm, (1, SB_HEADS))[:, :, None]
    lane = jnp.arange(HEAD_MEAN_WIDTH)
    e64 = ((lane[:, None] // HEAD_DIM == lane[None, :] // HEAD_DIM).astype(F32) / HEAD_DIM).astype(BF16)
    vec = lambda t, l: t[l][None, :]

    cache_kt = cache_k.transpose(0, 1, 3, 4, 2).reshape(depth, n_phys, SB_W, PAGE_SIZE)
    cache_vt = cache_v.transpose(0, 1, 3, 4, 2).reshape(depth, n_phys, SB_W, PAGE_SIZE)

    xp = x_prompt.reshape(n_p, D_MODEL)
    xs = x_sample.reshape(bs, D_MODEL)
    tm_p = _row_tile(n_p)
    tm_s = _row_tile(bs)
    kt_all = jnp.zeros((depth, bp, SB_W, seq), F32)
    vt_all = jnp.zeros((depth, bp, SB_W, seq), F32)
    outs = {name: [] for name in ("ks", "vs", "hp", "hs", "cp", "cs", "sp", "ss")}

    for l in range(depth):
        lru_w = (conv_w[l], vec(conv_b, l), wa_bd[l], vec(lru_ba, l), wi_bd[l], vec(lru_bi, l),
                 vec(lru_lambda, l))
        inproj_w = (vec(g_mix, l), w_in_p[l], gq[l])
        inproj_w2 = (w_al[l], vec(gla_b_alpha, l), e64)

        ffn1_w = (vec(g_ffn1, l), w_ffn1_gate, w_ffn1_up, w_ffn1_down, l)
        ffn2_w = (vec(g_ffn2, l), w_ffn2_gate, w_ffn2_up, w_ffn2_down, l)
        xp, xs = _ffn(xp, xs, *ffn1_w, tm_p)

        q, kt_all, vt_all, xl, gate, qg, kg, vg, rg, la = _inproj(
            xp, *inproj_w, gk_col[l], *inproj_w2, tm_p, (kt_all, vt_all), l, w_kvt[l], seq)
        osb = _sb_prompt(q, kt_all, vt_all, sb_bias[l], l)
        olru, h_last, conv_new = _lru_prompt(xl, *lru_w, bp, seq)
        ogla, s_fin = _gla_prompt(qg, kg, vg, la, bp, seq)
        xp = _merge(xp, osb, olru, ogla, gate, rg, vec(g_mix_out, l), e64, w_out_b[l], tm_p)
        outs["hp"].append(h_last.reshape(bp, LRU_W))
        outs["cp"].append(conv_new)
        outs["sp"].append(s_fin.reshape(bp, GLA_HEADS, GLA_DK, GLA_DV))

        q, k_new, v_new, xl, gate, qg, kg, vg, rg, la = _inproj(xs, *inproj_w, gk[l], *inproj_w2, tm_s)
        bias_b = jnp.broadcast_to(sb_bias[l][:, None], (SB_HEADS, PAGE_SIZE))
        osb = _sb_decode(q.reshape(bs, SB_W, 1), bias_b, cache_kt, cache_vt, page_table, l)
        h_new, conv_new, ogla, s_new = _step(
            xl, state_lru_conv[l].transpose(1, 0, 2), state_lru_h[l], *lru_w,
            qg.reshape(bs, GLA_QK, 1), kg.reshape(bs, GLA_QK, 1), la.reshape(bs, GLA_QK, 1),
            vg.reshape(bs, GLA_HEADS, GLA_DV), state_gla[l].reshape(bs, GLA_QK, GLA_DV))
        xs = _merge(xs, osb.reshape(bs, SB_W), h_new, ogla.reshape(bs, GLA_W), gate, rg,
                    vec(g_mix_out, l), e64, w_out_b[l], tm_s)
        xp, xs = _ffn(xp, xs, *ffn2_w, tm_p)
        outs["ks"].append(k_new.reshape(bs, 1, SB_HEADS, HEAD_DIM))
        outs["vs"].append(v_new.reshape(bs, 1, SB_HEADS, HEAD_DIM))
        outs["hs"].append(h_new)
        outs["cs"].append(conv_new.transpose(1, 0, 2))
        outs["ss"].append(s_new.reshape(bs, GLA_HEADS, GLA_DK, GLA_DV))

    st = lambda name: jnp.stack(outs[name])
    return (xp.reshape(bp, seq, D_MODEL), xs.reshape(bs, 1, D_MODEL),
            kt_all.reshape(depth, bp, SB_HEADS, HEAD_DIM, seq).transpose(0, 1, 4, 2, 3),
            vt_all.reshape(depth, bp, SB_HEADS, HEAD_DIM, seq).transpose(0, 1, 4, 2, 3),
            st("ks"), st("vs"), st("hp"), st("hs"), st("cp"), st("cs"), st("sp"), st("ss"))
```

```python
import functools

import jax
import jax.numpy as jnp
from jax import lax
from jax.experimental import pallas as pl
from jax.experimental.pallas import tpu as pltpu

F32 = jnp.float32
BF16 = jnp.bfloat16

D_MODEL = 1024
HEAD_DIM = 64
SB_W = D_MODEL // 2
SB_HEADS = SB_W // HEAD_DIM
LRU_W = D_MODEL // 4
LRU_BLOCKS = LRU_W // HEAD_DIM
LRU_CONV = 4
LRU_C = 8.0
GLA_W = D_MODEL // 4
GLA_HEADS = GLA_W // HEAD_DIM
GLA_DV = HEAD_DIM
GLA_DK = HEAD_DIM // 2
GLA_QK = GLA_HEADS * GLA_DK
GLA_RANK = 16
GLA_TAU = 16.0
D_MIX = SB_W + LRU_W + GLA_W
D_FF = 2816
EPS = 1e-6
PAGE_SIZE = 128

LANES = 128
SUBLANES = 8
VMEM_LIMIT_BYTES = 56 * 1024 * 1024

_OFF_Q = 0
_OFF_K = _OFF_Q + SB_W
_OFF_V = _OFF_K + SB_W
_OFF_XL = _OFF_V + SB_W
_OFF_GATE = _OFF_XL + LRU_W
_OFF_QG = _OFF_GATE + LRU_W
_OFF_KG = _OFF_QG + GLA_QK
_OFF_VG = _OFF_KG + GLA_QK
_OFF_RG = _OFF_VG + GLA_W
_OFF_ALR = _OFF_RG + GLA_W
D_IN_PAD = _OFF_ALR + LANES

FF_CHUNK = 256
HEAD_MEAN_WIDTH = 256
ROW_TILE = 512
SB_BLOCK = 256
SB_BLOCKS_PER_ITER = 4
GLA_CHUNK = 128
GLA_CHUNKS_PER_STEP = 4
DECODE_PAGES_PER_STEP = 32


def _idiv(x, d):
    assert d & (d - 1) == 0
    return x >> (d.bit_length() - 1)


def _imod(x, d):
    assert d & (d - 1) == 0
    return x & (d - 1)


def _dot(a, b):
    return jnp.dot(a, b, preferred_element_type=F32)


def _dot_nt(a, b):
    return lax.dot_general(a, b, (((1,), (1,)), ((), ())), preferred_element_type=F32)


def _dot_tn(a, b):
    return lax.dot_general(a, b, (((0,), (0,)), ((), ())), preferred_element_type=F32)


def _split_dot(x, w, passes):
    hi = x.astype(BF16)
    acc = _dot(hi, w)
    rem = x - hi.astype(F32)
    for _ in range(passes - 1):
        lo = rem.astype(BF16)
        acc = acc + _dot(lo, w)
        rem = rem - lo.astype(F32)
    return acc


def _split_dot_left(w, x, passes):
    hi = x.astype(BF16)
    acc = _dot(w, hi)
    rem = x - hi.astype(F32)
    for _ in range(passes - 1):
        lo = rem.astype(BF16)
        acc = acc + _dot(w, lo)
        rem = rem - lo.astype(F32)
    return acc


def _rms(x, g):
    ms = jnp.mean(x * x, axis=-1, keepdims=True)
    return x * lax.rsqrt(ms + EPS) * g


def _softplus(z):
    return jnp.maximum(z, 0.0) + jnp.log1p(jnp.exp(-jnp.abs(z)))


def _log_sigmoid(z):
    return jnp.minimum(z, 0.0) - jnp.log1p(jnp.exp(-jnp.abs(z)))


def _gelu_tanh(x):
    return 0.5 * x * (1.0 + jnp.tanh(0.7978845608028654 * (x + 0.044715 * (x * x * x))))


def _head_rms_cols(x, e64, g):
    cw = e64.shape[0]
    cols = []
    for c in range(x.shape[1] // cw):
        xc = x[:, c * cw:(c + 1) * cw]
        ms = _split_dot(xc * xc, e64, 2)
        cols.append(xc * lax.rsqrt(ms + EPS) * g[:, c * cw:(c + 1) * cw])
    return cols


def _resident(shape):
    nd = len(shape)
    return pl.BlockSpec(shape, lambda *_: (0,) * nd, pipeline_mode=pl.Buffered(1))


def _layer_resident(shape, layer):
    nd = len(shape)
    return pl.BlockSpec((None,) + tuple(shape), lambda *_: (layer,) + (0,) * nd, pipeline_mode=pl.Buffered(1))


def _params(semantics):
    return pltpu.CompilerParams(dimension_semantics=semantics, vmem_limit_bytes=VMEM_LIMIT_BYTES)


def _ffn_rows(x, g_ref, wg_ref, wu_ref, wd_ref, h_ref, acc_ref):
    h_ref[...] = _rms(x, g_ref[...]).astype(BF16)
    acc_ref[...] = jnp.zeros_like(acc_ref)

    def body(c, carry):
        cols = pl.ds(pl.multiple_of(c * FF_CHUNK, FF_CHUNK), FF_CHUNK)
        h = h_ref[...]
        gate = _dot(h, wg_ref[:, cols].astype(BF16))
        up = _dot(h, wu_ref[:, cols].astype(BF16))
        act = (gate * jax.nn.sigmoid(gate) * up).astype(BF16)
        acc_ref[...] += _dot(act, wd_ref[cols, :].astype(BF16))
        return carry

    lax.fori_loop(0, D_FF // FF_CHUNK, body, 0, unroll=True)
    return x + 0.5 * acc_ref[...]


def _ffn_kernel(xp_ref, xs_ref, g_ref, wg_ref, wu_ref, wd_ref, op_ref, os_ref, h_ref, acc_ref):
    weights = (g_ref, wg_ref, wu_ref, wd_ref)
    op_ref[...] = _ffn_rows(xp_ref[...], *weights, h_ref, acc_ref)

    @pl.when(pl.program_id(0) == pl.num_programs(0) - 1)
    def _sample_rows():
        rows = xs_ref.shape[0]
        os_ref[...] = _ffn_rows(xs_ref[...], *weights, h_ref.at[:rows], acc_ref.at[:rows])


def _ffn(xp, xs, g, wg, wu, wd, layer, tm):
    n = xp.shape[0]
    ns = xs.shape[0]
    assert ns <= tm
    return pl.pallas_call(
        _ffn_kernel,
        grid=(n // tm,),
        in_specs=[
            pl.BlockSpec((tm, D_MODEL), lambda i: (i, 0)),
            _resident((ns, D_MODEL)),
            _resident((1, D_MODEL)),
            _layer_resident((D_MODEL, D_FF), layer),
            _layer_resident((D_MODEL, D_FF), layer),
            _layer_resident((D_FF, D_MODEL), layer),
        ],
        out_specs=[pl.BlockSpec((tm, D_MODEL), lambda i: (i, 0)),
                   pl.BlockSpec((ns, D_MODEL), lambda i: (0, 0))],
        out_shape=[jax.ShapeDtypeStruct((n, D_MODEL), F32), jax.ShapeDtypeStruct((ns, D_MODEL), F32)],
        scratch_shapes=[pltpu.VMEM((tm, D_MODEL), BF16), pltpu.VMEM((tm, D_MODEL), F32)],
        compiler_params=_params(("arbitrary",)),
        name="ffn",
    )(xp, xs, g, wg, wu, wd)


def _inproj_kernel(*refs, transposed_kv):
    if transposed_kv:
        (x_ref, g_ref, w_ref, gq_ref, gk_ref, wal_ref, bal_ref, e64_ref, wkvt_ref,
         q_ref, k_ref, v_ref, xl_ref, gate_ref, qg_ref, kg_ref, vg_ref, rg_ref, la_ref, h_ref) = refs
    else:
        (x_ref, g_ref, w_ref, gq_ref, gk_ref, wal_ref, bal_ref, e64_ref,
         q_ref, k_ref, v_ref, xl_ref, gate_ref, qg_ref, kg_ref, vg_ref, rg_ref, la_ref, h_ref) = refs
    h_ref[...] = _rms(x_ref[...], g_ref[...]).astype(BF16)

    def proj(lo, width):
        return _dot(h_ref[...], w_ref[:, lo:lo + width])

    e64 = e64_ref[...]
    q_cols = _head_rms_cols(proj(_OFF_Q, SB_W), e64, gq_ref[...])
    cw = e64.shape[0]
    for c, col in enumerate(q_cols):
        q_ref[:, c * cw:(c + 1) * cw] = col * (HEAD_DIM ** -0.5)
    if transposed_kv:
        rows = h_ref.shape[0]
        k_t = _dot_nt(wkvt_ref[:SB_W, :], h_ref[...]).reshape(SB_HEADS, HEAD_DIM, rows)
        ms = jnp.mean(k_t * k_t, axis=1, keepdims=True)
        k_ref[...] = (k_t * lax.rsqrt(ms + EPS)).reshape(SB_W, rows) * gk_ref[...]
        v_ref[...] = _dot_nt(wkvt_ref[SB_W:, :], h_ref[...])
    else:
        k_cols = _head_rms_cols(proj(_OFF_K, SB_W), e64, gk_ref[...])
        for c, col in enumerate(k_cols):
            k_ref[:, c * cw:(c + 1) * cw] = col
        v_ref[...] = proj(_OFF_V, SB_W)
    xl_ref[...] = proj(_OFF_XL, LRU_W)
    gate_ref[...] = _gelu_tanh(proj(_OFF_GATE, LRU_W))
    qg_ref[...] = proj(_OFF_QG, GLA_QK)
    kg_ref[...] = proj(_OFF_KG, GLA_QK)
    vg_ref[...] = proj(_OFF_VG, GLA_W)
    r = proj(_OFF_RG, GLA_W)
    rg_ref[...] = r * jax.nn.sigmoid(r)
    a_lr = proj(_OFF_ALR, LANES)
    xa = _dot(a_lr.astype(BF16), wal_ref[...]) + bal_ref[...]
    la_ref[...] = _log_sigmoid(xa) * (1.0 / GLA_TAU)


def _inproj(x, g, w_in, gq, gk, wal, bal, e64, tm, kv_slabs=None, layer=None, w_kvt=None, seq=None):
    n = x.shape[0]
    row = lambda w: pl.BlockSpec((tm, w), lambda i: (i, 0))
    in_specs = [
        row(D_MODEL),
        _resident((1, D_MODEL)),
        _resident((D_MODEL, D_IN_PAD)),
        _resident((1, SB_W)),
        _resident(gk.shape),
        _resident((LANES, LANES)),
        _resident((1, LANES)),
        _resident(e64.shape),
    ]
    args = [x, g, w_in, gq, gk, wal, bal, e64]
    small = lambda w: jax.ShapeDtypeStruct((n, w), F32)
    transposed_kv = kv_slabs is not None
    if not transposed_kv:
        kv_specs = [row(SB_W), row(SB_W)]
        kv_shapes = [small(SB_W), small(SB_W)]
        aliases = {}
        kernel = functools.partial(_inproj_kernel, transposed_kv=False)
    else:
        k_all, v_all = kv_slabs
        tiles = seq // tm
        slab = pl.BlockSpec((None, None, SB_W, tm), lambda i: (layer, i // tiles, 0, i % tiles))
        kv_specs = [slab, slab]
        kv_shapes = [jax.ShapeDtypeStruct(k_all.shape, F32), jax.ShapeDtypeStruct(v_all.shape, F32)]
        in_specs += [_resident((2 * SB_W, D_MODEL)),
                     pl.BlockSpec(memory_space=pl.ANY), pl.BlockSpec(memory_space=pl.ANY)]
        args += [w_kvt, k_all, v_all]
        aliases = {len(args) - 2: 1, len(args) - 1: 2}

        def kernel(*refs):
            _inproj_kernel(*refs[:9], *refs[11:], transposed_kv=True)

    out_specs = [row(SB_W)] + kv_specs + [row(LRU_W), row(LRU_W), row(GLA_QK), row(GLA_QK),
                                          row(GLA_W), row(GLA_W), row(GLA_QK)]
    out_shape = [small(SB_W)] + kv_shapes + [small(LRU_W), small(LRU_W), small(GLA_QK), small(GLA_QK),
                                             small(GLA_W), small(GLA_W), small(GLA_QK)]
    return pl.pallas_call(
        kernel,
        grid=(n // tm,),
        in_specs=in_specs,
        out_specs=out_specs,
        out_shape=out_shape,
        input_output_aliases=aliases,
        scratch_shapes=[pltpu.VMEM((tm, D_MODEL), BF16)],
        compiler_params=_params(("parallel",)),
        name="inproj",
    )(*args)


def _merge_kernel(x_ref, osb_ref, olru_ref, ogla_ref, gate_ref, rg_ref, gmo_ref, e64_ref, wout_ref, o_ref):
    e64 = e64_ref[...]
    gmo = gmo_ref[...]
    sb = _head_rms_cols(osb_ref[...], e64, gmo[:, :SB_W])
    lru = _head_rms_cols(olru_ref[...], e64, gmo[:, SB_W:SB_W + LRU_W])
    gla = _head_rms_cols(ogla_ref[...], e64, gmo[:, SB_W + LRU_W:])
    gate = gate_ref[...]
    rg = rg_ref[...]
    cw = e64.shape[0]
    lru = [col * gate[:, c * cw:(c + 1) * cw] for c, col in enumerate(lru)]
    gla = [col * rg[:, c * cw:(c + 1) * cw] for c, col in enumerate(gla)]
    acc = x_ref[...]
    for c, col in enumerate(sb + lru + gla):
        acc = acc + _dot(col.astype(BF16), wout_ref[c * cw:(c + 1) * cw, :])
    o_ref[...] = acc


def _merge(x, osb, olru, ogla, gate, rg, gmo, e64, wout, tm):
    n = x.shape[0]
    row = lambda w: pl.BlockSpec((tm, w), lambda i: (i, 0))
    return pl.pallas_call(
        _merge_kernel,
        grid=(n // tm,),
        in_specs=[row(D_MODEL), row(SB_W), row(LRU_W), row(GLA_W), row(LRU_W), row(GLA_W),
                  _resident((1, D_MIX)), _resident(e64.shape), _resident((D_MIX, D_MODEL))],
        out_specs=row(D_MODEL),
        out_shape=jax.ShapeDtypeStruct((n, D_MODEL), F32),
        compiler_params=_params(("parallel",)),
        name="merge",
    )(x, osb, olru, ogla, gate, rg, gmo, e64, wout)


_LOG2_E = 1.4426950408889634


def _softplus_plain(z):
    return jnp.maximum(z, 0.0) + jnp.log(1.0 + jnp.exp2(jnp.abs(z) * (-_LOG2_E)))


def _sb_tile_stages(q_h, kt_b, vt_h, upper, tail, mask, tail_out=None):
    st = {}

    def logits():
        st["z"] = _dot(q_h, kt_b)

    def log_terms():
        z = st["z"]
        sp = _softplus_plain(z)
        st["lb"] = z - sp
        if mask is not None:
            sp = jnp.where(mask, sp, 0.0)
        st["edge"] = sp[:, 0:1]
        st["sp"] = sp.astype(BF16)

    def suffix_sums():
        st["within"] = _dot(st["sp"], upper)

    def weights():
        within = st["within"]
        tail_in = tail[0] if isinstance(tail, list) else tail
        a = jnp.exp(st["lb"] - within - tail_in)
        if mask is not None:
            a = jnp.where(mask, a, 0.0)
        st["a"] = a.astype(BF16)
        st["tail"] = tail_in + within[:, 0:1] + st["edge"]
        if tail_out is not None:
            tail_out.append(st["tail"])

    def values():
        return _dot_nt(st["a"], vt_h), st["tail"]

    return [logits, log_terms, suffix_sums, weights, values]


def _run_skewed(pipelines):
    depth = len(pipelines[0])
    results = [None] * len(pipelines)
    for step in range(len(pipelines) + depth - 1):
        for p, stages in enumerate(pipelines):
            s = step - p
            if 0 <= s < depth:
                results[p] = stages[s]()
    return results


def _sb_prompt_kernel(bias_ref, q_ref, kt_ref, vt_ref, o_ref, kh_ref, vh_ref, *, blk):
    pairs = SB_W // LANES
    i = pl.program_id(1)
    lane = lax.broadcasted_iota(jnp.int32, (1, LANES), 1)
    first = lane < HEAD_DIM

    @pl.when(i == 0)
    def _stage():
        row = lax.broadcasted_iota(jnp.int32, (LANES, vt_ref.shape[1]), 0)
        first_rows = row < HEAD_DIM
        ones_first = jnp.where(row < 2, 1.0, 0.0)
        ones_second = jnp.where(first_rows, 0.0, jnp.where(row < HEAD_DIM + 2, 1.0, 0.0))
        for p in range(pairs):
            k = kt_ref[p * LANES:(p + 1) * LANES, :]
            kh_ref[2 * p] = jnp.where(first_rows, k, ones_second).astype(BF16)
            kh_ref[2 * p + 1] = jnp.where(first_rows, ones_first, k).astype(BF16)
            v = vt_ref[p * LANES:(p + 1) * LANES, :]
            vh_ref[2 * p] = jnp.where(first_rows, v, 0.0).astype(BF16)
            vh_ref[2 * p + 1] = jnp.where(first_rows, 0.0, v).astype(BF16)

    def bias_lanes(h, lane0):
        b = jnp.full((1, LANES), bias_ref[h], F32)
        hi = b.astype(BF16).astype(F32)
        return jnp.where(lane == lane0, hi, jnp.where(lane == lane0 + 1, b - hi, 0.0))

    q_heads = []
    for p in range(pairs):
        q = q_ref[:, p * LANES:(p + 1) * LANES]
        q_heads += [jnp.where(first, q, bias_lanes(2 * p, HEAD_DIM)).astype(BF16),
                    jnp.where(first, bias_lanes(2 * p + 1, 0), q).astype(BF16)]
    r_i = lax.broadcasted_iota(jnp.int32, (blk, blk), 0)
    c_i = lax.broadcasted_iota(jnp.int32, (blk, blk), 1)
    upper = jnp.where(r_i > c_i, 1.0, 0.0).astype(BF16)
    causal = c_i < r_i

    def tiles(j, n_blocks, tails, accs, mask):
        pipelines = []
        for b in range(n_blocks):
            cols = pl.ds(pl.multiple_of((j - b) * blk, blk), blk)
            handed_on = [[] for _ in range(2 * pairs)]
            for h in range(2 * pairs):
                pipelines.append(_sb_tile_stages(q_heads[h], kh_ref[h, :, cols], vh_ref[h, :, cols], upper,
                                                 tails[h], mask, handed_on[h]))
            tails = handed_on
        results = _run_skewed(pipelines)
        new_accs = list(accs)
        for p, (out, _) in enumerate(results):
            h = p % (2 * pairs)
            new_accs[h // 2] = new_accs[h // 2] + out
        return tuple(t for _, t in results[-2 * pairs:]), tuple(new_accs)

    zero_c = jnp.zeros((blk, 1), F32)
    zero_a = jnp.zeros((blk, LANES), F32)
    state = tiles(i, 1, (zero_c,) * (2 * pairs), (zero_a,) * pairs, causal)

    def body(it, state):
        return tiles(i - 1 - SB_BLOCKS_PER_ITER * it, SB_BLOCKS_PER_ITER, state[0], state[1], None)

    state = lax.fori_loop(0, i // SB_BLOCKS_PER_ITER, body, state)
    for left in range(1, SB_BLOCKS_PER_ITER):
        state = lax.cond(i % SB_BLOCKS_PER_ITER == left,
                         lambda st, left=left: tiles(left - 1, left, st[0], st[1], None),
                         lambda st: st, state)
    _, accs = state
    for p in range(pairs):
        o_ref[:, p * LANES:(p + 1) * LANES] = accs[p]


def _sb_prompt(q, kt_all, vt_all, bias, layer):
    _, batch, _, seq = kt_all.shape
    blk = min(SB_BLOCK, seq)
    nq = seq // blk
    kv_spec = pl.BlockSpec((None, None, SB_W, seq), lambda b, i, *_: (layer, b, 0, 0))
    return pl.pallas_call(
        functools.partial(_sb_prompt_kernel, blk=blk),
        grid_spec=pltpu.PrefetchScalarGridSpec(
            num_scalar_prefetch=1,
            grid=(batch, nq),
            in_specs=[pl.BlockSpec((blk, SB_W), lambda b, i, *_: (b * nq + i, 0)), kv_spec, kv_spec],
            out_specs=pl.BlockSpec((blk, SB_W), lambda b, i, *_: (b * nq + i, 0)),
            scratch_shapes=[pltpu.VMEM((SB_HEADS, LANES, seq), BF16)] * 2,
        ),
        out_shape=jax.ShapeDtypeStruct((batch * seq, SB_W), F32),
        compiler_params=_params(("parallel", "arbitrary")),
        name="sb_prompt",
    )(bias, q, kt_all, vt_all)


def _sb_decode_kernel(pt_ref, q_ref, bias_ref, *refs, n_pages, group):
    k_refs = refs[:group]
    v_refs = refs[group:2 * group]
    o_ref = refs[2 * group]
    z_ref, a_ref, acc_ref = refs[2 * group + 1:]
    s = pl.program_id(1)
    steps = n_pages // group
    rows = n_pages * SB_HEADS

    @pl.when(s < steps)
    def _scores():
        q_col = q_ref[...]
        for g in range(group):
            prod = (k_refs[g][...] * q_col).reshape(SB_HEADS, HEAD_DIM, PAGE_SIZE)
            z = jnp.sum(prod, axis=1) + bias_ref[...]
            row0 = pl.multiple_of((s * group + g) * SB_HEADS, SB_HEADS)
            z_ref[pl.ds(row0, SB_HEADS), :] = z

    @pl.when(s == steps - 1)
    def _weights():
        z = z_ref[...]
        sp = _softplus(z)
        l1m = -sp
        r_i = lax.broadcasted_iota(jnp.int32, (PAGE_SIZE, PAGE_SIZE), 0)
        c_i = lax.broadcasted_iota(jnp.int32, (PAGE_SIZE, PAGE_SIZE), 1)
        upper = (r_i > c_i).astype(BF16)
        ones = jnp.ones((PAGE_SIZE, PAGE_SIZE), BF16)
        within = _split_dot(l1m, upper, 3)
        page_total = _split_dot(l1m, ones, 3)
        pr = lax.broadcasted_iota(jnp.int32, (rows, rows), 0)
        pc = lax.broadcasted_iota(jnp.int32, (rows, rows), 1)
        same_head = _imod(pc, SB_HEADS) == _imod(pr, SB_HEADS)
        later_page = jnp.where(_idiv(pc, SB_HEADS) > _idiv(pr, SB_HEADS),
                               jnp.where(same_head, 1.0, 0.0), 0.0).astype(BF16)
        later = _split_dot_left(later_page, page_total, 3)
        a_ref[...] = jnp.exp(z - sp + within + later)
        acc_ref[...] = jnp.zeros_like(acc_ref)

    @pl.when(s >= steps)
    def _values():
        acc = acc_ref[...]
        for g in range(group):
            row0 = pl.multiple_of(((s - steps) * group + g) * SB_HEADS, SB_HEADS)
            a_g = a_ref[pl.ds(row0, SB_HEADS), :]
            a_rows = jnp.concatenate(
                [jnp.broadcast_to(a_g[h:h + 1, :], (HEAD_DIM, PAGE_SIZE)) for h in range(SB_HEADS)], axis=0)
            acc = acc + v_refs[g][...] * a_rows
        acc_ref[...] = acc

    @pl.when(s == 2 * steps - 1)
    def _emit():
        o_ref[...] = jnp.sum(acc_ref[...], axis=1, keepdims=True)


def _sb_decode(q_col, bias_b, cache_kt, cache_vt, page_table, layer):
    b, n_pages = page_table.shape
    group = min(DECODE_PAGES_PER_STEP, n_pages)
    steps = n_pages // group

    def k_map(g):
        return lambda bi, s, pt: (layer, pt[bi, jnp.minimum(s, steps - 1) * group + g], 0, 0)

    def v_map(g):
        def index(bi, s, pt):
            in_values = s >= steps
            row = jnp.where(in_values, bi, jnp.maximum(bi - 1, 0))
            col = jnp.where(in_values, s - steps, steps - 1) * group + g
            return (layer, pt[row, col], 0, 0)
        return index

    page = lambda m: pl.BlockSpec((None, None, SB_W, PAGE_SIZE), m)
    in_specs = [pl.BlockSpec((None, SB_W, 1), lambda bi, s, pt: (bi, 0, 0)),
                pl.BlockSpec((SB_HEADS, PAGE_SIZE), lambda bi, s, pt: (0, 0))]
    in_specs += [page(k_map(g)) for g in range(group)]
    in_specs += [page(v_map(g)) for g in range(group)]
    rows = n_pages * SB_HEADS
    return pl.pallas_call(
        functools.partial(_sb_decode_kernel, n_pages=n_pages, group=group),
        grid_spec=pltpu.PrefetchScalarGridSpec(
            num_scalar_prefetch=1,
            grid=(b, 2 * steps),
            in_specs=in_specs,
            out_specs=pl.BlockSpec((None, SB_W, 1), lambda bi, s, pt: (bi, 0, 0)),
            scratch_shapes=[pltpu.VMEM((rows, PAGE_SIZE), F32),
                            pltpu.VMEM((rows, PAGE_SIZE), F32),
                            pltpu.VMEM((SB_W, PAGE_SIZE), F32)],
        ),
        out_shape=jax.ShapeDtypeStruct((b, SB_W, 1), F32),
        compiler_params=_params(("parallel", "arbitrary")),
        name="sb_decode",
    )(page_table, q_col, bias_b, *([cache_kt] * group), *([cache_vt] * group))


def _lru_gates(xc, wa_ref, ba_ref, wi_ref, bi_ref, lam_ref):
    xb = xc.astype(BF16)
    r = jax.nn.sigmoid(_dot(xb, wa_ref[...]) + ba_ref[...])
    i = jax.nn.sigmoid(_dot(xb, wi_ref[...]) + bi_ref[...])
    log_a = -LRU_C * r * _softplus(-lam_ref[...])
    a = jnp.exp(log_a)
    t = jnp.tanh(log_a)
    one_minus_a2 = -2.0 * t / (1.0 - t)
    return a, jnp.sqrt(one_minus_a2) * (i * xc)


def _lru_prompt_kernel(x_ref, cw_ref, cb_ref, wa_ref, ba_ref, wi_ref, bi_ref, lam_ref,
                       o_ref, hl_ref, cn_ref, *, seq):
    x = x_ref[...]
    rows = lax.broadcasted_iota(jnp.int32, (seq, LRU_W), 0)
    xc = cb_ref[...] + cw_ref[LRU_CONV - 1:LRU_CONV, :] * x
    for j in range(1, LRU_CONV):
        shifted = jnp.where(rows >= j, pltpu.roll(x, j, 0), 0.0)
        xc = xc + cw_ref[LRU_CONV - 1 - j:LRU_CONV - j, :] * shifted
    a, g = _lru_gates(xc, wa_ref, ba_ref, wi_ref, bi_ref, lam_ref)
    step = 1
    while step < seq:
        valid = rows >= step
        a_prev = pltpu.roll(a, step, 0)
        g_prev = pltpu.roll(g, step, 0)
        g = jnp.where(valid, a * g_prev + g, g)
        a = jnp.where(valid, a * a_prev, a)
        step *= 2
    o_ref[...] = g
    hl_ref[...] = g[seq - 1:seq, :]
    cn_ref[...] = x[seq - (LRU_CONV - 1):, :]


def _lru_prompt(xl, cw, cb, wa, ba, wi, bi, lam, batch, seq):
    vec = _resident((1, LRU_W))
    mat = _resident((LRU_W, LRU_W))
    return pl.pallas_call(
        functools.partial(_lru_prompt_kernel, seq=seq),
        grid=(batch,),
        in_specs=[pl.BlockSpec((seq, LRU_W), lambda b: (b, 0)), _resident((LRU_CONV, LRU_W)), vec,
                  mat, vec, mat, vec, vec],
        out_specs=[pl.BlockSpec((seq, LRU_W), lambda b: (b, 0)),
                   pl.BlockSpec((None, 1, LRU_W), lambda b: (b, 0, 0)),
                   pl.BlockSpec((None, LRU_CONV - 1, LRU_W), lambda b: (b, 0, 0))],
        out_shape=[jax.ShapeDtypeStruct((batch * seq, LRU_W), F32),
                   jax.ShapeDtypeStruct((batch, 1, LRU_W), F32),
                   jax.ShapeDtypeStruct((batch, LRU_CONV - 1, LRU_W), F32)],
        compiler_params=_params(("parallel",)),
        name="lru_prompt",
    )(xl, cw, cb, wa, ba, wi, bi, lam)


def _block_mid_rows(bc, level):
    n_rows, width = bc.shape
    half = level // 2
    if level >= 2 * SUBLANES:
        return jnp.concatenate(
            [jnp.broadcast_to(bc[b * level + half - 1:b * level + half, :], (level, width))
             for b in range(n_rows // level)], axis=0)
    groups = n_rows // SUBLANES
    bc3 = bc.reshape(groups, SUBLANES, width)
    sub = lax.broadcasted_iota(jnp.int32, (groups, SUBLANES, width), 1)
    mid = None
    for b in range(SUBLANES // level):
        piece = jnp.broadcast_to(bc3[:, b * level + half - 1:b * level + half, :], bc3.shape)
        mid = piece if mid is None else jnp.where(_idiv(sub, level) == b, piece, mid)
    return mid.reshape(n_rows, width)


def _gla_chunk(qs, k, v, la, state):
    chunk = qs.shape[0]
    r_i = lax.broadcasted_iota(jnp.int32, (chunk, chunk), 0)
    c_i = lax.broadcasted_iota(jnp.int32, (chunk, chunk), 1)
    lower = (c_i <= r_i).astype(BF16)
    bc = _split_dot_left(lower, la, 3)
    rows = lax.broadcasted_iota(jnp.int32, (chunk, GLA_QK), 0)
    qk_head = _idiv(lax.broadcasted_iota(jnp.int32, (1, GLA_QK), 1), GLA_DK)
    v_head = _idiv(lax.broadcasted_iota(jnp.int32, (1, GLA_W), 1), GLA_DV)
    own = (_idiv(lax.broadcasted_iota(jnp.int32, (GLA_QK, GLA_W), 0), GLA_DK)
           == _idiv(lax.broadcasted_iota(jnp.int32, (GLA_QK, GLA_W), 1), GLA_DV))

    out = _dot((qs * jnp.exp(bc)).astype(BF16), state.astype(BF16))

    scores = [jnp.zeros((chunk, chunk), F32) for _ in range(GLA_HEADS)]
    level = 2
    while level <= chunk:
        half = level // 2
        mid = _block_mid_rows(bc, level)
        second = (rows & (level - 1)) >= half
        q_dec = jnp.where(second, qs * jnp.exp(bc - mid), 0.0)
        k_dec = jnp.where(second, 0.0, k * jnp.exp(mid - bc)).astype(BF16)
        same_block = _idiv(r_i, level) == _idiv(c_i, level)
        for h in range(GLA_HEADS):
            sc = _dot_nt(jnp.where(qk_head == h, q_dec, 0.0).astype(BF16), k_dec)
            if level < chunk:
                sc = jnp.where(same_block, sc, 0.0)
            scores[h] = scores[h] + sc
        level *= 2
    for h in range(GLA_HEADS):
        out = out + _dot(scores[h].astype(BF16), jnp.where(v_head == h, v, 0.0).astype(BF16))

    expand = jnp.where(own, 1.0, 0.0).astype(BF16)
    out = out + _split_dot(qs * k, expand, 2) * v

    last = bc[chunk - 1:chunk, :]
    kv = _dot_tn((k * jnp.exp(last - bc)).astype(BF16), v.astype(BF16))
    dec_col = jnp.transpose(jnp.broadcast_to(jnp.exp(last), (GLA_QK, GLA_QK)))
    dec = jnp.concatenate([dec_col] * (GLA_W // GLA_QK), axis=1)
    return out, dec * state + jnp.where(own, kv, 0.0)


def _gla_prompt_kernel(q_ref, k_ref, v_ref, la_ref, o_ref, sfin_ref, s_ref, *, chunk, n_steps):
    ci = pl.program_id(1)

    @pl.when(ci == 0)
    def _init():
        s_ref[...] = jnp.zeros_like(s_ref)

    state = s_ref[...]
    for u in range(q_ref.shape[0] // chunk):
        rows = slice(u * chunk, (u + 1) * chunk)
        out, state = _gla_chunk(q_ref[rows, :] * (GLA_DK ** -0.5), k_ref[rows, :], v_ref[rows, :],
                                la_ref[rows, :], state)
        o_ref[rows, :] = out
    s_ref[...] = state

    @pl.when(ci == n_steps - 1)
    def _emit():
        for h in range(GLA_HEADS):
            sfin_ref[h * GLA_DK:(h + 1) * GLA_DK, :] = (
                state[h * GLA_DK:(h + 1) * GLA_DK, h * GLA_DV:(h + 1) * GLA_DV])


def _gla_prompt(qg, kg, vg, la, batch, seq):
    chunk = min(GLA_CHUNK, seq)
    block = chunk * GLA_CHUNKS_PER_STEP if seq % (chunk * GLA_CHUNKS_PER_STEP) == 0 else chunk
    n_steps = seq // block
    row = lambda w: pl.BlockSpec((block, w), lambda b, c: (b * n_steps + c, 0))
    return pl.pallas_call(
        functools.partial(_gla_prompt_kernel, chunk=chunk, n_steps=n_steps),
        grid=(batch, n_steps),
        in_specs=[row(GLA_QK), row(GLA_QK), row(GLA_W), row(GLA_QK)],
        out_specs=[row(GLA_W), pl.BlockSpec((None, GLA_QK, GLA_DV), lambda b, c: (b, 0, 0))],
        out_shape=[jax.ShapeDtypeStruct((batch * seq, GLA_W), F32),
                   jax.ShapeDtypeStruct((batch, GLA_QK, GLA_DV), F32)],
        scratch_shapes=[pltpu.VMEM((GLA_QK, GLA_W), F32)],
        compiler_params=_params(("parallel", "arbitrary")),
        name="gla_prompt",
    )(qg, kg, vg, la)


def _step_kernel(x_ref, conv_ref, h0_ref, cw_ref, cb_ref, wa_ref, ba_ref, wi_ref, bi_ref, lam_ref,
                 qc_ref, kc_ref, lac_ref, v_ref, s0_ref,
                 h_ref, cn_ref, og_ref, s_ref):
    x = x_ref[...]
    xc = cb_ref[...] + cw_ref[LRU_CONV - 1:LRU_CONV, :] * x
    for j in range(LRU_CONV - 1):
        xc = xc + cw_ref[j:j + 1, :] * conv_ref[j]
    a, g = _lru_gates(xc, wa_ref, ba_ref, wi_ref, bi_ref, lam_ref)
    h_ref[...] = a * h0_ref[...] + g
    for j in range(LRU_CONV - 2):
        cn_ref[j] = conv_ref[j + 1]
    cn_ref[LRU_CONV - 2] = x

    alpha = jnp.exp(lac_ref[...])
    kc = kc_ref[...]
    qc = qc_ref[...] * (GLA_DK ** -0.5)
    for h in range(GLA_HEADS):
        rs = slice(h * GLA_DK, (h + 1) * GLA_DK)
        new = alpha[:, rs, :] * s0_ref[:, rs, :] + kc[:, rs, :] * v_ref[:, h:h + 1, :]
        s_ref[:, rs, :] = new
        og_ref[:, h:h + 1, :] = jnp.sum(qc[:, rs, :] * new, axis=1, keepdims=True)


def _step(xl, conv0, h0, cw, cb, wa, ba, wi, bi, lam, q_col, k_col, la_col, v3, s0):
    b = xl.shape[0]
    return pl.pallas_call(
        _step_kernel,
        out_shape=[jax.ShapeDtypeStruct((b, LRU_W), F32),
                   jax.ShapeDtypeStruct((LRU_CONV - 1, b, LRU_W), F32),
                   jax.ShapeDtypeStruct((b, GLA_HEADS, GLA_DV), F32),
                   jax.ShapeDtypeStruct((b, GLA_QK, GLA_DV), F32)],
        compiler_params=pltpu.CompilerParams(vmem_limit_bytes=VMEM_LIMIT_BYTES),
        name="sample_step",
    )(xl, conv0, h0, cw, cb, wa, ba, wi, bi, lam, q_col, k_col, la_col, v3, s0)


def _block_diag(w):
    nb, bi, bj = w.shape
    eye = jnp.eye(nb, dtype=w.dtype)
    return (eye[:, None, :, None] * w[:, :, None, :]).reshape(nb * bi, nb * bj)


def _row_tile(n):
    return min(ROW_TILE, n)


def kernel(x_prompt, x_sample, cache_k, cache_v, page_table, state_lru_h, state_lru_conv, state_gla,
           g_ffn1, w_ffn1_gate, w_ffn1_up, w_ffn1_down, g_mix, w_in, g_qnorm, g_knorm, sb_bias,
           conv_w, conv_b, lru_wa, lru_ba, lru_wi, lru_bi, lru_lambda, gla_w_alpha, gla_b_alpha,
           g_mix_out, w_out, g_ffn2, w_ffn2_gate, w_ffn2_up, w_ffn2_down):
    depth = w_in.shape[0]
    bp, seq, _ = x_prompt.shape
    bs, dec_seq, _ = x_sample.shape
    assert dec_seq == 1
    n_p = bp * seq
    n_phys = cache_k.shape[1]

    w_in_p = jnp.pad(w_in, ((0, 0), (0, 0), (0, D_IN_PAD - w_in.shape[2]))).astype(BF16)
    w_kvt = w_in[:, :, _OFF_K:_OFF_XL].transpose(0, 2, 1).astype(BF16)
    w_al = jnp.pad(gla_w_alpha, ((0, 0), (0, LANES - GLA_RANK), (0, 0))).astype(BF16)
    w_out_b = w_out.astype(BF16)
    wa_bd = jax.vmap(_block_diag)(lru_wa).astype(BF16)
    wi_bd = jax.vmap(_block_diag)(lru_wi).astype(BF16)
    gq = jnp.tile(g_qnorm, (1, SB_HEADS))[:, None, :]
    gk = jnp.tile(g_knorm, (1, SB_HEADS))[:, None, :]
    gk_col = jnp.tile(g_knorm, (1, SB_HEADS))[:, :, None]
    lane = jnp.arange(HEAD_MEAN_WIDTH)
    e64 = ((lane[:, None] // HEAD_DIM == lane[None, :] // HEAD_DIM).astype(F32) / HEAD_DIM).astype(BF16)
    vec = lambda t, l: t[l][None, :]

    cache_kt = cache_k.transpose(0, 1, 3, 4, 2).reshape(depth, n_phys, SB_W, PAGE_SIZE)
    cache_vt = cache_v.transpose(0, 1, 3, 4, 2).reshape(depth, n_phys, SB_W, PAGE_SIZE)

    xp = x_prompt.reshape(n_p, D_MODEL)
    xs = x_sample.reshape(bs, D_MODEL)
    tm_p = _row_tile(n_p)
    tm_s = _row_tile(bs)
    kt_all = jnp.zeros((depth, bp, SB_W, seq), F32)
    vt_all = jnp.zeros((depth, bp, SB_W, seq), F32)
    outs = {name: [] for name in ("ks", "vs", "hp", "hs", "cp", "cs", "sp", "ss")}

    for l in range(depth):
        lru_w = (conv_w[l], vec(conv_b, l), wa_bd[l], vec(lru_ba, l), wi_bd[l], vec(lru_bi, l),
                 vec(lru_lambda, l))
        inproj_w = (vec(g_mix, l), w_in_p[l], gq[l])
        inproj_w2 = (w_al[l], vec(gla_b_alpha, l), e64)

        ffn1_w = (vec(g_ffn1, l), w_ffn1_gate, w_ffn1_up, w_ffn1_down, l)
        ffn2_w = (vec(g_ffn2, l), w_ffn2_gate, w_ffn2_up, w_ffn2_down, l)
        xp, xs = _ffn(xp, xs, *ffn1_w, tm_p)

        q, kt_all, vt_all, xl, gate, qg, kg, vg, rg, la = _inproj(
            xp, *inproj_w, gk_col[l], *inproj_w2, tm_p, (kt_all, vt_all), l, w_kvt[l], seq)
        osb = _sb_prompt(q, kt_all, vt_all, sb_bias[l], l)
        olru, h_last, conv_new = _lru_prompt(xl, *lru_w, bp, seq)
        ogla, s_fin = _gla_prompt(qg, kg, vg, la, bp, seq)
        xp = _merge(xp, osb, olru, ogla, gate, rg, vec(g_mix_out, l), e64, w_out_b[l], tm_p)
        outs["hp"].append(h_last.reshape(bp, LRU_W))
        outs["cp"].append(conv_new)
        outs["sp"].append(s_fin.reshape(bp, GLA_HEADS, GLA_DK, GLA_DV))

        q, k_new, v_new, xl, gate, qg, kg, vg, rg, la = _inproj(xs, *inproj_w, gk[l], *inproj_w2, tm_s)
        bias_b = jnp.broadcast_to(sb_bias[l][:, None], (SB_HEADS, PAGE_SIZE))
        osb = _sb_decode(q.reshape(bs, SB_W, 1), bias_b, cache_kt, cache_vt, page_table, l)
        h_new, conv_new, ogla, s_new = _step(
            xl, state_lru_conv[l].transpose(1, 0, 2), state_lru_h[l], *lru_w,
            qg.reshape(bs, GLA_QK, 1), kg.reshape(bs, GLA_QK, 1), la.reshape(bs, GLA_QK, 1),
            vg.reshape(bs, GLA_HEADS, GLA_DV), state_gla[l].reshape(bs, GLA_QK, GLA_DV))
        xs = _merge(xs, osb.reshape(bs, SB_W), h_new, ogla.reshape(bs, GLA_W), gate, rg,
                    vec(g_mix_out, l), e64, w_out_b[l], tm_s)
        xp, xs = _ffn(xp, xs, *ffn2_w, tm_p)
        outs["ks"].append(k_new.reshape(bs, 1, SB_HEADS, HEAD_DIM))
        outs["vs"].append(v_new.reshape(bs, 1, SB_HEADS, HEAD_DIM))
        outs["hs"].append(h_new)
        outs["cs"].append(conv_new.transpose(1, 0, 2))
        outs["ss"].append(s_new.reshape(bs, GLA_HEADS, GLA_DK, GLA_DV))

    st = lambda name: jnp.stack(outs[name])
    return (xp.reshape(bp, seq, D_MODEL), xs.reshape(bs, 1, D_MODEL),
            kt_all.reshape(depth, bp, SB_HEADS, HEAD_DIM, seq).transpose(0, 1, 4, 2, 3),
            vt_all.reshape(depth, bp, SB_HEADS, HEAD_DIM, seq).transpose(0, 1, 4, 2, 3),
            st("ks"), st("vs"), st("hp"), st("hs"), st("cp"), st("cs"), st("sp"), st("ss"))
```

```python
import functools

import jax
import jax.numpy as jnp
from jax import lax
from jax.experimental import pallas as pl
from jax.experimental.pallas import tpu as pltpu

F32 = jnp.float32
BF16 = jnp.bfloat16

D_MODEL = 1024
HEAD_DIM = 64
SB_W = D_MODEL // 2
SB_HEADS = SB_W // HEAD_DIM
LRU_W = D_MODEL // 4
LRU_BLOCKS = LRU_W // HEAD_DIM
LRU_CONV = 4
LRU_C = 8.0
GLA_W = D_MODEL // 4
GLA_HEADS = GLA_W // HEAD_DIM
GLA_DV = HEAD_DIM
GLA_DK = HEAD_DIM // 2
GLA_QK = GLA_HEADS * GLA_DK
GLA_RANK = 16
GLA_TAU = 16.0
D_MIX = SB_W + LRU_W + GLA_W
D_FF = 2816
EPS = 1e-6
PAGE_SIZE = 128

LANES = 128
SUBLANES = 8
VMEM_LIMIT_BYTES = 56 * 1024 * 1024

_OFF_Q = 0
_OFF_K = _OFF_Q + SB_W
_OFF_V = _OFF_K + SB_W
_OFF_XL = _OFF_V + SB_W
_OFF_GATE = _OFF_XL + LRU_W
_OFF_QG = _OFF_GATE + LRU_W
_OFF_KG = _OFF_QG + GLA_QK
_OFF_VG = _OFF_KG + GLA_QK
_OFF_RG = _OFF_VG + GLA_W
_OFF_ALR = _OFF_RG + GLA_W
D_IN_PAD = _OFF_ALR + LANES

FF_CHUNK = 256
HEAD_MEAN_WIDTH = 256
ROW_TILE = 512
SB_BLOCK = 256
SB_BLOCKS_PER_ITER = 2
GLA_CHUNK = 128
GLA_CHUNKS_PER_STEP = 4
DECODE_PAGES_PER_STEP = 32


def _idiv(x, d):
    assert d & (d - 1) == 0
    return x >> (d.bit_length() - 1)


def _imod(x, d):
    assert d & (d - 1) == 0
    return x & (d - 1)


def _dot(a, b):
    return jnp.dot(a, b, preferred_element_type=F32)


def _dot_nt(a, b):
    return lax.dot_general(a, b, (((1,), (1,)), ((), ())), preferred_element_type=F32)


def _dot_tn(a, b):
    return lax.dot_general(a, b, (((0,), (0,)), ((), ())), preferred_element_type=F32)


def _split_dot(x, w, passes):
    hi = x.astype(BF16)
    acc = _dot(hi, w)
    rem = x - hi.astype(F32)
    for _ in range(passes - 1):
        lo = rem.astype(BF16)
        acc = acc + _dot(lo, w)
        rem = rem - lo.astype(F32)
    return acc


def _split_dot_left(w, x, passes):
    hi = x.astype(BF16)
    acc = _dot(w, hi)
    rem = x - hi.astype(F32)
    for _ in range(passes - 1):
        lo = rem.astype(BF16)
        acc = acc + _dot(w, lo)
        rem = rem - lo.astype(F32)
    return acc


def _rms(x, g):
    ms = jnp.mean(x * x, axis=-1, keepdims=True)
    return x * lax.rsqrt(ms + EPS) * g


def _softplus(z):
    return jnp.maximum(z, 0.0) + jnp.log1p(jnp.exp(-jnp.abs(z)))


def _log_sigmoid(z):
    return jnp.minimum(z, 0.0) - jnp.log1p(jnp.exp(-jnp.abs(z)))


def _gelu_tanh(x):
    return 0.5 * x * (1.0 + jnp.tanh(0.7978845608028654 * (x + 0.044715 * (x * x * x))))


def _head_rms_cols(x, e64, g):
    cw = e64.shape[0]
    cols = []
    for c in range(x.shape[1] // cw):
        xc = x[:, c * cw:(c + 1) * cw]
        ms = _split_dot(xc * xc, e64, 2)
        cols.append(xc * lax.rsqrt(ms + EPS) * g[:, c * cw:(c + 1) * cw])
    return cols


def _resident(shape):
    nd = len(shape)
    return pl.BlockSpec(shape, lambda *_: (0,) * nd, pipeline_mode=pl.Buffered(1))


def _layer_resident(shape, layer):
    nd = len(shape)
    return pl.BlockSpec((None,) + tuple(shape), lambda *_: (layer,) + (0,) * nd, pipeline_mode=pl.Buffered(1))


def _params(semantics):
    return pltpu.CompilerParams(dimension_semantics=semantics, vmem_limit_bytes=VMEM_LIMIT_BYTES)


def _ffn_rows(x, g_ref, wg_ref, wu_ref, wd_ref, h_ref, acc_ref):
    h_ref[...] = _rms(x, g_ref[...]).astype(BF16)
    acc_ref[...] = jnp.zeros_like(acc_ref)

    def body(c, carry):
        cols = pl.ds(pl.multiple_of(c * FF_CHUNK, FF_CHUNK), FF_CHUNK)
        h = h_ref[...]
        gate = _dot(h, wg_ref[:, cols].astype(BF16))
        up = _dot(h, wu_ref[:, cols].astype(BF16))
        act = (gate * jax.nn.sigmoid(gate) * up).astype(BF16)
        acc_ref[...] += _dot(act, wd_ref[cols, :].astype(BF16))
        return carry

    lax.fori_loop(0, D_FF // FF_CHUNK, body, 0, unroll=True)
    return x + 0.5 * acc_ref[...]


def _ffn_kernel(xp_ref, xs_ref, g_ref, wg_ref, wu_ref, wd_ref, op_ref, os_ref, h_ref, acc_ref):
    weights = (g_ref, wg_ref, wu_ref, wd_ref)
    op_ref[...] = _ffn_rows(xp_ref[...], *weights, h_ref, acc_ref)

    @pl.when(pl.program_id(0) == pl.num_programs(0) - 1)
    def _sample_rows():
        rows = xs_ref.shape[0]
        os_ref[...] = _ffn_rows(xs_ref[...], *weights, h_ref.at[:rows], acc_ref.at[:rows])


def _ffn(xp, xs, g, wg, wu, wd, layer, tm):
    n = xp.shape[0]
    ns = xs.shape[0]
    assert ns <= tm
    return pl.pallas_call(
        _ffn_kernel,
        grid=(n // tm,),
        in_specs=[
            pl.BlockSpec((tm, D_MODEL), lambda i: (i, 0)),
            _resident((ns, D_MODEL)),
            _resident((1, D_MODEL)),
            _layer_resident((D_MODEL, D_FF), layer),
            _layer_resident((D_MODEL, D_FF), layer),
            _layer_resident((D_FF, D_MODEL), layer),
        ],
        out_specs=[pl.BlockSpec((tm, D_MODEL), lambda i: (i, 0)),
                   pl.BlockSpec((ns, D_MODEL), lambda i: (0, 0))],
        out_shape=[jax.ShapeDtypeStruct((n, D_MODEL), F32), jax.ShapeDtypeStruct((ns, D_MODEL), F32)],
        scratch_shapes=[pltpu.VMEM((tm, D_MODEL), BF16), pltpu.VMEM((tm, D_MODEL), F32)],
        compiler_params=_params(("arbitrary",)),
        name="ffn",
    )(xp, xs, g, wg, wu, wd)


def _inproj_kernel(*refs, transposed_kv):
    if transposed_kv:
        (x_ref, g_ref, w_ref, gq_ref, gk_ref, wal_ref, bal_ref, e64_ref, wkvt_ref,
         q_ref, k_ref, v_ref, xl_ref, gate_ref, qg_ref, kg_ref, vg_ref, rg_ref, la_ref, h_ref) = refs
    else:
        (x_ref, g_ref, w_ref, gq_ref, gk_ref, wal_ref, bal_ref, e64_ref,
         q_ref, k_ref, v_ref, xl_ref, gate_ref, qg_ref, kg_ref, vg_ref, rg_ref, la_ref, h_ref) = refs
    h_ref[...] = _rms(x_ref[...], g_ref[...]).astype(BF16)

    def proj(lo, width):
        return _dot(h_ref[...], w_ref[:, lo:lo + width])

    e64 = e64_ref[...]
    q_cols = _head_rms_cols(proj(_OFF_Q, SB_W), e64, gq_ref[...])
    cw = e64.shape[0]
    for c, col in enumerate(q_cols):
        q_ref[:, c * cw:(c + 1) * cw] = col * (HEAD_DIM ** -0.5)
    if transposed_kv:
        rows = h_ref.shape[0]
        k_t = _dot_nt(wkvt_ref[:SB_W, :], h_ref[...]).reshape(SB_HEADS, HEAD_DIM, rows)
        ms = jnp.mean(k_t * k_t, axis=1, keepdims=True)
        k_ref[...] = (k_t * lax.rsqrt(ms + EPS)).reshape(SB_W, rows) * gk_ref[...]
        v_ref[...] = _dot_nt(wkvt_ref[SB_W:, :], h_ref[...])
    else:
        k_cols = _head_rms_cols(proj(_OFF_K, SB_W), e64, gk_ref[...])
        for c, col in enumerate(k_cols):
            k_ref[:, c * cw:(c + 1) * cw] = col
        v_ref[...] = proj(_OFF_V, SB_W)
    xl_ref[...] = proj(_OFF_XL, LRU_W)
    gate_ref[...] = _gelu_tanh(proj(_OFF_GATE, LRU_W))
    qg_ref[...] = proj(_OFF_QG, GLA_QK)
    kg_ref[...] = proj(_OFF_KG, GLA_QK)
    vg_ref[...] = proj(_OFF_VG, GLA_W)
    r = proj(_OFF_RG, GLA_W)
    rg_ref[...] = r * jax.nn.sigmoid(r)
    a_lr = proj(_OFF_ALR, LANES)
    xa = _dot(a_lr.astype(BF16), wal_ref[...]) + bal_ref[...]
    la_ref[...] = _log_sigmoid(xa) * (1.0 / GLA_TAU)


def _inproj(x, g, w_in, gq, gk, wal, bal, e64, tm, kv_slabs=None, layer=None, w_kvt=None, seq=None):
    n = x.shape[0]
    row = lambda w: pl.BlockSpec((tm, w), lambda i: (i, 0))
    in_specs = [
        row(D_MODEL),
        _resident((1, D_MODEL)),
        _resident((D_MODEL, D_IN_PAD)),
        _resident((1, SB_W)),
        _resident(gk.shape),
        _resident((LANES, LANES)),
        _resident((1, LANES)),
        _resident(e64.shape),
    ]
    args = [x, g, w_in, gq, gk, wal, bal, e64]
    small = lambda w: jax.ShapeDtypeStruct((n, w), F32)
    transposed_kv = kv_slabs is not None
    if not transposed_kv:
        kv_specs = [row(SB_W), row(SB_W)]
        kv_shapes = [small(SB_W), small(SB_W)]
        aliases = {}
        kernel = functools.partial(_inproj_kernel, transposed_kv=False)
    else:
        k_all, v_all = kv_slabs
        tiles = seq // tm
        slab = pl.BlockSpec((None, None, SB_W, tm), lambda i: (layer, i // tiles, 0, i % tiles))
        kv_specs = [slab, slab]
        kv_shapes = [jax.ShapeDtypeStruct(k_all.shape, F32), jax.ShapeDtypeStruct(v_all.shape, F32)]
        in_specs += [_resident((2 * SB_W, D_MODEL)),
                     pl.BlockSpec(memory_space=pl.ANY), pl.BlockSpec(memory_space=pl.ANY)]
        args += [w_kvt, k_all, v_all]
        aliases = {len(args) - 2: 1, len(args) - 1: 2}

        def kernel(*refs):
            _inproj_kernel(*refs[:9], *refs[11:], transposed_kv=True)

    out_specs = [row(SB_W)] + kv_specs + [row(LRU_W), row(LRU_W), row(GLA_QK), row(GLA_QK),
                                          row(GLA_W), row(GLA_W), row(GLA_QK)]
    out_shape = [small(SB_W)] + kv_shapes + [small(LRU_W), small(LRU_W), small(GLA_QK), small(GLA_QK),
                                             small(GLA_W), small(GLA_W), small(GLA_QK)]
    return pl.pallas_call(
        kernel,
        grid=(n // tm,),
        in_specs=in_specs,
        out_specs=out_specs,
        out_shape=out_shape,
        input_output_aliases=aliases,
        scratch_shapes=[pltpu.VMEM((tm, D_MODEL), BF16)],
        compiler_params=_params(("parallel",)),
        name="inproj",
    )(*args)


def _merge_kernel(x_ref, osb_ref, olru_ref, ogla_ref, gate_ref, rg_ref, gmo_ref, e64_ref, wout_ref, o_ref):
    e64 = e64_ref[...]
    gmo = gmo_ref[...]
    sb = _head_rms_cols(osb_ref[...], e64, gmo[:, :SB_W])
    lru = _head_rms_cols(olru_ref[...], e64, gmo[:, SB_W:SB_W + LRU_W])
    gla = _head_rms_cols(ogla_ref[...], e64, gmo[:, SB_W + LRU_W:])
    gate = gate_ref[...]
    rg = rg_ref[...]
    cw = e64.shape[0]
    lru = [col * gate[:, c * cw:(c + 1) * cw] for c, col in enumerate(lru)]
    gla = [col * rg[:, c * cw:(c + 1) * cw] for c, col in enumerate(gla)]
    acc = x_ref[...]
    for c, col in enumerate(sb + lru + gla):
        acc = acc + _dot(col.astype(BF16), wout_ref[c * cw:(c + 1) * cw, :])
    o_ref[...] = acc


def _merge(x, osb, olru, ogla, gate, rg, gmo, e64, wout, tm):
    n = x.shape[0]
    row = lambda w: pl.BlockSpec((tm, w), lambda i: (i, 0))
    return pl.pallas_call(
        _merge_kernel,
        grid=(n // tm,),
        in_specs=[row(D_MODEL), row(SB_W), row(LRU_W), row(GLA_W), row(LRU_W), row(GLA_W),
                  _resident((1, D_MIX)), _resident(e64.shape), _resident((D_MIX, D_MODEL))],
        out_specs=row(D_MODEL),
        out_shape=jax.ShapeDtypeStruct((n, D_MODEL), F32),
        compiler_params=_params(("parallel",)),
        name="merge",
    )(x, osb, olru, ogla, gate, rg, gmo, e64, wout)


_LOG2_E = 1.4426950408889634


def _softplus_plain(z):
    return jnp.maximum(z, 0.0) + jnp.log(1.0 + jnp.exp2(jnp.abs(z) * (-_LOG2_E)))


def _sb_tile_stages(q_h, kt_b, vt_h, upper, tail, mask, tail_out=None):
    st = {}

    def logits():
        st["z"] = _dot(q_h, kt_b)

    def log_terms():
        z = st["z"]
        sp = _softplus_plain(z)
        st["lb"] = z - sp
        if mask is not None:
            sp = jnp.where(mask, sp, 0.0)
        st["edge"] = sp[:, 0:1]
        st["sp"] = sp.astype(BF16)

    def suffix_sums():
        st["within"] = _dot(st["sp"], upper)

    def weights():
        within = st["within"]
        tail_in = tail[0] if isinstance(tail, list) else tail
        a = jnp.exp(st["lb"] - within - tail_in)
        if mask is not None:
            a = jnp.where(mask, a, 0.0)
        st["a"] = a.astype(BF16)
        st["tail"] = tail_in + within[:, 0:1] + st["edge"]
        if tail_out is not None:
            tail_out.append(st["tail"])

    def values():
        return _dot_nt(st["a"], vt_h), st["tail"]

    return [logits, log_terms, suffix_sums, weights, values]


def _run_skewed(pipelines):
    depth = len(pipelines[0])
    results = [None] * len(pipelines)
    for step in range(len(pipelines) + depth - 1):
        for p, stages in enumerate(pipelines):
            s = step - p
            if 0 <= s < depth:
                results[p] = stages[s]()
    return results


def _sb_prompt_kernel(bias_ref, q_ref, kt_ref, vt_ref, o_ref, kh_ref, vh_ref, *, blk):
    pairs = SB_W // LANES
    i = pl.program_id(1)
    lane = lax.broadcasted_iota(jnp.int32, (1, LANES), 1)
    first = lane < HEAD_DIM

    @pl.when(i == 0)
    def _stage():
        row = lax.broadcasted_iota(jnp.int32, (LANES, vt_ref.shape[1]), 0)
        first_rows = row < HEAD_DIM
        ones_first = jnp.where(row < 2, 1.0, 0.0)
        ones_second = jnp.where(first_rows, 0.0, jnp.where(row < HEAD_DIM + 2, 1.0, 0.0))
        for p in range(pairs):
            k = kt_ref[p * LANES:(p + 1) * LANES, :]
            kh_ref[2 * p] = jnp.where(first_rows, k, ones_second).astype(BF16)
            kh_ref[2 * p + 1] = jnp.where(first_rows, ones_first, k).astype(BF16)
            v = vt_ref[p * LANES:(p + 1) * LANES, :]
            vh_ref[2 * p] = jnp.where(first_rows, v, 0.0).astype(BF16)
            vh_ref[2 * p + 1] = jnp.where(first_rows, 0.0, v).astype(BF16)

    def bias_lanes(h, lane0):
        b = jnp.full((1, LANES), bias_ref[h], F32)
        hi = b.astype(BF16).astype(F32)
        return jnp.where(lane == lane0, hi, jnp.where(lane == lane0 + 1, b - hi, 0.0))

    q_heads = []
    for p in range(pairs):
        q = q_ref[:, p * LANES:(p + 1) * LANES]
        q_heads += [jnp.where(first, q, bias_lanes(2 * p, HEAD_DIM)).astype(BF16),
                    jnp.where(first, bias_lanes(2 * p + 1, 0), q).astype(BF16)]
    r_i = lax.broadcasted_iota(jnp.int32, (blk, blk), 0)
    c_i = lax.broadcasted_iota(jnp.int32, (blk, blk), 1)
    upper = jnp.where(r_i > c_i, 1.0, 0.0).astype(BF16)
    causal = c_i < r_i

    def tiles(j, n_blocks, tails, accs, mask):
        pipelines = []
        for b in range(n_blocks):
            cols = pl.ds(pl.multiple_of((j - b) * blk, blk), blk)
            handed_on = [[] for _ in range(2 * pairs)]
            for h in range(2 * pairs):
                pipelines.append(_sb_tile_stages(q_heads[h], kh_ref[h, :, cols], vh_ref[h, :, cols], upper,
                                                 tails[h], mask, handed_on[h]))
            tails = handed_on
        results = _run_skewed(pipelines)
        new_accs = list(accs)
        for p, (out, _) in enumerate(results):
            h = p % (2 * pairs)
            new_accs[h // 2] = new_accs[h // 2] + out
        return tuple(t for _, t in results[-2 * pairs:]), tuple(new_accs)

    zero_c = jnp.zeros((blk, 1), F32)
    zero_a = jnp.zeros((blk, LANES), F32)
    state = tiles(i, 1, (zero_c,) * (2 * pairs), (zero_a,) * pairs, causal)

    def body(it, state):
        return tiles(i - 1 - SB_BLOCKS_PER_ITER * it, SB_BLOCKS_PER_ITER, state[0], state[1], None)

    state = lax.fori_loop(0, i // SB_BLOCKS_PER_ITER, body, state)
    for left in range(1, SB_BLOCKS_PER_ITER):
        state = lax.cond(i % SB_BLOCKS_PER_ITER == left,
                         lambda st, left=left: tiles(left - 1, left, st[0], st[1], None),
                         lambda st: st, state)
    _, accs = state
    for p in range(pairs):
        o_ref[:, p * LANES:(p + 1) * LANES] = accs[p]


def _sb_prompt(q, kt_all, vt_all, bias, layer):
    _, batch, _, seq = kt_all.shape
    blk = min(SB_BLOCK, seq)
    nq = seq // blk
    kv_spec = pl.BlockSpec((None, None, SB_W, seq), lambda b, i, *_: (layer, b, 0, 0))
    return pl.pallas_call(
        functools.partial(_sb_prompt_kernel, blk=blk),
        grid_spec=pltpu.PrefetchScalarGridSpec(
            num_scalar_prefetch=1,
            grid=(batch, nq),
            in_specs=[pl.BlockSpec((blk, SB_W), lambda b, i, *_: (b * nq + i, 0)), kv_spec, kv_spec],
            out_specs=pl.BlockSpec((blk, SB_W), lambda b, i, *_: (b * nq + i, 0)),
            scratch_shapes=[pltpu.VMEM((SB_HEADS, LANES, seq), BF16)] * 2,
        ),
        out_shape=jax.ShapeDtypeStruct((batch * seq, SB_W), F32),
        compiler_params=_params(("parallel", "arbitrary")),
        name="sb_prompt",
    )(bias, q, kt_all, vt_all)


def _sb_decode_kernel(pt_ref, q_ref, bias_ref, *refs, n_pages, group):
    k_refs = refs[:group]
    v_refs = refs[group:2 * group]
    o_ref = refs[2 * group]
    z_ref, a_ref, acc_ref = refs[2 * group + 1:]
    s = pl.program_id(1)
    steps = n_pages // group
    rows = n_pages * SB_HEADS

    @pl.when(s < steps)
    def _scores():
        q_col = q_ref[...]
        for g in range(group):
            prod = (k_refs[g][...] * q_col).reshape(SB_HEADS, HEAD_DIM, PAGE_SIZE)
            z = jnp.sum(prod, axis=1) + bias_ref[...]
            row0 = pl.multiple_of((s * group + g) * SB_HEADS, SB_HEADS)
            z_ref[pl.ds(row0, SB_HEADS), :] = z

    @pl.when(s == steps - 1)
    def _weights():
        z = z_ref[...]
        sp = _softplus(z)
        l1m = -sp
        r_i = lax.broadcasted_iota(jnp.int32, (PAGE_SIZE, PAGE_SIZE), 0)
        c_i = lax.broadcasted_iota(jnp.int32, (PAGE_SIZE, PAGE_SIZE), 1)
        upper = (r_i > c_i).astype(BF16)
        ones = jnp.ones((PAGE_SIZE, PAGE_SIZE), BF16)
        within = _split_dot(l1m, upper, 3)
        page_total = _split_dot(l1m, ones, 3)
        pr = lax.broadcasted_iota(jnp.int32, (rows, rows), 0)
        pc = lax.broadcasted_iota(jnp.int32, (rows, rows), 1)
        same_head = _imod(pc, SB_HEADS) == _imod(pr, SB_HEADS)
        later_page = jnp.where(_idiv(pc, SB_HEADS) > _idiv(pr, SB_HEADS),
                               jnp.where(same_head, 1.0, 0.0), 0.0).astype(BF16)
        later = _split_dot_left(later_page, page_total, 3)
        a_ref[...] = jnp.exp(z - sp + within + later)
        acc_ref[...] = jnp.zeros_like(acc_ref)

    @pl.when(s >= steps)
    def _values():
        acc = acc_ref[...]
        for g in range(group):
            row0 = pl.multiple_of(((s - steps) * group + g) * SB_HEADS, SB_HEADS)
            a_g = a_ref[pl.ds(row0, SB_HEADS), :]
            a_rows = jnp.concatenate(
                [jnp.broadcast_to(a_g[h:h + 1, :], (HEAD_DIM, PAGE_SIZE)) for h in range(SB_HEADS)], axis=0)
            acc = acc + v_refs[g][...] * a_rows
        acc_ref[...] = acc

    @pl.when(s == 2 * steps - 1)
    def _emit():
        o_ref[...] = jnp.sum(acc_ref[...], axis=1, keepdims=True)


def _sb_decode(q_col, bias_b, cache_kt, cache_vt, page_table, layer):
    b, n_pages = page_table.shape
    group = min(DECODE_PAGES_PER_STEP, n_pages)
    steps = n_pages // group

    def k_map(g):
        return lambda bi, s, pt: (layer, pt[bi, jnp.minimum(s, steps - 1) * group + g], 0, 0)

    def v_map(g):
        def index(bi, s, pt):
            in_values = s >= steps
            row = jnp.where(in_values, bi, jnp.maximum(bi - 1, 0))
            col = jnp.where(in_values, s - steps, steps - 1) * group + g
            return (layer, pt[row, col], 0, 0)
        return index

    page = lambda m: pl.BlockSpec((None, None, SB_W, PAGE_SIZE), m)
    in_specs = [pl.BlockSpec((None, SB_W, 1), lambda bi, s, pt: (bi, 0, 0)),
                pl.BlockSpec((SB_HEADS, PAGE_SIZE), lambda bi, s, pt: (0, 0))]
    in_specs += [page(k_map(g)) for g in range(group)]
    in_specs += [page(v_map(g)) for g in range(group)]
    rows = n_pages * SB_HEADS
    return pl.pallas_call(
        functools.partial(_sb_decode_kernel, n_pages=n_pages, group=group),
        grid_spec=pltpu.PrefetchScalarGridSpec(
            num_scalar_prefetch=1,
            grid=(b, 2 * steps),
            in_specs=in_specs,
            out_specs=pl.BlockSpec((None, SB_W, 1), lambda bi, s, pt: (bi, 0, 0)),
            scratch_shapes=[pltpu.VMEM((rows, PAGE_SIZE), F32),
                            pltpu.VMEM((rows, PAGE_SIZE), F32),
                            pltpu.VMEM((SB_W, PAGE_SIZE), F32)],
        ),
        out_shape=jax.ShapeDtypeStruct((b, SB_W, 1), F32),
        compiler_params=_params(("parallel", "arbitrary")),
        name="sb_decode",
    )(page_table, q_col, bias_b, *([cache_kt] * group), *([cache_vt] * group))


def _lru_gates(xc, wa_ref, ba_ref, wi_ref, bi_ref, lam_ref):
    xb = xc.astype(BF16)
    r = jax.nn.sigmoid(_dot(xb, wa_ref[...]) + ba_ref[...])
    i = jax.nn.sigmoid(_dot(xb, wi_ref[...]) + bi_ref[...])
    log_a = -LRU_C * r * _softplus(-lam_ref[...])
    a = jnp.exp(log_a)
    t = jnp.tanh(log_a)
    one_minus_a2 = -2.0 * t / (1.0 - t)
    return a, jnp.sqrt(one_minus_a2) * (i * xc)


def _lru_prompt_kernel(x_ref, cw_ref, cb_ref, wa_ref, ba_ref, wi_ref, bi_ref, lam_ref,
                       o_ref, hl_ref, cn_ref, *, seq):
    x = x_ref[...]
    rows = lax.broadcasted_iota(jnp.int32, (seq, LRU_W), 0)
    xc = cb_ref[...] + cw_ref[LRU_CONV - 1:LRU_CONV, :] * x
    for j in range(1, LRU_CONV):
        shifted = jnp.where(rows >= j, pltpu.roll(x, j, 0), 0.0)
        xc = xc + cw_ref[LRU_CONV - 1 - j:LRU_CONV - j, :] * shifted
    a, g = _lru_gates(xc, wa_ref, ba_ref, wi_ref, bi_ref, lam_ref)
    step = 1
    while step < seq:
        valid = rows >= step
        a_prev = pltpu.roll(a, step, 0)
        g_prev = pltpu.roll(g, step, 0)
        g = jnp.where(valid, a * g_prev + g, g)
        a = jnp.where(valid, a * a_prev, a)
        step *= 2
    o_ref[...] = g
    hl_ref[...] = g[seq - 1:seq, :]
    cn_ref[...] = x[seq - (LRU_CONV - 1):, :]


def _lru_prompt(xl, cw, cb, wa, ba, wi, bi, lam, batch, seq):
    vec = _resident((1, LRU_W))
    mat = _resident((LRU_W, LRU_W))
    return pl.pallas_call(
        functools.partial(_lru_prompt_kernel, seq=seq),
        grid=(batch,),
        in_specs=[pl.BlockSpec((seq, LRU_W), lambda b: (b, 0)), _resident((LRU_CONV, LRU_W)), vec,
                  mat, vec, mat, vec, vec],
        out_specs=[pl.BlockSpec((seq, LRU_W), lambda b: (b, 0)),
                   pl.BlockSpec((None, 1, LRU_W), lambda b: (b, 0, 0)),
                   pl.BlockSpec((None, LRU_CONV - 1, LRU_W), lambda b: (b, 0, 0))],
        out_shape=[jax.ShapeDtypeStruct((batch * seq, LRU_W), F32),
                   jax.ShapeDtypeStruct((batch, 1, LRU_W), F32),
                   jax.ShapeDtypeStruct((batch, LRU_CONV - 1, LRU_W), F32)],
        compiler_params=_params(("parallel",)),
        name="lru_prompt",
    )(xl, cw, cb, wa, ba, wi, bi, lam)


def _block_mid_rows(bc, level):
    n_rows, width = bc.shape
    half = level // 2
    if level >= 2 * SUBLANES:
        return jnp.concatenate(
            [jnp.broadcast_to(bc[b * level + half - 1:b * level + half, :], (level, width))
             for b in range(n_rows // level)], axis=0)
    groups = n_rows // SUBLANES
    bc3 = bc.reshape(groups, SUBLANES, width)
    sub = lax.broadcasted_iota(jnp.int32, (groups, SUBLANES, width), 1)
    mid = None
    for b in range(SUBLANES // level):
        piece = jnp.broadcast_to(bc3[:, b * level + half - 1:b * level + half, :], bc3.shape)
        mid = piece if mid is None else jnp.where(_idiv(sub, level) == b, piece, mid)
    return mid.reshape(n_rows, width)


def _gla_chunk(qs, k, v, la, state):
    chunk = qs.shape[0]
    r_i = lax.broadcasted_iota(jnp.int32, (chunk, chunk), 0)
    c_i = lax.broadcasted_iota(jnp.int32, (chunk, chunk), 1)
    lower = (c_i <= r_i).astype(BF16)
    bc = _split_dot_left(lower, la, 3)
    rows = lax.broadcasted_iota(jnp.int32, (chunk, GLA_QK), 0)
    qk_head = _idiv(lax.broadcasted_iota(jnp.int32, (1, GLA_QK), 1), GLA_DK)
    v_head = _idiv(lax.broadcasted_iota(jnp.int32, (1, GLA_W), 1), GLA_DV)
    own = (_idiv(lax.broadcasted_iota(jnp.int32, (GLA_QK, GLA_W), 0), GLA_DK)
           == _idiv(lax.broadcasted_iota(jnp.int32, (GLA_QK, GLA_W), 1), GLA_DV))

    out = _dot((qs * jnp.exp(bc)).astype(BF16), state.astype(BF16))

    scores = [jnp.zeros((chunk, chunk), F32) for _ in range(GLA_HEADS)]
    level = 2
    while level <= chunk:
        half = level // 2
        mid = _block_mid_rows(bc, level)
        second = (rows & (level - 1)) >= half
        q_dec = jnp.where(second, qs * jnp.exp(bc - mid), 0.0)
        k_dec = jnp.where(second, 0.0, k * jnp.exp(mid - bc)).astype(BF16)
        same_block = _idiv(r_i, level) == _idiv(c_i, level)
        for h in range(GLA_HEADS):
            sc = _dot_nt(jnp.where(qk_head == h, q_dec, 0.0).astype(BF16), k_dec)
            if level < chunk:
                sc = jnp.where(same_block, sc, 0.0)
            scores[h] = scores[h] + sc
        level *= 2
    for h in range(GLA_HEADS):
        out = out + _dot(scores[h].astype(BF16), jnp.where(v_head == h, v, 0.0).astype(BF16))

    expand = jnp.where(own, 1.0, 0.0).astype(BF16)
    out = out + _split_dot(qs * k, expand, 2) * v

    last = bc[chunk - 1:chunk, :]
    kv = _dot_tn((k * jnp.exp(last - bc)).astype(BF16), v.astype(BF16))
    dec_col = jnp.transpose(jnp.broadcast_to(jnp.exp(last), (GLA_QK, GLA_QK)))
    dec = jnp.concatenate([dec_col] * (GLA_W // GLA_QK), axis=1)
    return out, dec * state + jnp.where(own, kv, 0.0)


def _gla_prompt_kernel(q_ref, k_ref, v_ref, la_ref, o_ref, sfin_ref, s_ref, *, chunk, n_steps):
    ci = pl.program_id(1)

    @pl.when(ci == 0)
    def _init():
        s_ref[...] = jnp.zeros_like(s_ref)

    state = s_ref[...]
    for u in range(q_ref.shape[0] // chunk):
        rows = slice(u * chunk, (u + 1) * chunk)
        out, state = _gla_chunk(q_ref[rows, :] * (GLA_DK ** -0.5), k_ref[rows, :], v_ref[rows, :],
                                la_ref[rows, :], state)
        o_ref[rows, :] = out
    s_ref[...] = state

    @pl.when(ci == n_steps - 1)
    def _emit():
        for h in range(GLA_HEADS):
            sfin_ref[h * GLA_DK:(h + 1) * GLA_DK, :] = (
                state[h * GLA_DK:(h + 1) * GLA_DK, h * GLA_DV:(h + 1) * GLA_DV])


def _gla_prompt(qg, kg, vg, la, batch, seq):
    chunk = min(GLA_CHUNK, seq)
    block = chunk * GLA_CHUNKS_PER_STEP if seq % (chunk * GLA_CHUNKS_PER_STEP) == 0 else chunk
    n_steps = seq // block
    row = lambda w: pl.BlockSpec((block, w), lambda b, c: (b * n_steps + c, 0))
    return pl.pallas_call(
        functools.partial(_gla_prompt_kernel, chunk=chunk, n_steps=n_steps),
        grid=(batch, n_steps),
        in_specs=[row(GLA_QK), row(GLA_QK), row(GLA_W), row(GLA_QK)],
        out_specs=[row(GLA_W), pl.BlockSpec((None, GLA_QK, GLA_DV), lambda b, c: (b, 0, 0))],
        out_shape=[jax.ShapeDtypeStruct((batch * seq, GLA_W), F32),
                   jax.ShapeDtypeStruct((batch, GLA_QK, GLA_DV), F32)],
        scratch_shapes=[pltpu.VMEM((GLA_QK, GLA_W), F32)],
        compiler_params=_params(("parallel", "arbitrary")),
        name="gla_prompt",
    )(qg, kg, vg, la)


def _step_kernel(x_ref, conv_ref, h0_ref, cw_ref, cb_ref, wa_ref, ba_ref, wi_ref, bi_ref, lam_ref,
                 qc_ref, kc_ref, lac_ref, v_ref, s0_ref,
                 h_ref, cn_ref, og_ref, s_ref):
    x = x_ref[...]
    xc = cb_ref[...] + cw_ref[LRU_CONV - 1:LRU_CONV, :] * x
    for j in range(LRU_CONV - 1):
        xc = xc + cw_ref[j:j + 1, :] * conv_ref[j]
    a, g = _lru_gates(xc, wa_ref, ba_ref, wi_ref, bi_ref, lam_ref)
    h_ref[...] = a * h0_ref[...] + g
    for j in range(LRU_CONV - 2):
        cn_ref[j] = conv_ref[j + 1]
    cn_ref[LRU_CONV - 2] = x

    alpha = jnp.exp(lac_ref[...])
    kc = kc_ref[...]
    qc = qc_ref[...] * (GLA_DK ** -0.5)
    for h in range(GLA_HEADS):
        rs = slice(h * GLA_DK, (h + 1) * GLA_DK)
        new = alpha[:, rs, :] * s0_ref[:, rs, :] + kc[:, rs, :] * v_ref[:, h:h + 1, :]
        s_ref[:, rs, :] = new
        og_ref[:, h:h + 1, :] = jnp.sum(qc[:, rs, :] * new, axis=1, keepdims=True)


def _step(xl, conv0, h0, cw, cb, wa, ba, wi, bi, lam, q_col, k_col, la_col, v3, s0):
    b = xl.shape[0]
    return pl.pallas_call(
        _step_kernel,
        out_shape=[jax.ShapeDtypeStruct((b, LRU_W), F32),
                   jax.ShapeDtypeStruct((LRU_CONV - 1, b, LRU_W), F32),
                   jax.ShapeDtypeStruct((b, GLA_HEADS, GLA_DV), F32),
                   jax.ShapeDtypeStruct((b, GLA_QK, GLA_DV), F32)],
        compiler_params=pltpu.CompilerParams(vmem_limit_bytes=VMEM_LIMIT_BYTES),
        name="sample_step",
    )(xl, conv0, h0, cw, cb, wa, ba, wi, bi, lam, q_col, k_col, la_col, v3, s0)


def _block_diag(w):
    nb, bi, bj = w.shape
    eye = jnp.eye(nb, dtype=w.dtype)
    return (eye[:, None, :, None] * w[:, :, None, :]).reshape(nb * bi, nb * bj)


def _row_tile(n):
    return min(ROW_TILE, n)


def kernel(x_prompt, x_sample, cache_k, cache_v, page_table, state_lru_h, state_lru_conv, state_gla,
           g_ffn1, w_ffn1_gate, w_ffn1_up, w_ffn1_down, g_mix, w_in, g_qnorm, g_knorm, sb_bias,
           conv_w, conv_b, lru_wa, lru_ba, lru_wi, lru_bi, lru_lambda, gla_w_alpha, gla_b_alpha,
           g_mix_out, w_out, g_ffn2, w_ffn2_gate, w_ffn2_up, w_ffn2_down):
    depth = w_in.shape[0]
    bp, seq, _ = x_prompt.shape
    bs, dec_seq, _ = x_sample.shape
    assert dec_seq == 1
    n_p = bp * seq
    n_phys = cache_k.shape[1]

    w_in_p = jnp.pad(w_in, ((0, 0), (0, 0), (0, D_IN_PAD - w_in.shape[2]))).astype(BF16)
    w_kvt = w_in[:, :, _OFF_K:_OFF_XL].transpose(0, 2, 1).astype(BF16)
    w_al = jnp.pad(gla_w_alpha, ((0, 0), (0, LANES - GLA_RANK), (0, 0))).astype(BF16)
    w_out_b = w_out.astype(BF16)
    wa_bd = jax.vmap(_block_diag)(lru_wa).astype(BF16)
    wi_bd = jax.vmap(_block_diag)(lru_wi).astype(BF16)
    gq = jnp.tile(g_qnorm, (1, SB_HEADS))[:, None, :]
    gk = jnp.tile(g_knorm, (1, SB_HEADS))[:, None, :]
    gk_col = jnp.tile(g_knorm, (1, SB_HEADS))[:, :, None]
    lane = jnp.arange(HEAD_MEAN_WIDTH)
    e64 = ((lane[:, None] // HEAD_DIM == lane[None, :] // HEAD_DIM).astype(F32) / HEAD_DIM).astype(BF16)
    vec = lambda t, l: t[l][None, :]

    cache_kt = cache_k.transpose(0, 1, 3, 4, 2).reshape(depth, n_phys, SB_W, PAGE_SIZE)
    cache_vt = cache_v.transpose(0, 1, 3, 4, 2).reshape(depth, n_phys, SB_W, PAGE_SIZE)

    xp = x_prompt.reshape(n_p, D_MODEL)
    xs = x_sample.reshape(bs, D_MODEL)
    tm_p = _row_tile(n_p)
    tm_s = _row_tile(bs)
    kt_all = jnp.zeros((depth, bp, SB_W, seq), F32)
    vt_all = jnp.zeros((depth, bp, SB_W, seq), F32)
    outs = {name: [] for name in ("ks", "vs", "hp", "hs", "cp", "cs", "sp", "ss")}

    for l in range(depth):
        lru_w = (conv_w[l], vec(conv_b, l), wa_bd[l], vec(lru_ba, l), wi_bd[l], vec(lru_bi, l),
                 vec(lru_lambda, l))
        inproj_w = (vec(g_mix, l), w_in_p[l], gq[l])
        inproj_w2 = (w_al[l], vec(gla_b_alpha, l), e64)

        ffn1_w = (vec(g_ffn1, l), w_ffn1_gate, w_ffn1_up, w_ffn1_down, l)
        ffn2_w = (vec(g_ffn2, l), w_ffn2_gate, w_ffn2_up, w_ffn2_down, l)
        xp, xs = _ffn(xp, xs, *ffn1_w, tm_p)

        q, kt_all, vt_all, xl, gate, qg, kg, vg, rg, la = _inproj(
            xp, *inproj_w, gk_col[l], *inproj_w2, tm_p, (kt_all, vt_all), l, w_kvt[l], seq)
        osb = _sb_prompt(q, kt_all, vt_all, sb_bias[l], l)
        olru, h_last, conv_new = _lru_prompt(xl, *lru_w, bp, seq)
        ogla, s_fin = _gla_prompt(qg, kg, vg, la, bp, seq)
        xp = _merge(xp, osb, olru, ogla, gate, rg, vec(g_mix_out, l), e64, w_out_b[l], tm_p)
        outs["hp"].append(h_last.reshape(bp, LRU_W))
        outs["cp"].append(conv_new)
        outs["sp"].append(s_fin.reshape(bp, GLA_HEADS, GLA_DK, GLA_DV))

        q, k_new, v_new, xl, gate, qg, kg, vg, rg, la = _inproj(xs, *inproj_w, gk[l], *inproj_w2, tm_s)
        bias_b = jnp.broadcast_to(sb_bias[l][:, None], (SB_HEADS, PAGE_SIZE))
        osb = _sb_decode(q.reshape(bs, SB_W, 1), bias_b, cache_kt, cache_vt, page_table, l)
        h_new, conv_new, ogla, s_new = _step(
            xl, state_lru_conv[l].transpose(1, 0, 2), state_lru_h[l], *lru_w,
            qg.reshape(bs, GLA_QK, 1), kg.reshape(bs, GLA_QK, 1), la.reshape(bs, GLA_QK, 1),
            vg.reshape(bs, GLA_HEADS, GLA_DV), state_gla[l].reshape(bs, GLA_QK, GLA_DV))
        xs = _merge(xs, osb.reshape(bs, SB_W), h_new, ogla.reshape(bs, GLA_W), gate, rg,
                    vec(g_mix_out, l), e64, w_out_b[l], tm_s)
        xp, xs = _ffn(xp, xs, *ffn2_w, tm_p)
        outs["ks"].append(k_new.reshape(bs, 1, SB_HEADS, HEAD_DIM))
        outs["vs"].append(v_new.reshape(bs, 1, SB_HEADS, HEAD_DIM))
        outs["hs"].append(h_new)
        outs["cs"].append(conv_new.transpose(1, 0, 2))
        outs["ss"].append(s_new.reshape(bs, GLA_HEADS, GLA_DK, GLA_DV))

    st = lambda name: jnp.stack(outs[name])
    return (xp.reshape(bp, seq, D_MODEL), xs.reshape(bs, 1, D_MODEL),
            kt_all.reshape(depth, bp, SB_HEADS, HEAD_DIM, seq).transpose(0, 1, 4, 2, 3),
            vt_all.reshape(depth, bp, SB_HEADS, HEAD_DIM, seq).transpose(0, 1, 4, 2, 3),
            st("ks"), st("vs"), st("hp"), st("hs"), st("cp"), st("cs"), st("sp"), st("ss"))
```

```python
import functools

import jax
import jax.numpy as jnp
from jax import lax
from jax.experimental import pallas as pl
from jax.experimental.pallas import tpu as pltpu

F32 = jnp.float32
BF16 = jnp.bfloat16

D_MODEL = 1024
HEAD_DIM = 64
SB_W = D_MODEL // 2
SB_HEADS = SB_W // HEAD_DIM
LRU_W = D_MODEL // 4
LRU_BLOCKS = LRU_W // HEAD_DIM
LRU_CONV = 4
LRU_C = 8.0
GLA_W = D_MODEL // 4
GLA_HEADS = GLA_W // HEAD_DIM
GLA_DV = HEAD_DIM
GLA_DK = HEAD_DIM // 2
GLA_QK = GLA_HEADS * GLA_DK
GLA_RANK = 16
GLA_TAU = 16.0
D_MIX = SB_W + LRU_W + GLA_W
D_FF = 2816
EPS = 1e-6
PAGE_SIZE = 128

LANES = 128
SUBLANES = 8
VMEM_LIMIT_BYTES = 56 * 1024 * 1024

_OFF_Q = 0
_OFF_K = _OFF_Q + SB_W
_OFF_V = _OFF_K + SB_W
_OFF_XL = _OFF_V + SB_W
_OFF_GATE = _OFF_XL + LRU_W
_OFF_QG = _OFF_GATE + LRU_W
_OFF_KG = _OFF_QG + GLA_QK
_OFF_VG = _OFF_KG + GLA_QK
_OFF_RG = _OFF_VG + GLA_W
_OFF_ALR = _OFF_RG + GLA_W
D_IN_PAD = _OFF_ALR + LANES

FF_CHUNK = 256
HEAD_MEAN_WIDTH = 256
ROW_TILE = 512
SB_BLOCK = 256
SB_BLOCKS_PER_ITER = 3
GLA_CHUNK = 128
GLA_CHUNKS_PER_STEP = 4
DECODE_PAGES_PER_STEP = 32


def _idiv(x, d):
    assert d & (d - 1) == 0
    return x >> (d.bit_length() - 1)


def _imod(x, d):
    assert d & (d - 1) == 0
    return x & (d - 1)


def _dot(a, b):
    return jnp.dot(a, b, preferred_element_type=F32)


def _dot_nt(a, b):
    return lax.dot_general(a, b, (((1,), (1,)), ((), ())), preferred_element_type=F32)


def _dot_tn(a, b):
    return lax.dot_general(a, b, (((0,), (0,)), ((), ())), preferred_element_type=F32)


def _split_dot(x, w, passes):
    hi = x.astype(BF16)
    acc = _dot(hi, w)
    rem = x - hi.astype(F32)
    for _ in range(passes - 1):
        lo = rem.astype(BF16)
        acc = acc + _dot(lo, w)
        rem = rem - lo.astype(F32)
    return acc


def _split_dot_left(w, x, passes):
    hi = x.astype(BF16)
    acc = _dot(w, hi)
    rem = x - hi.astype(F32)
    for _ in range(passes - 1):
        lo = rem.astype(BF16)
        acc = acc + _dot(w, lo)
        rem = rem - lo.astype(F32)
    return acc


def _rms(x, g):
    ms = jnp.mean(x * x, axis=-1, keepdims=True)
    return x * lax.rsqrt(ms + EPS) * g


def _softplus(z):
    return jnp.maximum(z, 0.0) + jnp.log1p(jnp.exp(-jnp.abs(z)))


def _log_sigmoid(z):
    return jnp.minimum(z, 0.0) - jnp.log1p(jnp.exp(-jnp.abs(z)))


def _gelu_tanh(x):
    return 0.5 * x * (1.0 + jnp.tanh(0.7978845608028654 * (x + 0.044715 * (x * x * x))))


def _head_rms_cols(x, e64, g):
    cw = e64.shape[0]
    cols = []
    for c in range(x.shape[1] // cw):
        xc = x[:, c * cw:(c + 1) * cw]
        ms = _split_dot(xc * xc, e64, 2)
        cols.append(xc * lax.rsqrt(ms + EPS) * g[:, c * cw:(c + 1) * cw])
    return cols


def _resident(shape):
    nd = len(shape)
    return pl.BlockSpec(shape, lambda *_: (0,) * nd, pipeline_mode=pl.Buffered(1))


def _layer_resident(shape, layer):
    nd = len(shape)
    return pl.BlockSpec((None,) + tuple(shape), lambda *_: (layer,) + (0,) * nd, pipeline_mode=pl.Buffered(1))


def _params(semantics):
    return pltpu.CompilerParams(dimension_semantics=semantics, vmem_limit_bytes=VMEM_LIMIT_BYTES)


def _ffn_rows(x, g_ref, wg_ref, wu_ref, wd_ref, h_ref, acc_ref):
    h_ref[...] = _rms(x, g_ref[...]).astype(BF16)
    acc_ref[...] = jnp.zeros_like(acc_ref)

    def body(c, carry):
        cols = pl.ds(pl.multiple_of(c * FF_CHUNK, FF_CHUNK), FF_CHUNK)
        h = h_ref[...]
        gate = _dot(h, wg_ref[:, cols].astype(BF16))
        up = _dot(h, wu_ref[:, cols].astype(BF16))
        act = (gate * jax.nn.sigmoid(gate) * up).astype(BF16)
        acc_ref[...] += _dot(act, wd_ref[cols, :].astype(BF16))
        return carry

    lax.fori_loop(0, D_FF // FF_CHUNK, body, 0, unroll=True)
    return x + 0.5 * acc_ref[...]


def _ffn_kernel(xp_ref, xs_ref, g_ref, wg_ref, wu_ref, wd_ref, op_ref, os_ref, h_ref, acc_ref):
    weights = (g_ref, wg_ref, wu_ref, wd_ref)
    op_ref[...] = _ffn_rows(xp_ref[...], *weights, h_ref, acc_ref)

    @pl.when(pl.program_id(0) == pl.num_programs(0) - 1)
    def _sample_rows():
        rows = xs_ref.shape[0]
        os_ref[...] = _ffn_rows(xs_ref[...], *weights, h_ref.at[:rows], acc_ref.at[:rows])


def _ffn(xp, xs, g, wg, wu, wd, layer, tm):
    n = xp.shape[0]
    ns = xs.shape[0]
    assert ns <= tm
    return pl.pallas_call(
        _ffn_kernel,
        grid=(n // tm,),
        in_specs=[
            pl.BlockSpec((tm, D_MODEL), lambda i: (i, 0)),
            _resident((ns, D_MODEL)),
            _resident((1, D_MODEL)),
            _layer_resident((D_MODEL, D_FF), layer),
            _layer_resident((D_MODEL, D_FF), layer),
            _layer_resident((D_FF, D_MODEL), layer),
        ],
        out_specs=[pl.BlockSpec((tm, D_MODEL), lambda i: (i, 0)),
                   pl.BlockSpec((ns, D_MODEL), lambda i: (0, 0))],
        out_shape=[jax.ShapeDtypeStruct((n, D_MODEL), F32), jax.ShapeDtypeStruct((ns, D_MODEL), F32)],
        scratch_shapes=[pltpu.VMEM((tm, D_MODEL), BF16), pltpu.VMEM((tm, D_MODEL), F32)],
        compiler_params=_params(("arbitrary",)),
        name="ffn",
    )(xp, xs, g, wg, wu, wd)


def _inproj_kernel(*refs, transposed_kv):
    if transposed_kv:
        (x_ref, g_ref, w_ref, gq_ref, gk_ref, wal_ref, bal_ref, e64_ref, wkvt_ref,
         q_ref, k_ref, v_ref, xl_ref, gate_ref, qg_ref, kg_ref, vg_ref, rg_ref, la_ref, h_ref) = refs
    else:
        (x_ref, g_ref, w_ref, gq_ref, gk_ref, wal_ref, bal_ref, e64_ref,
         q_ref, k_ref, v_ref, xl_ref, gate_ref, qg_ref, kg_ref, vg_ref, rg_ref, la_ref, h_ref) = refs
    h_ref[...] = _rms(x_ref[...], g_ref[...]).astype(BF16)

    def proj(lo, width):
        return _dot(h_ref[...], w_ref[:, lo:lo + width])

    e64 = e64_ref[...]
    q_cols = _head_rms_cols(proj(_OFF_Q, SB_W), e64, gq_ref[...])
    cw = e64.shape[0]
    for c, col in enumerate(q_cols):
        q_ref[:, c * cw:(c + 1) * cw] = col * (HEAD_DIM ** -0.5)
    if transposed_kv:
        rows = h_ref.shape[0]
        k_t = _dot_nt(wkvt_ref[:SB_W, :], h_ref[...]).reshape(SB_HEADS, HEAD_DIM, rows)
        ms = jnp.mean(k_t * k_t, axis=1, keepdims=True)
        k_ref[...] = (k_t * lax.rsqrt(ms + EPS)).reshape(SB_W, rows) * gk_ref[...]
        v_ref[...] = _dot_nt(wkvt_ref[SB_W:, :], h_ref[...])
    else:
        k_cols = _head_rms_cols(proj(_OFF_K, SB_W), e64, gk_ref[...])
        for c, col in enumerate(k_cols):
            k_ref[:, c * cw:(c + 1) * cw] = col
        v_ref[...] = proj(_OFF_V, SB_W)
    xl_ref[...] = proj(_OFF_XL, LRU_W)
    gate_ref[...] = _gelu_tanh(proj(_OFF_GATE, LRU_W))
    qg_ref[...] = proj(_OFF_QG, GLA_QK)
    kg_ref[...] = proj(_OFF_KG, GLA_QK)
    vg_ref[...] = proj(_OFF_VG, GLA_W)
    r = proj(_OFF_RG, GLA_W)
    rg_ref[...] = r * jax.nn.sigmoid(r)
    a_lr = proj(_OFF_ALR, LANES)
    xa = _dot(a_lr.astype(BF16), wal_ref[...]) + bal_ref[...]
    la_ref[...] = _log_sigmoid(xa) * (1.0 / GLA_TAU)


def _inproj(x, g, w_in, gq, gk, wal, bal, e64, tm, kv_slabs=None, layer=None, w_kvt=None, seq=None):
    n = x.shape[0]
    row = lambda w: pl.BlockSpec((tm, w), lambda i: (i, 0))
    in_specs = [
        row(D_MODEL),
        _resident((1, D_MODEL)),
        _resident((D_MODEL, D_IN_PAD)),
        _resident((1, SB_W)),
        _resident(gk.shape),
        _resident((LANES, LANES)),
        _resident((1, LANES)),
        _resident(e64.shape),
    ]
    args = [x, g, w_in, gq, gk, wal, bal, e64]
    small = lambda w: jax.ShapeDtypeStruct((n, w), F32)
    transposed_kv = kv_slabs is not None
    if not transposed_kv:
        kv_specs = [row(SB_W), row(SB_W)]
        kv_shapes = [small(SB_W), small(SB_W)]
        aliases = {}
        kernel = functools.partial(_inproj_kernel, transposed_kv=False)
    else:
        k_all, v_all = kv_slabs
        tiles = seq // tm
        slab = pl.BlockSpec((None, None, SB_W, tm), lambda i: (layer, i // tiles, 0, i % tiles))
        kv_specs = [slab, slab]
        kv_shapes = [jax.ShapeDtypeStruct(k_all.shape, F32), jax.ShapeDtypeStruct(v_all.shape, F32)]
        in_specs += [_resident((2 * SB_W, D_MODEL)),
                     pl.BlockSpec(memory_space=pl.ANY), pl.BlockSpec(memory_space=pl.ANY)]
        args += [w_kvt, k_all, v_all]
        aliases = {len(args) - 2: 1, len(args) - 1: 2}

        def kernel(*refs):
            _inproj_kernel(*refs[:9], *refs[11:], transposed_kv=True)

    out_specs = [row(SB_W)] + kv_specs + [row(LRU_W), row(LRU_W), row(GLA_QK), row(GLA_QK),
                                          row(GLA_W), row(GLA_W), row(GLA_QK)]
    out_shape = [small(SB_W)] + kv_shapes + [small(LRU_W), small(LRU_W), small(GLA_QK), small(GLA_QK),
                                             small(GLA_W), small(GLA_W), small(GLA_QK)]
    return pl.pallas_call(
        kernel,
        grid=(n // tm,),
        in_specs=in_specs,
        out_specs=out_specs,
        out_shape=out_shape,
        input_output_aliases=aliases,
        scratch_shapes=[pltpu.VMEM((tm, D_MODEL), BF16)],
        compiler_params=_params(("parallel",)),
        name="inproj",
    )(*args)


def _merge_kernel(x_ref, osb_ref, olru_ref, ogla_ref, gate_ref, rg_ref, gmo_ref, e64_ref, wout_ref, o_ref):
    e64 = e64_ref[...]
    gmo = gmo_ref[...]
    sb = _head_rms_cols(osb_ref[...], e64, gmo[:, :SB_W])
    lru = _head_rms_cols(olru_ref[...], e64, gmo[:, SB_W:SB_W + LRU_W])
    gla = _head_rms_cols(ogla_ref[...], e64, gmo[:, SB_W + LRU_W:])
    gate = gate_ref[...]
    rg = rg_ref[...]
    cw = e64.shape[0]
    lru = [col * gate[:, c * cw:(c + 1) * cw] for c, col in enumerate(lru)]
    gla = [col * rg[:, c * cw:(c + 1) * cw] for c, col in enumerate(gla)]
    acc = x_ref[...]
    for c, col in enumerate(sb + lru + gla):
        acc = acc + _dot(col.astype(BF16), wout_ref[c * cw:(c + 1) * cw, :])
    o_ref[...] = acc


def _merge(x, osb, olru, ogla, gate, rg, gmo, e64, wout, tm):
    n = x.shape[0]
    row = lambda w: pl.BlockSpec((tm, w), lambda i: (i, 0))
    return pl.pallas_call(
        _merge_kernel,
        grid=(n // tm,),
        in_specs=[row(D_MODEL), row(SB_W), row(LRU_W), row(GLA_W), row(LRU_W), row(GLA_W),
                  _resident((1, D_MIX)), _resident(e64.shape), _resident((D_MIX, D_MODEL))],
        out_specs=row(D_MODEL),
        out_shape=jax.ShapeDtypeStruct((n, D_MODEL), F32),
        compiler_params=_params(("parallel",)),
        name="merge",
    )(x, osb, olru, ogla, gate, rg, gmo, e64, wout)


_LOG2_E = 1.4426950408889634


def _softplus_plain(z):
    return jnp.maximum(z, 0.0) + jnp.log(1.0 + jnp.exp2(jnp.abs(z) * (-_LOG2_E)))


def _sb_tile_stages(q_h, kt_b, vt_h, upper, tail, mask, tail_out=None):
    st = {}

    def logits():
        st["z"] = _dot(q_h, kt_b)

    def log_terms():
        z = st["z"]
        sp = _softplus_plain(z)
        st["lb"] = z - sp
        if mask is not None:
            sp = jnp.where(mask, sp, 0.0)
        st["edge"] = sp[:, 0:1]
        st["sp"] = sp.astype(BF16)

    def suffix_sums():
        st["within"] = _dot(st["sp"], upper)

    def weights():
        within = st["within"]
        tail_in = tail[0] if isinstance(tail, list) else tail
        a = jnp.exp(st["lb"] - within - tail_in)
        if mask is not None:
            a = jnp.where(mask, a, 0.0)
        st["a"] = a.astype(BF16)
        st["tail"] = tail_in + within[:, 0:1] + st["edge"]
        if tail_out is not None:
            tail_out.append(st["tail"])

    def values():
        return _dot_nt(st["a"], vt_h), st["tail"]

    return [logits, log_terms, suffix_sums, weights, values]


def _run_skewed(pipelines):
    depth = len(pipelines[0])
    results = [None] * len(pipelines)
    for step in range(len(pipelines) + depth - 1):
        for p, stages in enumerate(pipelines):
            s = step - p
            if 0 <= s < depth:
                results[p] = stages[s]()
    return results


def _sb_prompt_kernel(bias_ref, q_ref, kt_ref, vt_ref, o_ref, kh_ref, vh_ref, *, blk):
    pairs = SB_W // LANES
    i = pl.program_id(1)
    lane = lax.broadcasted_iota(jnp.int32, (1, LANES), 1)
    first = lane < HEAD_DIM

    @pl.when(i == 0)
    def _stage():
        row = lax.broadcasted_iota(jnp.int32, (LANES, vt_ref.shape[1]), 0)
        first_rows = row < HEAD_DIM
        ones_first = jnp.where(row < 2, 1.0, 0.0)
        ones_second = jnp.where(first_rows, 0.0, jnp.where(row < HEAD_DIM + 2, 1.0, 0.0))
        for p in range(pairs):
            k = kt_ref[p * LANES:(p + 1) * LANES, :]
            kh_ref[2 * p] = jnp.where(first_rows, k, ones_second).astype(BF16)
            kh_ref[2 * p + 1] = jnp.where(first_rows, ones_first, k).astype(BF16)
            v = vt_ref[p * LANES:(p + 1) * LANES, :]
            vh_ref[2 * p] = jnp.where(first_rows, v, 0.0).astype(BF16)
            vh_ref[2 * p + 1] = jnp.where(first_rows, 0.0, v).astype(BF16)

    def bias_lanes(h, lane0):
        b = jnp.full((1, LANES), bias_ref[h], F32)
        hi = b.astype(BF16).astype(F32)
        return jnp.where(lane == lane0, hi, jnp.where(lane == lane0 + 1, b - hi, 0.0))

    q_heads = []
    for p in range(pairs):
        q = q_ref[:, p * LANES:(p + 1) * LANES]
        q_heads += [jnp.where(first, q, bias_lanes(2 * p, HEAD_DIM)).astype(BF16),
                    jnp.where(first, bias_lanes(2 * p + 1, 0), q).astype(BF16)]
    r_i = lax.broadcasted_iota(jnp.int32, (blk, blk), 0)
    c_i = lax.broadcasted_iota(jnp.int32, (blk, blk), 1)
    upper = jnp.where(r_i > c_i, 1.0, 0.0).astype(BF16)
    causal = c_i < r_i

    def tiles(j, n_blocks, tails, accs, mask):
        pipelines = []
        for b in range(n_blocks):
            cols = pl.ds(pl.multiple_of((j - b) * blk, blk), blk)
            handed_on = [[] for _ in range(2 * pairs)]
            for h in range(2 * pairs):
                pipelines.append(_sb_tile_stages(q_heads[h], kh_ref[h, :, cols], vh_ref[h, :, cols], upper,
                                                 tails[h], mask, handed_on[h]))
            tails = handed_on
        results = _run_skewed(pipelines)
        new_accs = list(accs)
        for p, (out, _) in enumerate(results):
            h = p % (2 * pairs)
            new_accs[h // 2] = new_accs[h // 2] + out
        return tuple(t for _, t in results[-2 * pairs:]), tuple(new_accs)

    zero_c = jnp.zeros((blk, 1), F32)
    zero_a = jnp.zeros((blk, LANES), F32)
    state = tiles(i, 1, (zero_c,) * (2 * pairs), (zero_a,) * pairs, causal)

    def body(it, state):
        return tiles(i - 1 - SB_BLOCKS_PER_ITER * it, SB_BLOCKS_PER_ITER, state[0], state[1], None)

    state = lax.fori_loop(0, i // SB_BLOCKS_PER_ITER, body, state)
    for left in range(1, SB_BLOCKS_PER_ITER):
        state = lax.cond(i % SB_BLOCKS_PER_ITER == left,
                         lambda st, left=left: tiles(left - 1, left, st[0], st[1], None),
                         lambda st: st, state)
    _, accs = state
    for p in range(pairs):
        o_ref[:, p * LANES:(p + 1) * LANES] = accs[p]


def _sb_prompt(q, kt_all, vt_all, bias, layer):
    _, batch, _, seq = kt_all.shape
    blk = min(SB_BLOCK, seq)
    nq = seq // blk
    kv_spec = pl.BlockSpec((None, None, SB_W, seq), lambda b, i, *_: (layer, b, 0, 0))
    return pl.pallas_call(
        functools.partial(_sb_prompt_kernel, blk=blk),
        grid_spec=pltpu.PrefetchScalarGridSpec(
            num_scalar_prefetch=1,
            grid=(batch, nq),
            in_specs=[pl.BlockSpec((blk, SB_W), lambda b, i, *_: (b * nq + i, 0)), kv_spec, kv_spec],
            out_specs=pl.BlockSpec((blk, SB_W), lambda b, i, *_: (b * nq + i, 0)),
            scratch_shapes=[pltpu.VMEM((SB_HEADS, LANES, seq), BF16)] * 2,
        ),
        out_shape=jax.ShapeDtypeStruct((batch * seq, SB_W), F32),
        compiler_params=_params(("parallel", "arbitrary")),
        name="sb_prompt",
    )(bias, q, kt_all, vt_all)


def _sb_decode_kernel(pt_ref, q_ref, bias_ref, *refs, n_pages, group):
    k_refs = refs[:group]
    v_refs = refs[group:2 * group]
    o_ref = refs[2 * group]
    z_ref, a_ref, acc_ref = refs[2 * group + 1:]
    s = pl.program_id(1)
    steps = n_pages // group
    rows = n_pages * SB_HEADS

    @pl.when(s < steps)
    def _scores():
        q_col = q_ref[...]
        for g in range(group):
            prod = (k_refs[g][...] * q_col).reshape(SB_HEADS, HEAD_DIM, PAGE_SIZE)
            z = jnp.sum(prod, axis=1) + bias_ref[...]
            row0 = pl.multiple_of((s * group + g) * SB_HEADS, SB_HEADS)
            z_ref[pl.ds(row0, SB_HEADS), :] = z

    @pl.when(s == steps - 1)
    def _weights():
        z = z_ref[...]
        sp = _softplus(z)
        l1m = -sp
        r_i = lax.broadcasted_iota(jnp.int32, (PAGE_SIZE, PAGE_SIZE), 0)
        c_i = lax.broadcasted_iota(jnp.int32, (PAGE_SIZE, PAGE_SIZE), 1)
        upper = (r_i > c_i).astype(BF16)
        ones = jnp.ones((PAGE_SIZE, PAGE_SIZE), BF16)
        within = _split_dot(l1m, upper, 3)
        page_total = _split_dot(l1m, ones, 3)
        pr = lax.broadcasted_iota(jnp.int32, (rows, rows), 0)
        pc = lax.broadcasted_iota(jnp.int32, (rows, rows), 1)
        same_head = _imod(pc, SB_HEADS) == _imod(pr, SB_HEADS)
        later_page = jnp.where(_idiv(pc, SB_HEADS) > _idiv(pr, SB_HEADS),
                               jnp.where(same_head, 1.0, 0.0), 0.0).astype(BF16)
        later = _split_dot_left(later_page, page_total, 3)
        a_ref[...] = jnp.exp(z - sp + within + later)
        acc_ref[...] = jnp.zeros_like(acc_ref)

    @pl.when(s >= steps)
    def _values():
        acc = acc_ref[...]
        for g in range(group):
            row0 = pl.multiple_of(((s - steps) * group + g) * SB_HEADS, SB_HEADS)
            a_g = a_ref[pl.ds(row0, SB_HEADS), :]
            a_rows = jnp.concatenate(
                [jnp.broadcast_to(a_g[h:h + 1, :], (HEAD_DIM, PAGE_SIZE)) for h in range(SB_HEADS)], axis=0)
            acc = acc + v_refs[g][...] * a_rows
        acc_ref[...] = acc

    @pl.when(s == 2 * steps - 1)
    def _emit():
        o_ref[...] = jnp.sum(acc_ref[...], axis=1, keepdims=True)


def _sb_decode(q_col, bias_b, cache_kt, cache_vt, page_table, layer):
    b, n_pages = page_table.shape
    group = min(DECODE_PAGES_PER_STEP, n_pages)
    steps = n_pages // group

    def k_map(g):
        return lambda bi, s, pt: (layer, pt[bi, jnp.minimum(s, steps - 1) * group + g], 0, 0)

    def v_map(g):
        def index(bi, s, pt):
            in_values = s >= steps
            row = jnp.where(in_values, bi, jnp.maximum(bi - 1, 0))
            col = jnp.where(in_values, s - steps, steps - 1) * group + g
            return (layer, pt[row, col], 0, 0)
        return index

    page = lambda m: pl.BlockSpec((None, None, SB_W, PAGE_SIZE), m)
    in_specs = [pl.BlockSpec((None, SB_W, 1), lambda bi, s, pt: (bi, 0, 0)),
                pl.BlockSpec((SB_HEADS, PAGE_SIZE), lambda bi, s, pt: (0, 0))]
    in_specs += [page(k_map(g)) for g in range(group)]
    in_specs += [page(v_map(g)) for g in range(group)]
    rows = n_pages * SB_HEADS
    return pl.pallas_call(
        functools.partial(_sb_decode_kernel, n_pages=n_pages, group=group),
        grid_spec=pltpu.PrefetchScalarGridSpec(
            num_scalar_prefetch=1,
            grid=(b, 2 * steps),
            in_specs=in_specs,
            out_specs=pl.BlockSpec((None, SB_W, 1), lambda bi, s, pt: (bi, 0, 0)),
            scratch_shapes=[pltpu.VMEM((rows, PAGE_SIZE), F32),
                            pltpu.VMEM((rows, PAGE_SIZE), F32),
                            pltpu.VMEM((SB_W, PAGE_SIZE), F32)],
        ),
        out_shape=jax.ShapeDtypeStruct((b, SB_W, 1), F32),
        compiler_params=_params(("parallel", "arbitrary")),
        name="sb_decode",
    )(page_table, q_col, bias_b, *([cache_kt] * group), *([cache_vt] * group))


def _lru_gates(xc, wa_ref, ba_ref, wi_ref, bi_ref, lam_ref):
    xb = xc.astype(BF16)
    r = jax.nn.sigmoid(_dot(xb, wa_ref[...]) + ba_ref[...])
    i = jax.nn.sigmoid(_dot(xb, wi_ref[...]) + bi_ref[...])
    log_a = -LRU_C * r * _softplus(-lam_ref[...])
    a = jnp.exp(log_a)
    t = jnp.tanh(log_a)
    one_minus_a2 = -2.0 * t / (1.0 - t)
    return a, jnp.sqrt(one_minus_a2) * (i * xc)


def _lru_prompt_kernel(x_ref, cw_ref, cb_ref, wa_ref, ba_ref, wi_ref, bi_ref, lam_ref,
                       o_ref, hl_ref, cn_ref, *, seq):
    x = x_ref[...]
    rows = lax.broadcasted_iota(jnp.int32, (seq, LRU_W), 0)
    xc = cb_ref[...] + cw_ref[LRU_CONV - 1:LRU_CONV, :] * x
    for j in range(1, LRU_CONV):
        shifted = jnp.where(rows >= j, pltpu.roll(x, j, 0), 0.0)
        xc = xc + cw_ref[LRU_CONV - 1 - j:LRU_CONV - j, :] * shifted
    a, g = _lru_gates(xc, wa_ref, ba_ref, wi_ref, bi_ref, lam_ref)
    step = 1
    while step < seq:
        valid = rows >= step
        a_prev = pltpu.roll(a, step, 0)
        g_prev = pltpu.roll(g, step, 0)
        g = jnp.where(valid, a * g_prev + g, g)
        a = jnp.where(valid, a * a_prev, a)
        step *= 2
    o_ref[...] = g
    hl_ref[...] = g[seq - 1:seq, :]
    cn_ref[...] = x[seq - (LRU_CONV - 1):, :]


def _lru_prompt(xl, cw, cb, wa, ba, wi, bi, lam, batch, seq):
    vec = _resident((1, LRU_W))
    mat = _resident((LRU_W, LRU_W))
    return pl.pallas_call(
        functools.partial(_lru_prompt_kernel, seq=seq),
        grid=(batch,),
        in_specs=[pl.BlockSpec((seq, LRU_W), lambda b: (b, 0)), _resident((LRU_CONV, LRU_W)), vec,
                  mat, vec, mat, vec, vec],
        out_specs=[pl.BlockSpec((seq, LRU_W), lambda b: (b, 0)),
                   pl.BlockSpec((None, 1, LRU_W), lambda b: (b, 0, 0)),
                   pl.BlockSpec((None, LRU_CONV - 1, LRU_W), lambda b: (b, 0, 0))],
        out_shape=[jax.ShapeDtypeStruct((batch * seq, LRU_W), F32),
                   jax.ShapeDtypeStruct((batch, 1, LRU_W), F32),
                   jax.ShapeDtypeStruct((batch, LRU_CONV - 1, LRU_W), F32)],
        compiler_params=_params(("parallel",)),
        name="lru_prompt",
    )(xl, cw, cb, wa, ba, wi, bi, lam)


def _block_mid_rows(bc, level):
    n_rows, width = bc.shape
    half = level // 2
    if level >= 2 * SUBLANES:
        return jnp.concatenate(
            [jnp.broadcast_to(bc[b * level + half - 1:b * level + half, :], (level, width))
             for b in range(n_rows // level)], axis=0)
    groups = n_rows // SUBLANES
    bc3 = bc.reshape(groups, SUBLANES, width)
    sub = lax.broadcasted_iota(jnp.int32, (groups, SUBLANES, width), 1)
    mid = None
    for b in range(SUBLANES // level):
        piece = jnp.broadcast_to(bc3[:, b * level + half - 1:b * level + half, :], bc3.shape)
        mid = piece if mid is None else jnp.where(_idiv(sub, level) == b, piece, mid)
    return mid.reshape(n_rows, width)


def _gla_chunk(qs, k, v, la, state):
    chunk = qs.shape[0]
    r_i = lax.broadcasted_iota(jnp.int32, (chunk, chunk), 0)
    c_i = lax.broadcasted_iota(jnp.int32, (chunk, chunk), 1)
    lower = (c_i <= r_i).astype(BF16)
    bc = _split_dot_left(lower, la, 3)
    rows = lax.broadcasted_iota(jnp.int32, (chunk, GLA_QK), 0)
    qk_head = _idiv(lax.broadcasted_iota(jnp.int32, (1, GLA_QK), 1), GLA_DK)
    v_head = _idiv(lax.broadcasted_iota(jnp.int32, (1, GLA_W), 1), GLA_DV)
    own = (_idiv(lax.broadcasted_iota(jnp.int32, (GLA_QK, GLA_W), 0), GLA_DK)
           == _idiv(lax.broadcasted_iota(jnp.int32, (GLA_QK, GLA_W), 1), GLA_DV))

    out = _dot((qs * jnp.exp(bc)).astype(BF16), state.astype(BF16))

    scores = [jnp.zeros((chunk, chunk), F32) for _ in range(GLA_HEADS)]
    level = 2
    while level <= chunk:
        half = level // 2
        mid = _block_mid_rows(bc, level)
        second = (rows & (level - 1)) >= half
        q_dec = jnp.where(second, qs * jnp.exp(bc - mid), 0.0)
        k_dec = jnp.where(second, 0.0, k * jnp.exp(mid - bc)).astype(BF16)
        same_block = _idiv(r_i, level) == _idiv(c_i, level)
        for h in range(GLA_HEADS):
            sc = _dot_nt(jnp.where(qk_head == h, q_dec, 0.0).astype(BF16), k_dec)
            if level < chunk:
                sc = jnp.where(same_block, sc, 0.0)
            scores[h] = scores[h] + sc
        level *= 2
    for h in range(GLA_HEADS):
        out = out + _dot(scores[h].astype(BF16), jnp.where(v_head == h, v, 0.0).astype(BF16))

    expand = jnp.where(own, 1.0, 0.0).astype(BF16)
    out = out + _split_dot(qs * k, expand, 2) * v

    last = bc[chunk - 1:chunk, :]
    kv = _dot_tn((k * jnp.exp(last - bc)).astype(BF16), v.astype(BF16))
    dec_col = jnp.transpose(jnp.broadcast_to(jnp.exp(last), (GLA_QK, GLA_QK)))
    dec = jnp.concatenate([dec_col] * (GLA_W // GLA_QK), axis=1)
    return out, dec * state + jnp.where(own, kv, 0.0)


def _gla_prompt_kernel(q_ref, k_ref, v_ref, la_ref, o_ref, sfin_ref, s_ref, *, chunk, n_steps):
    ci = pl.program_id(1)

    @pl.when(ci == 0)
    def _init():
        s_ref[...] = jnp.zeros_like(s_ref)

    state = s_ref[...]
    for u in range(q_ref.shape[0] // chunk):
        rows = slice(u * chunk, (u + 1) * chunk)
        out, state = _gla_chunk(q_ref[rows, :] * (GLA_DK ** -0.5), k_ref[rows, :], v_ref[rows, :],
                                la_ref[rows, :], state)
        o_ref[rows, :] = out
    s_ref[...] = state

    @pl.when(ci == n_steps - 1)
    def _emit():
        for h in range(GLA_HEADS):
            sfin_ref[h * GLA_DK:(h + 1) * GLA_DK, :] = (
                state[h * GLA_DK:(h + 1) * GLA_DK, h * GLA_DV:(h + 1) * GLA_DV])


def _gla_prompt(qg, kg, vg, la, batch, seq):
    chunk = min(GLA_CHUNK, seq)
    block = chunk * GLA_CHUNKS_PER_STEP if seq % (chunk * GLA_CHUNKS_PER_STEP) == 0 else chunk
    n_steps = seq // block
    row = lambda w: pl.BlockSpec((block, w), lambda b, c: (b * n_steps + c, 0))
    return pl.pallas_call(
        functools.partial(_gla_prompt_kernel, chunk=chunk, n_steps=n_steps),
        grid=(batch, n_steps),
        in_specs=[row(GLA_QK), row(GLA_QK), row(GLA_W), row(GLA_QK)],
        out_specs=[row(GLA_W), pl.BlockSpec((None, GLA_QK, GLA_DV), lambda b, c: (b, 0, 0))],
        out_shape=[jax.ShapeDtypeStruct((batch * seq, GLA_W), F32),
                   jax.ShapeDtypeStruct((batch, GLA_QK, GLA_DV), F32)],
        scratch_shapes=[pltpu.VMEM((GLA_QK, GLA_W), F32)],
        compiler_params=_params(("parallel", "arbitrary")),
        name="gla_prompt",
    )(qg, kg, vg, la)


def _step_kernel(x_ref, conv_ref, h0_ref, cw_ref, cb_ref, wa_ref, ba_ref, wi_ref, bi_ref, lam_ref,
                 qc_ref, kc_ref, lac_ref, v_ref, s0_ref,
                 h_ref, cn_ref, og_ref, s_ref):
    x = x_ref[...]
    xc = cb_ref[...] + cw_ref[LRU_CONV - 1:LRU_CONV, :] * x
    for j in range(LRU_CONV - 1):
        xc = xc + cw_ref[j:j + 1, :] * conv_ref[j]
    a, g = _lru_gates(xc, wa_ref, ba_ref, wi_ref, bi_ref, lam_ref)
    h_ref[...] = a * h0_ref[...] + g
    for j in range(LRU_CONV - 2):
        cn_ref[j] = conv_ref[j + 1]
    cn_ref[LRU_CONV - 2] = x

    alpha = jnp.exp(lac_ref[...])
    kc = kc_ref[...]
    qc = qc_ref[...] * (GLA_DK ** -0.5)
    for h in range(GLA_HEADS):
        rs = slice(h * GLA_DK, (h + 1) * GLA_DK)
        new = alpha[:, rs, :] * s0_ref[:, rs, :] + kc[:, rs, :] * v_ref[:, h:h + 1, :]
        s_ref[:, rs, :] = new
        og_ref[:, h:h + 1, :] = jnp.sum(qc[:, rs, :] * new, axis=1, keepdims=True)


def _step(xl, conv0, h0, cw, cb, wa, ba, wi, bi, lam, q_col, k_col, la_col, v3, s0):
    b = xl.shape[0]
    return pl.pallas_call(
        _step_kernel,
        out_shape=[jax.ShapeDtypeStruct((b, LRU_W), F32),
                   jax.ShapeDtypeStruct((LRU_CONV - 1, b, LRU_W), F32),
                   jax.ShapeDtypeStruct((b, GLA_HEADS, GLA_DV), F32),
                   jax.ShapeDtypeStruct((b, GLA_QK, GLA_DV), F32)],
        compiler_params=pltpu.CompilerParams(vmem_limit_bytes=VMEM_LIMIT_BYTES),
        name="sample_step",
    )(xl, conv0, h0, cw, cb, wa, ba, wi, bi, lam, q_col, k_col, la_col, v3, s0)


def _block_diag(w):
    nb, bi, bj = w.shape
    eye = jnp.eye(nb, dtype=w.dtype)
    return (eye[:, None, :, None] * w[:, :, None, :]).reshape(nb * bi, nb * bj)


def _row_tile(n):
    return min(ROW_TILE, n)


def kernel(x_prompt, x_sample, cache_k, cache_v, page_table, state_lru_h, state_lru_conv, state_gla,
           g_ffn1, w_ffn1_gate, w_ffn1_up, w_ffn1_down, g_mix, w_in, g_qnorm, g_knorm, sb_bias,
           conv_w, conv_b, lru_wa, lru_ba, lru_wi, lru_bi, lru_lambda, gla_w_alpha, gla_b_alpha,
           g_mix_out, w_out, g_ffn2, w_ffn2_gate, w_ffn2_up, w_ffn2_down):
    depth = w_in.shape[0]
    bp, seq, _ = x_prompt.shape
    bs, dec_seq, _ = x_sample.shape
    assert dec_seq == 1
    n_p = bp * seq
    n_phys = cache_k.shape[1]

    w_in_p = jnp.pad(w_in, ((0, 0), (0, 0), (0, D_IN_PAD - w_in.shape[2]))).astype(BF16)
    w_kvt = w_in[:, :, _OFF_K:_OFF_XL].transpose(0, 2, 1).astype(BF16)
    w_al = jnp.pad(gla_w_alpha, ((0, 0), (0, LANES - GLA_RANK), (0, 0))).astype(BF16)
    w_out_b = w_out.astype(BF16)
    wa_bd = jax.vmap(_block_diag)(lru_wa).astype(BF16)
    wi_bd = jax.vmap(_block_diag)(lru_wi).astype(BF16)
    gq = jnp.tile(g_qnorm, (1, SB_HEADS))[:, None, :]
    gk = jnp.tile(g_knorm, (1, SB_HEADS))[:, None, :]
    gk_col = jnp.tile(g_knorm, (1, SB_HEADS))[:, :, None]
    lane = jnp.arange(HEAD_MEAN_WIDTH)
    e64 = ((lane[:, None] // HEAD_DIM == lane[None, :] // HEAD_DIM).astype(F32) / HEAD_DIM).astype(BF16)
    vec = lambda t, l: t[l][None, :]

    cache_kt = cache_k.transpose(0, 1, 3, 4, 2).reshape(depth, n_phys, SB_W, PAGE_SIZE)
    cache_vt = cache_v.transpose(0, 1, 3, 4, 2).reshape(depth, n_phys, SB_W, PAGE_SIZE)

    xp = x_prompt.reshape(n_p, D_MODEL)
    xs = x_sample.reshape(bs, D_MODEL)
    tm_p = _row_tile(n_p)
    tm_s = _row_tile(bs)
    kt_all = jnp.zeros((depth, bp, SB_W, seq), F32)
    vt_all = jnp.zeros((depth, bp, SB_W, seq), F32)
    outs = {name: [] for name in ("ks", "vs", "hp", "hs", "cp", "cs", "sp", "ss")}

    for l in range(depth):
        lru_w = (conv_w[l], vec(conv_b, l), wa_bd[l], vec(lru_ba, l), wi_bd[l], vec(lru_bi, l),
                 vec(lru_lambda, l))
        inproj_w = (vec(g_mix, l), w_in_p[l], gq[l])
        inproj_w2 = (w_al[l], vec(gla_b_alpha, l), e64)

        ffn1_w = (vec(g_ffn1, l), w_ffn1_gate, w_ffn1_up, w_ffn1_down, l)
        ffn2_w = (vec(g_ffn2, l), w_ffn2_gate, w_ffn2_up, w_ffn2_down, l)
        xp, xs = _ffn(xp, xs, *ffn1_w, tm_p)

        q, kt_all, vt_all, xl, gate, qg, kg, vg, rg, la = _inproj(
            xp, *inproj_w, gk_col[l], *inproj_w2, tm_p, (kt_all, vt_all), l, w_kvt[l], seq)
        osb = _sb_prompt(q, kt_all, vt_all, sb_bias[l], l)
        olru, h_last, conv_new = _lru_prompt(xl, *lru_w, bp, seq)
        ogla, s_fin = _gla_prompt(qg, kg, vg, la, bp, seq)
        xp = _merge(xp, osb, olru, ogla, gate, rg, vec(g_mix_out, l), e64, w_out_b[l], tm_p)
        outs["hp"].append(h_last.reshape(bp, LRU_W))
        outs["cp"].append(conv_new)
        outs["sp"].append(s_fin.reshape(bp, GLA_HEADS, GLA_DK, GLA_DV))

        q, k_new, v_new, xl, gate, qg, kg, vg, rg, la = _inproj(xs, *inproj_w, gk[l], *inproj_w2, tm_s)
        bias_b = jnp.broadcast_to(sb_bias[l][:, None], (SB_HEADS, PAGE_SIZE))
        osb = _sb_decode(q.reshape(bs, SB_W, 1), bias_b, cache_kt, cache_vt, page_table, l)
        h_new, conv_new, ogla, s_new = _step(
            xl, state_lru_conv[l].transpose(1, 0, 2), state_lru_h[l], *lru_w,
            qg.reshape(bs, GLA_QK, 1), kg.reshape(bs, GLA_QK, 1), la.reshape(bs, GLA_QK, 1),
            vg.reshape(bs, GLA_HEADS, GLA_DV), state_gla[l].reshape(bs, GLA_QK, GLA_DV))
        xs = _merge(xs, osb.reshape(bs, SB_W), h_new, ogla.reshape(bs, GLA_W), gate, rg,
                    vec(g_mix_out, l), e64, w_out_b[l], tm_s)
        xp, xs = _ffn(xp, xs, *ffn2_w, tm_p)
        outs["ks"].append(k_new.reshape(bs, 1, SB_HEADS, HEAD_DIM))
        outs["vs"].append(v_new.reshape(bs, 1, SB_HEADS, HEAD_DIM))
        outs["hs"].append(h_new)
        outs["cs"].append(conv_new.transpose(1, 0, 2))
        outs["ss"].append(s_new.reshape(bs, GLA_HEADS, GLA_DK, GLA_DV))

    st = lambda name: jnp.stack(outs[name])
    return (xp.reshape(bp, seq, D_MODEL), xs.reshape(bs, 1, D_MODEL),
            kt_all.reshape(depth, bp, SB_HEADS, HEAD_DIM, seq).transpose(0, 1, 4, 2, 3),
            vt_all.reshape(depth, bp, SB_HEADS, HEAD_DIM, seq).transpose(0, 1, 4, 2, 3),
            st("ks"), st("vs"), st("hp"), st("hs"), st("cp"), st("cs"), st("sp"), st("ss"))
```

```python
import functools

import jax
import jax.numpy as jnp
from jax import lax
from jax.experimental import pallas as pl
from jax.experimental.pallas import tpu as pltpu

F32 = jnp.float32
BF16 = jnp.bfloat16

D_MODEL = 1024
HEAD_DIM = 64
SB_W = D_MODEL // 2
SB_HEADS = SB_W // HEAD_DIM
LRU_W = D_MODEL // 4
LRU_BLOCKS = LRU_W // HEAD_DIM
LRU_CONV = 4
LRU_C = 8.0
GLA_W = D_MODEL // 4
GLA_HEADS = GLA_W // HEAD_DIM
GLA_DV = HEAD_DIM
GLA_DK = HEAD_DIM // 2
GLA_QK = GLA_HEADS * GLA_DK
GLA_RANK = 16
GLA_TAU = 16.0
D_MIX = SB_W + LRU_W + GLA_W
D_FF = 2816
EPS = 1e-6
PAGE_SIZE = 128

LANES = 128
SUBLANES = 8
VMEM_LIMIT_BYTES = 56 * 1024 * 1024

_OFF_Q = 0
_OFF_K = _OFF_Q + SB_W
_OFF_V = _OFF_K + SB_W
_OFF_XL = _OFF_V + SB_W
_OFF_GATE = _OFF_XL + LRU_W
_OFF_QG = _OFF_GATE + LRU_W
_OFF_KG = _OFF_QG + GLA_QK
_OFF_VG = _OFF_KG + GLA_QK
_OFF_RG = _OFF_VG + GLA_W
_OFF_ALR = _OFF_RG + GLA_W
D_IN_PAD = _OFF_ALR + LANES

FF_CHUNK = 256
HEAD_MEAN_WIDTH = 256
ROW_TILE = 512
SB_BLOCK = 256
SB_BLOCKS_PER_ITER = 2
GLA_CHUNK = 128
GLA_CHUNKS_PER_STEP = 4
DECODE_PAGES_PER_STEP = 32


def _idiv(x, d):
    assert d & (d - 1) == 0
    return x >> (d.bit_length() - 1)


def _imod(x, d):
    assert d & (d - 1) == 0
    return x & (d - 1)


def _dot(a, b):
    return jnp.dot(a, b, preferred_element_type=F32)


def _dot_nt(a, b):
    return lax.dot_general(a, b, (((1,), (1,)), ((), ())), preferred_element_type=F32)


def _dot_tn(a, b):
    return lax.dot_general(a, b, (((0,), (0,)), ((), ())), preferred_element_type=F32)


def _split_dot(x, w, passes):
    hi = x.astype(BF16)
    acc = _dot(hi, w)
    rem = x - hi.astype(F32)
    for _ in range(passes - 1):
        lo = rem.astype(BF16)
        acc = acc + _dot(lo, w)
        rem = rem - lo.astype(F32)
    return acc


def _split_dot_left(w, x, passes):
    hi = x.astype(BF16)
    acc = _dot(w, hi)
    rem = x - hi.astype(F32)
    for _ in range(passes - 1):
        lo = rem.astype(BF16)
        acc = acc + _dot(w, lo)
        rem = rem - lo.astype(F32)
    return acc


def _rms(x, g):
    ms = jnp.mean(x * x, axis=-1, keepdims=True)
    return x * lax.rsqrt(ms + EPS) * g


def _softplus(z):
    return jnp.maximum(z, 0.0) + jnp.log1p(jnp.exp(-jnp.abs(z)))


def _log_sigmoid(z):
    return jnp.minimum(z, 0.0) - jnp.log1p(jnp.exp(-jnp.abs(z)))


def _gelu_tanh(x):
    return 0.5 * x * (1.0 + jnp.tanh(0.7978845608028654 * (x + 0.044715 * (x * x * x))))


def _head_rms_cols(x, e64, g):
    cw = e64.shape[0]
    cols = []
    for c in range(x.shape[1] // cw):
        xc = x[:, c * cw:(c + 1) * cw]
        ms = _split_dot(xc * xc, e64, 2)
        cols.append(xc * lax.rsqrt(ms + EPS) * g[:, c * cw:(c + 1) * cw])
    return cols


def _resident(shape):
    nd = len(shape)
    return pl.BlockSpec(shape, lambda *_: (0,) * nd, pipeline_mode=pl.Buffered(1))


def _layer_resident(shape, layer):
    nd = len(shape)
    return pl.BlockSpec((None,) + tuple(shape), lambda *_: (layer,) + (0,) * nd, pipeline_mode=pl.Buffered(1))


def _params(semantics):
    return pltpu.CompilerParams(dimension_semantics=semantics, vmem_limit_bytes=VMEM_LIMIT_BYTES)


def _ffn_rows(x, g_ref, wg_ref, wu_ref, wd_ref, h_ref, acc_ref):
    h_ref[...] = _rms(x, g_ref[...]).astype(BF16)
    acc_ref[...] = jnp.zeros_like(acc_ref)

    def body(c, carry):
        cols = pl.ds(pl.multiple_of(c * FF_CHUNK, FF_CHUNK), FF_CHUNK)
        h = h_ref[...]
        gate = _dot(h, wg_ref[:, cols].astype(BF16))
        up = _dot(h, wu_ref[:, cols].astype(BF16))
        act = (gate * jax.nn.sigmoid(gate) * up).astype(BF16)
        acc_ref[...] += _dot(act, wd_ref[cols, :].astype(BF16))
        return carry

    lax.fori_loop(0, D_FF // FF_CHUNK, body, 0, unroll=True)
    return x + 0.5 * acc_ref[...]


def _ffn_kernel(xp_ref, xs_ref, g_ref, wg_ref, wu_ref, wd_ref, op_ref, os_ref, h_ref, acc_ref):
    weights = (g_ref, wg_ref, wu_ref, wd_ref)
    op_ref[...] = _ffn_rows(xp_ref[...], *weights, h_ref, acc_ref)

    @pl.when(pl.program_id(0) == pl.num_programs(0) - 1)
    def _sample_rows():
        rows = xs_ref.shape[0]
        os_ref[...] = _ffn_rows(xs_ref[...], *weights, h_ref.at[:rows], acc_ref.at[:rows])


def _ffn(xp, xs, g, wg, wu, wd, layer, tm):
    n = xp.shape[0]
    ns = xs.shape[0]
    assert ns <= tm
    return pl.pallas_call(
        _ffn_kernel,
        grid=(n // tm,),
        in_specs=[
            pl.BlockSpec((tm, D_MODEL), lambda i: (i, 0)),
            _resident((ns, D_MODEL)),
            _resident((1, D_MODEL)),
            _layer_resident((D_MODEL, D_FF), layer),
            _layer_resident((D_MODEL, D_FF), layer),
            _layer_resident((D_FF, D_MODEL), layer),
        ],
        out_specs=[pl.BlockSpec((tm, D_MODEL), lambda i: (i, 0)),
                   pl.BlockSpec((ns, D_MODEL), lambda i: (0, 0))],
        out_shape=[jax.ShapeDtypeStruct((n, D_MODEL), F32), jax.ShapeDtypeStruct((ns, D_MODEL), F32)],
        scratch_shapes=[pltpu.VMEM((tm, D_MODEL), BF16), pltpu.VMEM((tm, D_MODEL), F32)],
        compiler_params=_params(("arbitrary",)),
        name="ffn",
    )(xp, xs, g, wg, wu, wd)


def _inproj_kernel(*refs, transposed_kv):
    if transposed_kv:
        (x_ref, g_ref, w_ref, gq_ref, gk_ref, wal_ref, bal_ref, e64_ref, wkvt_ref,
         q_ref, k_ref, v_ref, xl_ref, gate_ref, qg_ref, kg_ref, vg_ref, rg_ref, la_ref, h_ref) = refs
    else:
        (x_ref, g_ref, w_ref, gq_ref, gk_ref, wal_ref, bal_ref, e64_ref,
         q_ref, k_ref, v_ref, xl_ref, gate_ref, qg_ref, kg_ref, vg_ref, rg_ref, la_ref, h_ref) = refs
    h_ref[...] = _rms(x_ref[...], g_ref[...]).astype(BF16)

    def proj(lo, width):
        return _dot(h_ref[...], w_ref[:, lo:lo + width])

    e64 = e64_ref[...]
    q_cols = _head_rms_cols(proj(_OFF_Q, SB_W), e64, gq_ref[...])
    cw = e64.shape[0]
    for c, col in enumerate(q_cols):
        q_ref[:, c * cw:(c + 1) * cw] = col * (HEAD_DIM ** -0.5)
    if transposed_kv:
        rows = h_ref.shape[0]
        kv_t = _dot_nt(wkvt_ref[...], h_ref[...])
        k_t = kv_t[:SB_W].reshape(SB_HEADS, HEAD_DIM, rows)
        ms = jnp.mean(k_t * k_t, axis=1, keepdims=True)
        k_ref[...] = (k_t * lax.rsqrt(ms + EPS)).reshape(SB_W, rows) * gk_ref[...]
        v_ref[...] = kv_t[SB_W:]
    else:
        k_cols = _head_rms_cols(proj(_OFF_K, SB_W), e64, gk_ref[...])
        for c, col in enumerate(k_cols):
            k_ref[:, c * cw:(c + 1) * cw] = col
        v_ref[...] = proj(_OFF_V, SB_W)
    xl_ref[...] = proj(_OFF_XL, LRU_W)
    gate_ref[...] = _gelu_tanh(proj(_OFF_GATE, LRU_W))
    qg_ref[...] = proj(_OFF_QG, GLA_QK)
    kg_ref[...] = proj(_OFF_KG, GLA_QK)
    vg_ref[...] = proj(_OFF_VG, GLA_W)
    r = proj(_OFF_RG, GLA_W)
    rg_ref[...] = r * jax.nn.sigmoid(r)
    a_lr = proj(_OFF_ALR, LANES)
    xa = _dot(a_lr.astype(BF16), wal_ref[...]) + bal_ref[...]
    la_ref[...] = _log_sigmoid(xa) * (1.0 / GLA_TAU)


def _inproj(x, g, w_in, gq, gk, wal, bal, e64, tm, kv_slabs=None, layer=None, w_kvt=None, seq=None):
    n = x.shape[0]
    row = lambda w: pl.BlockSpec((tm, w), lambda i: (i, 0))
    in_specs = [
        row(D_MODEL),
        _resident((1, D_MODEL)),
        _resident((D_MODEL, D_IN_PAD)),
        _resident((1, SB_W)),
        _resident(gk.shape),
        _resident((LANES, LANES)),
        _resident((1, LANES)),
        _resident(e64.shape),
    ]
    args = [x, g, w_in, gq, gk, wal, bal, e64]
    small = lambda w: jax.ShapeDtypeStruct((n, w), F32)
    transposed_kv = kv_slabs is not None
    if not transposed_kv:
        kv_specs = [row(SB_W), row(SB_W)]
        kv_shapes = [small(SB_W), small(SB_W)]
        aliases = {}
        kernel = functools.partial(_inproj_kernel, transposed_kv=False)
    else:
        k_all, v_all = kv_slabs
        tiles = seq // tm
        slab = pl.BlockSpec((None, None, SB_W, tm), lambda i: (layer, i // tiles, 0, i % tiles))
        kv_specs = [slab, slab]
        kv_shapes = [jax.ShapeDtypeStruct(k_all.shape, F32), jax.ShapeDtypeStruct(v_all.shape, F32)]
        in_specs += [_resident((2 * SB_W, D_MODEL)),
                     pl.BlockSpec(memory_space=pl.ANY), pl.BlockSpec(memory_space=pl.ANY)]
        args += [w_kvt, k_all, v_all]
        aliases = {len(args) - 2: 1, len(args) - 1: 2}

        def kernel(*refs):
            _inproj_kernel(*refs[:9], *refs[11:], transposed_kv=True)

    out_specs = [row(SB_W)] + kv_specs + [row(LRU_W), row(LRU_W), row(GLA_QK), row(GLA_QK),
                                          row(GLA_W), row(GLA_W), row(GLA_QK)]
    out_shape = [small(SB_W)] + kv_shapes + [small(LRU_W), small(LRU_W), small(GLA_QK), small(GLA_QK),
                                             small(GLA_W), small(GLA_W), small(GLA_QK)]
    return pl.pallas_call(
        kernel,
        grid=(n // tm,),
        in_specs=in_specs,
        out_specs=out_specs,
        out_shape=out_shape,
        input_output_aliases=aliases,
        scratch_shapes=[pltpu.VMEM((tm, D_MODEL), BF16)],
        compiler_params=_params(("parallel",)),
        name="inproj",
    )(*args)


def _merge_kernel(x_ref, osb_ref, olru_ref, ogla_ref, gate_ref, rg_ref, gmo_ref, e64_ref, wout_ref, o_ref):
    e64 = e64_ref[...]
    gmo = gmo_ref[...]
    sb = _head_rms_cols(osb_ref[...], e64, gmo[:, :SB_W])
    lru = _head_rms_cols(olru_ref[...], e64, gmo[:, SB_W:SB_W + LRU_W])
    gla = _head_rms_cols(ogla_ref[...], e64, gmo[:, SB_W + LRU_W:])
    gate = gate_ref[...]
    rg = rg_ref[...]
    cw = e64.shape[0]
    lru = [col * gate[:, c * cw:(c + 1) * cw] for c, col in enumerate(lru)]
    gla = [col * rg[:, c * cw:(c + 1) * cw] for c, col in enumerate(gla)]
    acc = x_ref[...]
    for c, col in enumerate(sb + lru + gla):
        acc = acc + _dot(col.astype(BF16), wout_ref[c * cw:(c + 1) * cw, :])
    o_ref[...] = acc


def _merge(x, osb, olru, ogla, gate, rg, gmo, e64, wout, tm):
    n = x.shape[0]
    row = lambda w: pl.BlockSpec((tm, w), lambda i: (i, 0))
    return pl.pallas_call(
        _merge_kernel,
        grid=(n // tm,),
        in_specs=[row(D_MODEL), row(SB_W), row(LRU_W), row(GLA_W), row(LRU_W), row(GLA_W),
                  _resident((1, D_MIX)), _resident(e64.shape), _resident((D_MIX, D_MODEL))],
        out_specs=row(D_MODEL),
        out_shape=jax.ShapeDtypeStruct((n, D_MODEL), F32),
        compiler_params=_params(("parallel",)),
        name="merge",
    )(x, osb, olru, ogla, gate, rg, gmo, e64, wout)


_LOG2_E = 1.4426950408889634


def _softplus_plain(z):
    return jnp.maximum(z, 0.0) + jnp.log(1.0 + jnp.exp2(jnp.abs(z) * (-_LOG2_E)))


def _sb_tile_stages(q_h, kt_b, vt_h, upper, tail, mask, tail_out=None):
    st = {}

    def logits():
        st["z"] = _dot(q_h, kt_b)

    def log_terms():
        z = st["z"]
        sp = _softplus_plain(z)
        st["lb"] = z - sp
        if mask is not None:
            sp = jnp.where(mask, sp, 0.0)
        st["edge"] = sp[:, 0:1]
        st["sp"] = sp.astype(BF16)

    def suffix_sums():
        st["within"] = _dot(st["sp"], upper)

    def weights():
        within = st["within"]
        tail_in = tail[0] if isinstance(tail, list) else tail
        a = jnp.exp(st["lb"] - within - tail_in)
        if mask is not None:
            a = jnp.where(mask, a, 0.0)
        st["a"] = a.astype(BF16)
        st["tail"] = tail_in + within[:, 0:1] + st["edge"]
        if tail_out is not None:
            tail_out.append(st["tail"])

    def values():
        return _dot_nt(st["a"], vt_h), st["tail"]

    return [logits, log_terms, suffix_sums, weights, values]


def _run_skewed(pipelines):
    depth = len(pipelines[0])
    results = [None] * len(pipelines)
    for step in range(len(pipelines) + depth - 1):
        for p, stages in enumerate(pipelines):
            s = step - p
            if 0 <= s < depth:
                results[p] = stages[s]()
    return results


def _sb_prompt_kernel(bias_ref, q_ref, kt_ref, vt_ref, o_ref, kh_ref, vh_ref, *, blk):
    pairs = SB_W // LANES
    i = pl.program_id(1)
    lane = lax.broadcasted_iota(jnp.int32, (1, LANES), 1)
    first = lane < HEAD_DIM

    @pl.when(i == 0)
    def _stage():
        row = lax.broadcasted_iota(jnp.int32, (LANES, vt_ref.shape[1]), 0)
        first_rows = row < HEAD_DIM
        ones_first = jnp.where(row < 2, 1.0, 0.0)
        ones_second = jnp.where(first_rows, 0.0, jnp.where(row < HEAD_DIM + 2, 1.0, 0.0))
        for p in range(pairs):
            k = kt_ref[p * LANES:(p + 1) * LANES, :]
            kh_ref[2 * p] = jnp.where(first_rows, k, ones_second).astype(BF16)
            kh_ref[2 * p + 1] = jnp.where(first_rows, ones_first, k).astype(BF16)
            v = vt_ref[p * LANES:(p + 1) * LANES, :]
            vh_ref[2 * p] = jnp.where(first_rows, v, 0.0).astype(BF16)
            vh_ref[2 * p + 1] = jnp.where(first_rows, 0.0, v).astype(BF16)

    def bias_lanes(h, lane0):
        b = jnp.full((1, LANES), bias_ref[h], F32)
        hi = b.astype(BF16).astype(F32)
        return jnp.where(lane == lane0, hi, jnp.where(lane == lane0 + 1, b - hi, 0.0))

    q_heads = []
    for p in range(pairs):
        q = q_ref[:, p * LANES:(p + 1) * LANES]
        q_heads += [jnp.where(first, q, bias_lanes(2 * p, HEAD_DIM)).astype(BF16),
                    jnp.where(first, bias_lanes(2 * p + 1, 0), q).astype(BF16)]
    r_i = lax.broadcasted_iota(jnp.int32, (blk, blk), 0)
    c_i = lax.broadcasted_iota(jnp.int32, (blk, blk), 1)
    upper = jnp.where(r_i > c_i, 1.0, 0.0).astype(BF16)
    causal = c_i < r_i

    def tiles(j, n_blocks, tails, accs, mask):
        pipelines = []
        for b in range(n_blocks):
            cols = pl.ds(pl.multiple_of((j - b) * blk, blk), blk)
            handed_on = [[] for _ in range(2 * pairs)]
            for h in range(2 * pairs):
                pipelines.append(_sb_tile_stages(q_heads[h], kh_ref[h, :, cols], vh_ref[h, :, cols], upper,
                                                 tails[h], mask, handed_on[h]))
            tails = handed_on
        results = _run_skewed(pipelines)
        new_accs = list(accs)
        for p, (out, _) in enumerate(results):
            h = p % (2 * pairs)
            new_accs[h // 2] = new_accs[h // 2] + out
        return tuple(t for _, t in results[-2 * pairs:]), tuple(new_accs)

    zero_c = jnp.zeros((blk, 1), F32)
    zero_a = jnp.zeros((blk, LANES), F32)
    state = tiles(i, 1, (zero_c,) * (2 * pairs), (zero_a,) * pairs, causal)

    def body(it, state):
        return tiles(i - 1 - SB_BLOCKS_PER_ITER * it, SB_BLOCKS_PER_ITER, state[0], state[1], None)

    state = lax.fori_loop(0, i // SB_BLOCKS_PER_ITER, body, state)
    for left in range(1, SB_BLOCKS_PER_ITER):
        state = lax.cond(i % SB_BLOCKS_PER_ITER == left,
                         lambda st, left=left: tiles(left - 1, left, st[0], st[1], None),
                         lambda st: st, state)
    _, accs = state
    for p in range(pairs):
        o_ref[:, p * LANES:(p + 1) * LANES] = accs[p]


def _sb_prompt(q, kt_all, vt_all, bias, layer):
    _, batch, _, seq = kt_all.shape
    blk = min(SB_BLOCK, seq)
    nq = seq // blk
    kv_spec = pl.BlockSpec((None, None, SB_W, seq), lambda b, i, *_: (layer, b, 0, 0))
    return pl.pallas_call(
        functools.partial(_sb_prompt_kernel, blk=blk),
        grid_spec=pltpu.PrefetchScalarGridSpec(
            num_scalar_prefetch=1,
            grid=(batch, nq),
            in_specs=[pl.BlockSpec((blk, SB_W), lambda b, i, *_: (b * nq + i, 0)), kv_spec, kv_spec],
            out_specs=pl.BlockSpec((blk, SB_W), lambda b, i, *_: (b * nq + i, 0)),
            scratch_shapes=[pltpu.VMEM((SB_HEADS, LANES, seq), BF16)] * 2,
        ),
        out_shape=jax.ShapeDtypeStruct((batch * seq, SB_W), F32),
        compiler_params=_params(("parallel", "arbitrary")),
        name="sb_prompt",
    )(bias, q, kt_all, vt_all)


def _sb_decode_kernel(pt_ref, q_ref, bias_ref, *refs, n_pages, group):
    k_refs = refs[:group]
    v_refs = refs[group:2 * group]
    o_ref = refs[2 * group]
    z_ref, a_ref, acc_ref = refs[2 * group + 1:]
    s = pl.program_id(1)
    steps = n_pages // group
    rows = n_pages * SB_HEADS

    @pl.when(s < steps)
    def _scores():
        q_col = q_ref[...]
        for g in range(group):
            prod = (k_refs[g][...] * q_col).reshape(SB_HEADS, HEAD_DIM, PAGE_SIZE)
            z = jnp.sum(prod, axis=1) + bias_ref[...]
            row0 = pl.multiple_of((s * group + g) * SB_HEADS, SB_HEADS)
            z_ref[pl.ds(row0, SB_HEADS), :] = z

    @pl.when(s == steps - 1)
    def _weights():
        z = z_ref[...]
        sp = _softplus(z)
        l1m = -sp
        r_i = lax.broadcasted_iota(jnp.int32, (PAGE_SIZE, PAGE_SIZE), 0)
        c_i = lax.broadcasted_iota(jnp.int32, (PAGE_SIZE, PAGE_SIZE), 1)
        upper = (r_i > c_i).astype(BF16)
        ones = jnp.ones((PAGE_SIZE, PAGE_SIZE), BF16)
        within = _split_dot(l1m, upper, 3)
        page_total = _split_dot(l1m, ones, 3)
        pr = lax.broadcasted_iota(jnp.int32, (rows, rows), 0)
        pc = lax.broadcasted_iota(jnp.int32, (rows, rows), 1)
        same_head = _imod(pc, SB_HEADS) == _imod(pr, SB_HEADS)
        later_page = jnp.where(_idiv(pc, SB_HEADS) > _idiv(pr, SB_HEADS),
                               jnp.where(same_head, 1.0, 0.0), 0.0).astype(BF16)
        later = _split_dot_left(later_page, page_total, 3)
        a_ref[...] = jnp.exp(z - sp + within + later)
        acc_ref[...] = jnp.zeros_like(acc_ref)

    @pl.when(s >= steps)
    def _values():
        acc = acc_ref[...]
        for g in range(group):
            row0 = pl.multiple_of(((s - steps) * group + g) * SB_HEADS, SB_HEADS)
            a_g = a_ref[pl.ds(row0, SB_HEADS), :]
            a_rows = jnp.concatenate(
                [jnp.broadcast_to(a_g[h:h + 1, :], (HEAD_DIM, PAGE_SIZE)) for h in range(SB_HEADS)], axis=0)
            acc = acc + v_refs[g][...] * a_rows
        acc_ref[...] = acc

    @pl.when(s == 2 * steps - 1)
    def _emit():
        o_ref[...] = jnp.sum(acc_ref[...], axis=1, keepdims=True)


def _sb_decode(q_col, bias_b, cache_kt, cache_vt, page_table, layer):
    b, n_pages = page_table.shape
    group = min(DECODE_PAGES_PER_STEP, n_pages)
    steps = n_pages // group

    def k_map(g):
        return lambda bi, s, pt: (layer, pt[bi, jnp.minimum(s, steps - 1) * group + g], 0, 0)

    def v_map(g):
        def index(bi, s, pt):
            in_values = s >= steps
            row = jnp.where(in_values, bi, jnp.maximum(bi - 1, 0))
            col = jnp.where(in_values, s - steps, steps - 1) * group + g
            return (layer, pt[row, col], 0, 0)
        return index

    page = lambda m: pl.BlockSpec((None, None, SB_W, PAGE_SIZE), m)
    in_specs = [pl.BlockSpec((None, SB_W, 1), lambda bi, s, pt: (bi, 0, 0)),
                pl.BlockSpec((SB_HEADS, PAGE_SIZE), lambda bi, s, pt: (0, 0))]
    in_specs += [page(k_map(g)) for g in range(group)]
    in_specs += [page(v_map(g)) for g in range(group)]
    rows = n_pages * SB_HEADS
    return pl.pallas_call(
        functools.partial(_sb_decode_kernel, n_pages=n_pages, group=group),
        grid_spec=pltpu.PrefetchScalarGridSpec(
            num_scalar_prefetch=1,
            grid=(b, 2 * steps),
            in_specs=in_specs,
            out_specs=pl.BlockSpec((None, SB_W, 1), lambda bi, s, pt: (bi, 0, 0)),
            scratch_shapes=[pltpu.VMEM((rows, PAGE_SIZE), F32),
                            pltpu.VMEM((rows, PAGE_SIZE), F32),
                            pltpu.VMEM((SB_W, PAGE_SIZE), F32)],
        ),
        out_shape=jax.ShapeDtypeStruct((b, SB_W, 1), F32),
        compiler_params=_params(("parallel", "arbitrary")),
        name="sb_decode",
    )(page_table, q_col, bias_b, *([cache_kt] * group), *([cache_vt] * group))


def _lru_gates(xc, wa_ref, ba_ref, wi_ref, bi_ref, lam_ref):
    xb = xc.astype(BF16)
    r = jax.nn.sigmoid(_dot(xb, wa_ref[...]) + ba_ref[...])
    i = jax.nn.sigmoid(_dot(xb, wi_ref[...]) + bi_ref[...])
    log_a = -LRU_C * r * _softplus(-lam_ref[...])
    a = jnp.exp(log_a)
    t = jnp.tanh(log_a)
    one_minus_a2 = -2.0 * t / (1.0 - t)
    return a, jnp.sqrt(one_minus_a2) * (i * xc)


def _lru_prompt_kernel(x_ref, cw_ref, cb_ref, wa_ref, ba_ref, wi_ref, bi_ref, lam_ref,
                       o_ref, hl_ref, cn_ref, *, seq):
    x = x_ref[...]
    rows = lax.broadcasted_iota(jnp.int32, (seq, LRU_W), 0)
    xc = cb_ref[...] + cw_ref[LRU_CONV - 1:LRU_CONV, :] * x
    for j in range(1, LRU_CONV):
        shifted = jnp.where(rows >= j, pltpu.roll(x, j, 0), 0.0)
        xc = xc + cw_ref[LRU_CONV - 1 - j:LRU_CONV - j, :] * shifted
    a, g = _lru_gates(xc, wa_ref, ba_ref, wi_ref, bi_ref, lam_ref)
    step = 1
    while step < seq:
        valid = rows >= step
        a_prev = pltpu.roll(a, step, 0)
        g_prev = pltpu.roll(g, step, 0)
        g = jnp.where(valid, a * g_prev + g, g)
        a = jnp.where(valid, a * a_prev, a)
        step *= 2
    o_ref[...] = g
    hl_ref[...] = g[seq - 1:seq, :]
    cn_ref[...] = x[seq - (LRU_CONV - 1):, :]


def _lru_prompt(xl, cw, cb, wa, ba, wi, bi, lam, batch, seq):
    vec = _resident((1, LRU_W))
    mat = _resident((LRU_W, LRU_W))
    return pl.pallas_call(
        functools.partial(_lru_prompt_kernel, seq=seq),
        grid=(batch,),
        in_specs=[pl.BlockSpec((seq, LRU_W), lambda b: (b, 0)), _resident((LRU_CONV, LRU_W)), vec,
                  mat, vec, mat, vec, vec],
        out_specs=[pl.BlockSpec((seq, LRU_W), lambda b: (b, 0)),
                   pl.BlockSpec((None, 1, LRU_W), lambda b: (b, 0, 0)),
                   pl.BlockSpec((None, LRU_CONV - 1, LRU_W), lambda b: (b, 0, 0))],
        out_shape=[jax.ShapeDtypeStruct((batch * seq, LRU_W), F32),
                   jax.ShapeDtypeStruct((batch, 1, LRU_W), F32),
                   jax.ShapeDtypeStruct((batch, LRU_CONV - 1, LRU_W), F32)],
        compiler_params=_params(("parallel",)),
        name="lru_prompt",
    )(xl, cw, cb, wa, ba, wi, bi, lam)


def _block_mid_rows(bc, level):
    n_rows, width = bc.shape
    half = level // 2
    if level >= 2 * SUBLANES:
        return jnp.concatenate(
            [jnp.broadcast_to(bc[b * level + half - 1:b * level + half, :], (level, width))
             for b in range(n_rows // level)], axis=0)
    groups = n_rows // SUBLANES
    bc3 = bc.reshape(groups, SUBLANES, width)
    sub = lax.broadcasted_iota(jnp.int32, (groups, SUBLANES, width), 1)
    mid = None
    for b in range(SUBLANES // level):
        piece = jnp.broadcast_to(bc3[:, b * level + half - 1:b * level + half, :], bc3.shape)
        mid = piece if mid is None else jnp.where(_idiv(sub, level) == b, piece, mid)
    return mid.reshape(n_rows, width)


def _gla_chunk(qs, k, v, la, state):
    chunk = qs.shape[0]
    r_i = lax.broadcasted_iota(jnp.int32, (chunk, chunk), 0)
    c_i = lax.broadcasted_iota(jnp.int32, (chunk, chunk), 1)
    lower = (c_i <= r_i).astype(BF16)
    bc = _split_dot_left(lower, la, 3)
    rows = lax.broadcasted_iota(jnp.int32, (chunk, GLA_QK), 0)
    qk_head = _idiv(lax.broadcasted_iota(jnp.int32, (1, GLA_QK), 1), GLA_DK)
    v_head = _idiv(lax.broadcasted_iota(jnp.int32, (1, GLA_W), 1), GLA_DV)
    own = (_idiv(lax.broadcasted_iota(jnp.int32, (GLA_QK, GLA_W), 0), GLA_DK)
           == _idiv(lax.broadcasted_iota(jnp.int32, (GLA_QK, GLA_W), 1), GLA_DV))

    out = _dot((qs * jnp.exp(bc)).astype(BF16), state.astype(BF16))

    scores = [jnp.zeros((chunk, chunk), F32) for _ in range(GLA_HEADS)]
    level = 2
    while level <= chunk:
        half = level // 2
        mid = _block_mid_rows(bc, level)
        second = (rows & (level - 1)) >= half
        q_dec = jnp.where(second, qs * jnp.exp(bc - mid), 0.0)
        k_dec = jnp.where(second, 0.0, k * jnp.exp(mid - bc)).astype(BF16)
        same_block = _idiv(r_i, level) == _idiv(c_i, level)
        for h in range(GLA_HEADS):
            sc = _dot_nt(jnp.where(qk_head == h, q_dec, 0.0).astype(BF16), k_dec)
            if level < chunk:
                sc = jnp.where(same_block, sc, 0.0)
            scores[h] = scores[h] + sc
        level *= 2
    for h in range(GLA_HEADS):
        out = out + _dot(scores[h].astype(BF16), jnp.where(v_head == h, v, 0.0).astype(BF16))

    expand = jnp.where(own, 1.0, 0.0).astype(BF16)
    out = out + _split_dot(qs * k, expand, 2) * v

    last = bc[chunk - 1:chunk, :]
    kv = _dot_tn((k * jnp.exp(last - bc)).astype(BF16), v.astype(BF16))
    dec_col = jnp.transpose(jnp.broadcast_to(jnp.exp(last), (GLA_QK, GLA_QK)))
    dec = jnp.concatenate([dec_col] * (GLA_W // GLA_QK), axis=1)
    return out, dec * state + jnp.where(own, kv, 0.0)


def _gla_prompt_kernel(q_ref, k_ref, v_ref, la_ref, o_ref, sfin_ref, s_ref, *, chunk, n_steps):
    ci = pl.program_id(1)

    @pl.when(ci == 0)
    def _init():
        s_ref[...] = jnp.zeros_like(s_ref)

    state = s_ref[...]
    for u in range(q_ref.shape[0] // chunk):
        rows = slice(u * chunk, (u + 1) * chunk)
        out, state = _gla_chunk(q_ref[rows, :] * (GLA_DK ** -0.5), k_ref[rows, :], v_ref[rows, :],
                                la_ref[rows, :], state)
        o_ref[rows, :] = out
    s_ref[...] = state

    @pl.when(ci == n_steps - 1)
    def _emit():
        for h in range(GLA_HEADS):
            sfin_ref[h * GLA_DK:(h + 1) * GLA_DK, :] = (
                state[h * GLA_DK:(h + 1) * GLA_DK, h * GLA_DV:(h + 1) * GLA_DV])


def _gla_prompt(qg, kg, vg, la, batch, seq):
    chunk = min(GLA_CHUNK, seq)
    block = chunk * GLA_CHUNKS_PER_STEP if seq % (chunk * GLA_CHUNKS_PER_STEP) == 0 else chunk
    n_steps = seq // block
    row = lambda w: pl.BlockSpec((block, w), lambda b, c: (b * n_steps + c, 0))
    return pl.pallas_call(
        functools.partial(_gla_prompt_kernel, chunk=chunk, n_steps=n_steps),
        grid=(batch, n_steps),
        in_specs=[row(GLA_QK), row(GLA_QK), row(GLA_W), row(GLA_QK)],
        out_specs=[row(GLA_W), pl.BlockSpec((None, GLA_QK, GLA_DV), lambda b, c: (b, 0, 0))],
        out_shape=[jax.ShapeDtypeStruct((batch * seq, GLA_W), F32),
                   jax.ShapeDtypeStruct((batch, GLA_QK, GLA_DV), F32)],
        scratch_shapes=[pltpu.VMEM((GLA_QK, GLA_W), F32)],
        compiler_params=_params(("parallel", "arbitrary")),
        name="gla_prompt",
    )(qg, kg, vg, la)


def _step_kernel(x_ref, conv_ref, h0_ref, cw_ref, cb_ref, wa_ref, ba_ref, wi_ref, bi_ref, lam_ref,
                 qc_ref, kc_ref, lac_ref, v_ref, s0_ref,
                 h_ref, cn_ref, og_ref, s_ref):
    x = x_ref[...]
    xc = cb_ref[...] + cw_ref[LRU_CONV - 1:LRU_CONV, :] * x
    for j in range(LRU_CONV - 1):
        xc = xc + cw_ref[j:j + 1, :] * conv_ref[j]
    a, g = _lru_gates(xc, wa_ref, ba_ref, wi_ref, bi_ref, lam_ref)
    h_ref[...] = a * h0_ref[...] + g
    for j in range(LRU_CONV - 2):
        cn_ref[j] = conv_ref[j + 1]
    cn_ref[LRU_CONV - 2] = x

    alpha = jnp.exp(lac_ref[...])
    kc = kc_ref[...]
    qc = qc_ref[...] * (GLA_DK ** -0.5)
    for h in range(GLA_HEADS):
        rs = slice(h * GLA_DK, (h + 1) * GLA_DK)
        new = alpha[:, rs, :] * s0_ref[:, rs, :] + kc[:, rs, :] * v_ref[:, h:h + 1, :]
        s_ref[:, rs, :] = new
        og_ref[:, h:h + 1, :] = jnp.sum(qc[:, rs, :] * new, axis=1, keepdims=True)


def _step(xl, conv0, h0, cw, cb, wa, ba, wi, bi, lam, q_col, k_col, la_col, v3, s0):
    b = xl.shape[0]
    return pl.pallas_call(
        _step_kernel,
        out_shape=[jax.ShapeDtypeStruct((b, LRU_W), F32),
                   jax.ShapeDtypeStruct((LRU_CONV - 1, b, LRU_W), F32),
                   jax.ShapeDtypeStruct((b, GLA_HEADS, GLA_DV), F32),
                   jax.ShapeDtypeStruct((b, GLA_QK, GLA_DV), F32)],
        compiler_params=pltpu.CompilerParams(vmem_limit_bytes=VMEM_LIMIT_BYTES),
        name="sample_step",
    )(xl, conv0, h0, cw, cb, wa, ba, wi, bi, lam, q_col, k_col, la_col, v3, s0)


def _block_diag(w):
    nb, bi, bj = w.shape
    eye = jnp.eye(nb, dtype=w.dtype)
    return (eye[:, None, :, None] * w[:, :, None, :]).reshape(nb * bi, nb * bj)


def _row_tile(n):
    return min(ROW_TILE, n)


def kernel(x_prompt, x_sample, cache_k, cache_v, page_table, state_lru_h, state_lru_conv, state_gla,
           g_ffn1, w_ffn1_gate, w_ffn1_up, w_ffn1_down, g_mix, w_in, g_qnorm, g_knorm, sb_bias,
           conv_w, conv_b, lru_wa, lru_ba, lru_wi, lru_bi, lru_lambda, gla_w_alpha, gla_b_alpha,
           g_mix_out, w_out, g_ffn2, w_ffn2_gate, w_ffn2_up, w_ffn2_down):
    depth = w_in.shape[0]
    bp, seq, _ = x_prompt.shape
    bs, dec_seq, _ = x_sample.shape
    assert dec_seq == 1
    n_p = bp * seq
    n_phys = cache_k.shape[1]

    w_in_p = jnp.pad(w_in, ((0, 0), (0, 0), (0, D_IN_PAD - w_in.shape[2]))).astype(BF16)
    w_kvt = w_in[:, :, _OFF_K:_OFF_XL].transpose(0, 2, 1).astype(BF16)
    w_al = jnp.pad(gla_w_alpha, ((0, 0), (0, LANES - GLA_RANK), (0, 0))).astype(BF16)
    w_out_b = w_out.astype(BF16)
    wa_bd = jax.vmap(_block_diag)(lru_wa).astype(BF16)
    wi_bd = jax.vmap(_block_diag)(lru_wi).astype(BF16)
    gq = jnp.tile(g_qnorm, (1, SB_HEADS))[:, None, :]
    gk = jnp.tile(g_knorm, (1, SB_HEADS))[:, None, :]
    gk_col = jnp.tile(g_knorm, (1, SB_HEADS))[:, :, None]
    lane = jnp.arange(HEAD_MEAN_WIDTH)
    e64 = ((lane[:, None] // HEAD_DIM == lane[None, :] // HEAD_DIM).astype(F32) / HEAD_DIM).astype(BF16)
    vec = lambda t, l: t[l][None, :]

    cache_kt = cache_k.transpose(0, 1, 3, 4, 2).reshape(depth, n_phys, SB_W, PAGE_SIZE)
    cache_vt = cache_v.transpose(0, 1, 3, 4, 2).reshape(depth, n_phys, SB_W, PAGE_SIZE)

    xp = x_prompt.reshape(n_p, D_MODEL)
    xs = x_sample.reshape(bs, D_MODEL)
    tm_p = _row_tile(n_p)
    tm_s = _row_tile(bs)
    kt_all = jnp.zeros((depth, bp, SB_W, seq), F32)
    vt_all = jnp.zeros((depth, bp, SB_W, seq), F32)
    outs = {name: [] for name in ("ks", "vs", "hp", "hs", "cp", "cs", "sp", "ss")}

    for l in range(depth):
        lru_w = (conv_w[l], vec(conv_b, l), wa_bd[l], vec(lru_ba, l), wi_bd[l], vec(lru_bi, l),
                 vec(lru_lambda, l))
        inproj_w = (vec(g_mix, l), w_in_p[l], gq[l])
        inproj_w2 = (w_al[l], vec(gla_b_alpha, l), e64)

        ffn1_w = (vec(g_ffn1, l), w_ffn1_gate, w_ffn1_up, w_ffn1_down, l)
        ffn2_w = (vec(g_ffn2, l), w_ffn2_gate, w_ffn2_up, w_ffn2_down, l)
        xp, xs = _ffn(xp, xs, *ffn1_w, tm_p)

        q, kt_all, vt_all, xl, gate, qg, kg, vg, rg, la = _inproj(
            xp, *inproj_w, gk_col[l], *inproj_w2, tm_p, (kt_all, vt_all), l, w_kvt[l], seq)
        osb = _sb_prompt(q, kt_all, vt_all, sb_bias[l], l)
        olru, h_last, conv_new = _lru_prompt(xl, *lru_w, bp, seq)
        ogla, s_fin = _gla_prompt(qg, kg, vg, la, bp, seq)
        xp = _merge(xp, osb, olru, ogla, gate, rg, vec(g_mix_out, l), e64, w_out_b[l], tm_p)
        outs["hp"].append(h_last.reshape(bp, LRU_W))
        outs["cp"].append(conv_new)
        outs["sp"].append(s_fin.reshape(bp, GLA_HEADS, GLA_DK, GLA_DV))

        q, k_new, v_new, xl, gate, qg, kg, vg, rg, la = _inproj(xs, *inproj_w, gk[l], *inproj_w2, tm_s)
        bias_b = jnp.broadcast_to(sb_bias[l][:, None], (SB_HEADS, PAGE_SIZE))
        osb = _sb_decode(q.reshape(bs, SB_W, 1), bias_b, cache_kt, cache_vt, page_table, l)
        h_new, conv_new, ogla, s_new = _step(
            xl, state_lru_conv[l].transpose(1, 0, 2), state_lru_h[l], *lru_w,
            qg.reshape(bs, GLA_QK, 1), kg.reshape(bs, GLA_QK, 1), la.reshape(bs, GLA_QK, 1),
            vg.reshape(bs, GLA_HEADS, GLA_DV), state_gla[l].reshape(bs, GLA_QK, GLA_DV))
        xs = _merge(xs, osb.reshape(bs, SB_W), h_new, ogla.reshape(bs, GLA_W), gate, rg,
                    vec(g_mix_out, l), e64, w_out_b[l], tm_s)
        xp, xs = _ffn(xp, xs, *ffn2_w, tm_p)
        outs["ks"].append(k_new.reshape(bs, 1, SB_HEADS, HEAD_DIM))
        outs["vs"].append(v_new.reshape(bs, 1, SB_HEADS, HEAD_DIM))
        outs["hs"].append(h_new)
        outs["cs"].append(conv_new.transpose(1, 0, 2))
        outs["ss"].append(s_new.reshape(bs, GLA_HEADS, GLA_DK, GLA_DV))

    st = lambda name: jnp.stack(outs[name])
    return (xp.reshape(bp, seq, D_MODEL), xs.reshape(bs, 1, D_MODEL),
            kt_all.reshape(depth, bp, SB_HEADS, HEAD_DIM, seq).transpose(0, 1, 4, 2, 3),
            vt_all.reshape(depth, bp, SB_HEADS, HEAD_DIM, seq).transpose(0, 1, 4, 2, 3),
            st("ks"), st("vs"), st("hp"), st("hs"), st("cp"), st("cs"), st("sp"), st("ss"))
```
